```python
import math
import jax, jax.numpy as jnp
from jax import lax
import numpy as np

D_MODEL = 4096
BATCH = 4
SEQ = 2048
DEPTH = 2

HEAD_DIM = 128
DA_HEADS = 8
DA_V_DIM = 2 * HEAD_DIM
SW_Q_HEADS = 16
SW_KV_HEADS = 4
SW_GROUP = SW_Q_HEADS // SW_KV_HEADS
WINDOW = 128
Q_BLOCK = 128
ATTN_HEADS = DA_HEADS + SW_Q_HEADS
N_BUCKETS = 32
MAX_EXACT = N_BUCKETS // 2
MAX_DIST = 128
NEG = -1e30
DA_QK = DA_HEADS * 2 * HEAD_DIM
DA_V = DA_HEADS * DA_V_DIM
SW_Q = SW_Q_HEADS * HEAD_DIM
SW_KV = SW_KV_HEADS * HEAD_DIM
N_BRANCH = 2
GATE_W = N_BRANCH * D_MODEL
IN_WIDTHS = (DA_QK, DA_QK, DA_V, SW_Q, SW_KV, SW_KV, GATE_W)
IN_SPLITS = tuple(int(s) for s in np.cumsum(IN_WIDTHS)[:-1])
IN_TOTAL = sum(IN_WIDTHS)
PEER_HEADS = 8
N_KEYS = 128
N_EXPERTS = N_KEYS * N_KEYS
PEER_TOPK = 16
KEY_DIM = 128
PEER_QDIM = 2 * KEY_DIM
PEER_CHUNK = 128
PLE_DIM = 256

kernel_name = "hybrid_diffattn_swa_peer_block"


def rmsnorm(x, g, eps=1e-6):
    xf = x.astype(jnp.float32)
    y = xf * lax.rsqrt(jnp.mean(xf * xf, axis=-1, keepdims=True) + eps)
    return (y * g.astype(jnp.float32)).astype(x.dtype)


def t5_bucket(rel):
    n = jnp.maximum(rel, 0)
    nf = jnp.maximum(n, 1).astype(jnp.float32)
    large = MAX_EXACT + (jnp.log(nf / MAX_EXACT) / math.log(MAX_DIST / MAX_EXACT)
                         * (N_BUCKETS - MAX_EXACT)).astype(jnp.int32)
    large = jnp.minimum(large, N_BUCKETS - 1)
    return jnp.where(n < MAX_EXACT, n, large)


def diff_attention(q, k, v, pos, tab, lam_p, subln_g, lam_init):
    B, S = q.shape[0], q.shape[1]
    nb = S // Q_BLOCK
    lam_p = lam_p.astype(jnp.float32)
    lam = (jnp.exp(jnp.sum(lam_p[0] * lam_p[1])) - jnp.exp(jnp.sum(lam_p[2] * lam_p[3]))
           + lam_init)
    q = q * (HEAD_DIM ** -0.5)
    qb = q.reshape(B, nb, Q_BLOCK, DA_HEADS, 2, HEAD_DIM).transpose(1, 0, 2, 3, 4, 5)
    pq = pos.reshape(B, nb, Q_BLOCK).transpose(1, 0, 2)
    tab_t = tab.T.astype(jnp.float32)

    def block(args):
        qi, pqi = args
        s = jnp.einsum('bqhmd,bkhmd->bhmqk', qi, k).astype(jnp.float32)
        rel = pqi[:, :, None] - pos[:, None, :]
        bias = jnp.take(tab_t, t5_bucket(rel), axis=1).transpose(1, 0, 2, 3)
        s = jnp.where((rel >= 0)[:, None, None], s + bias[:, :, None], NEG)
        a = jax.nn.softmax(s, axis=-1)
        attn = a[:, :, 0] - lam * a[:, :, 1]
        return jnp.einsum('bhqk,bkhe->bqhe', attn.astype(v.dtype), v)

    o = lax.map(block, (qb, pq))
    o = o.transpose(1, 0, 2, 3, 4).reshape(B, S, DA_HEADS, DA_V_DIM)
    o = rmsnorm(o, subln_g) * (1.0 - lam_init)
    return o.reshape(B, S, DA_V)


def sliding_gqa_sinks(q, k, v, pos, tab, sinks):
    B, S = q.shape[0], q.shape[1]
    nb = S // WINDOW
    q = q * (HEAD_DIM ** -0.5)
    qb = q.reshape(B, nb, WINDOW, SW_KV_HEADS, SW_GROUP, HEAD_DIM)

    def band(t):
        tp = jnp.concatenate([jnp.zeros_like(t[:, :WINDOW]), t], axis=1)
        tp = tp.reshape((B, nb + 1, WINDOW) + t.shape[2:])
        return jnp.concatenate([tp[:, :-1], tp[:, 1:]], axis=2)

    kb, vb = band(k), band(v)
    posp = jnp.concatenate([jnp.full((B, WINDOW), -(2 ** 30), pos.dtype), pos], axis=1)
    posp = posp.reshape(B, nb + 1, WINDOW)
    pk = jnp.concatenate([posp[:, :-1], posp[:, 1:]], axis=2)
    pq = pos.reshape(B, nb, WINDOW)
    rel = pq[:, :, :, None] - pk[:, :, None, :]
    mask = (rel >= 0) & (rel < WINDOW)
    bias = jnp.take(tab.T.astype(jnp.float32), t5_bucket(rel), axis=1)
    bias = bias.reshape((SW_KV_HEADS, SW_GROUP) + rel.shape).transpose(2, 3, 0, 1, 4, 5)
    s = jnp.einsum('bnqhgd,bnkhd->bnhgqk', qb, kb).astype(jnp.float32) + bias
    s = jnp.where(mask[:, :, None, None], s, NEG)
    sk = sinks.astype(jnp.float32).reshape(SW_KV_HEADS, SW_GROUP)[None, None, :, :, None, None]
    m = jnp.maximum(jnp.max(s, axis=-1, keepdims=True), sk)
    e = jnp.exp(s - m)
    a = e / (jnp.sum(e, axis=-1, keepdims=True) + jnp.exp(sk - m))
    o = jnp.einsum('bnhgqk,bnkhd->bnqhgd', a.astype(v.dtype), vb)
    return o.reshape(B, S, SW_Q)


def peer(h, wq, k1, k2, u, v):
    B, S, D = h.shape
    T = B * S
    hf = h.reshape(T, D)
    q = (hf @ wq).reshape(T, PEER_HEADS, 2, KEY_DIM)
    s1 = jnp.einsum('thd,nd->thn', q[:, :, 0], k1).astype(jnp.float32)
    s2 = jnp.einsum('thd,nd->thn', q[:, :, 1], k2).astype(jnp.float32)
    v1, i1 = lax.top_k(s1, PEER_TOPK)
    v2, i2 = lax.top_k(s2, PEER_TOPK)
    cand = (v1[..., :, None] + v2[..., None, :]).reshape(T, PEER_HEADS, PEER_TOPK * PEER_TOPK)
    sc, ci = lax.top_k(cand, PEER_TOPK)
    e1 = jnp.take_along_axis(i1, ci // PEER_TOPK, axis=-1)
    e2 = jnp.take_along_axis(i2, ci % PEER_TOPK, axis=-1)
    idx = e1 * N_KEYS + e2
    g = jax.nn.softmax(sc, axis=-1)
    nc = T // PEER_CHUNK

    def chunk(args):
        xc, ic, gc = args
        a = jnp.einsum('cd,chkd->chk', xc, u[ic])
        w = gc.astype(xc.dtype) * jax.nn.gelu(a)
        return jnp.einsum('chk,chkd->cd', w, v[ic])

    out = lax.map(chunk, (hf.reshape(nc, PEER_CHUNK, D),
                          idx.reshape(nc, PEER_CHUNK, PEER_HEADS, PEER_TOPK),
                          g.reshape(nc, PEER_CHUNK, PEER_HEADS, PEER_TOPK)))
    return out.reshape(B, S, D)


def setup_inputs(seed: int = 0) -> dict:
    key = jax.random.key(seed)
    ks = jax.random.split(key, 24)
    f32 = jnp.float32
    nrm = lambda k, shape, scale: jax.random.normal(k, shape, f32) * scale
    gain = lambda k, shape: 1.0 + 0.05 * jax.random.normal(k, shape, f32)
    offs = jax.random.randint(ks[2], (BATCH, 1), 0, 1024, dtype=jnp.int32)
    positions = offs + jnp.arange(SEQ, dtype=jnp.int32)[None, :]
    return {
        "x": nrm(ks[0], (BATCH, SEQ, D_MODEL), 1.0),
        "p": nrm(ks[1], (DEPTH, BATCH, SEQ, PLE_DIM), 1.0),
        "positions": positions,
        "rel_bias": nrm(ks[3], (N_BUCKETS, ATTN_HEADS), 0.5),
        "norm_mix": gain(ks[4], (DEPTH, D_MODEL)),
        "w_in": nrm(ks[5], (DEPTH, D_MODEL, IN_TOTAL), D_MODEL ** -0.5),
        "da_lambda": nrm(ks[6], (DEPTH, 4, HEAD_DIM), 0.1),
        "da_subln": gain(ks[7], (DEPTH, DA_V_DIM)),
        "sw_sinks": nrm(ks[8], (DEPTH, SW_Q_HEADS), 0.5),
        "w_br_a": nrm(ks[9], (DEPTH, DA_V, D_MODEL), DA_V ** -0.5),
        "w_br_b": nrm(ks[10], (DEPTH, SW_Q, D_MODEL), SW_Q ** -0.5),
        "w_out": nrm(ks[11], (DEPTH, D_MODEL, D_MODEL), D_MODEL ** -0.5),
        "norm_ffn": gain(ks[12], (DEPTH, D_MODEL)),
        "peer_wq": nrm(ks[13], (DEPTH, D_MODEL, PEER_HEADS * PEER_QDIM), D_MODEL ** -0.5),
        "peer_k1": nrm(ks[14], (DEPTH, N_KEYS, KEY_DIM), KEY_DIM ** -0.5),
        "peer_k2": nrm(ks[15], (DEPTH, N_KEYS, KEY_DIM), KEY_DIM ** -0.5),
        "peer_u": nrm(ks[16], (DEPTH, N_EXPERTS, D_MODEL), D_MODEL ** -0.5),
        "peer_v": nrm(ks[17], (DEPTH, N_EXPERTS, D_MODEL), 0.3),
        "norm_ple": gain(ks[18], (DEPTH, D_MODEL)),
        "ple_gate": nrm(ks[19], (DEPTH, D_MODEL, D_MODEL), D_MODEL ** -0.5),
        "ple_proj": nrm(ks[20], (DEPTH, PLE_DIM, D_MODEL), PLE_DIM ** -0.5),
        "norm_final": gain(ks[21], (D_MODEL,)),
    }


def reference(x, p, positions, rel_bias, norm_mix, w_in, da_lambda, da_subln, sw_sinks,
              w_br_a, w_br_b, w_out, norm_ffn, peer_wq, peer_k1, peer_k2, peer_u, peer_v,
              norm_ple, ple_gate, ple_proj, norm_final):
    B, S, D = x.shape
    tab_da = rel_bias[:, :DA_HEADS]
    tab_sw = rel_bias[:, DA_HEADS:]
    for i in range(DEPTH):
        lam_init = 0.8 - 0.6 * math.exp(-0.3 * i)
        h = rmsnorm(x, norm_mix[i])
        proj = h @ w_in[i]
        q_da, k_da, v_da, q_sw, k_sw, v_sw, gates = jnp.split(proj, IN_SPLITS, axis=-1)
        o_a = diff_attention(q_da.reshape(B, S, DA_HEADS, 2, HEAD_DIM),
                             k_da.reshape(B, S, DA_HEADS, 2, HEAD_DIM),
                             v_da.reshape(B, S, DA_HEADS, DA_V_DIM),
                             positions, tab_da, da_lambda[i], da_subln[i], lam_init)
        o_b = sliding_gqa_sinks(q_sw.reshape(B, S, SW_Q_HEADS, HEAD_DIM),
                                k_sw.reshape(B, S, SW_KV_HEADS, HEAD_DIM),
                                v_sw.reshape(B, S, SW_KV_HEADS, HEAD_DIM),
                                positions, tab_sw, sw_sinks[i])
        gt = jax.nn.sigmoid(gates.reshape(B, S, N_BRANCH, D))
        merged = gt[:, :, 0] * (o_a @ w_br_a[i]) + gt[:, :, 1] * (o_b @ w_br_b[i])
        x = x + merged @ w_out[i]
        x = x + peer(rmsnorm(x, norm_ffn[i]), peer_wq[i], peer_k1[i], peer_k2[i],
                     peer_u[i], peer_v[i])
        x = x + jax.nn.sigmoid(rmsnorm(x, norm_ple[i]) @ ple_gate[i]) * (p[i] @ ple_proj[i])
    return rmsnorm(x, norm_final)
```

```python
import functools
import math

import jax
import jax.numpy as jnp
from jax import lax
from jax.experimental import pallas as pl
from jax.experimental.pallas import tpu as pltpu

F32 = jnp.float32
BF16 = jnp.bfloat16

HEAD_DIM = 128
DA_HEADS = 8
DA_V_DIM = 2 * HEAD_DIM
SW_Q_HEADS = 16
SW_KV_HEADS = 4
SW_GROUP = SW_Q_HEADS // SW_KV_HEADS
WINDOW = 128
N_BUCKETS = 32
MAX_EXACT = N_BUCKETS // 2
MAX_DIST = 128
NEG = -1e30
DA_QK = DA_HEADS * 2 * HEAD_DIM
DA_V = DA_HEADS * DA_V_DIM
SW_Q = SW_Q_HEADS * HEAD_DIM
SW_KV = SW_KV_HEADS * HEAD_DIM
QKV_WIDTH = 3 * DA_QK + SW_Q + 2 * SW_KV
PEER_HEADS = 8
N_KEYS = 128
PEER_TOPK = 16
KEY_DIM = 128
EPS = 1e-6

V7X_VMEM_REQUEST_CAP = 60 * 1024 * 1024
DA_BLK = 512


def _nbytes(shape, dtype):
    return math.prod(shape) * jnp.dtype(dtype).itemsize


def _params(semantics, block_bytes, scratch_bytes=0):
    need = int(1.25 * (2 * block_bytes + scratch_bytes)) + (4 << 20)
    return pltpu.CompilerParams(dimension_semantics=semantics,
                                vmem_limit_bytes=min(need, V7X_VMEM_REQUEST_CAP))


def _dot_nt(a, b):
    return lax.dot_general(a, b, (((1,), (1,)), ((), ())), preferred_element_type=F32)


def _rmsnorm_kernel(x_ref, g_ref, o_ref):
    x = x_ref[...]
    y = x * lax.rsqrt(jnp.mean(x * x, axis=-1, keepdims=True) + EPS)
    o_ref[...] = (y * g_ref[...]).astype(o_ref.dtype)


def _rmsnorm(x, g, out_dtype, rows=256):
    t, d = x.shape
    blocks = _nbytes((rows, d), F32) + _nbytes((rows, d), out_dtype)
    return pl.pallas_call(
        _rmsnorm_kernel,
        out_shape=jax.ShapeDtypeStruct((t, d), out_dtype),
        grid=(t // rows,),
        in_specs=[pl.BlockSpec((rows, d), lambda i: (i, 0)),
                  pl.BlockSpec((1, d), lambda i: (0, 0))],
        out_specs=pl.BlockSpec((rows, d), lambda i: (i, 0)),
        compiler_params=_params(("parallel",), blocks, _nbytes((rows, d), F32)),
        name="rmsnorm",
    )(x, g.reshape(1, d))


def _mm_scale_kernel(a_ref, b_ref, s_ref, o_ref):
    acc = jnp.dot(a_ref[...], b_ref[...], preferred_element_type=F32)
    o_ref[...] = (acc * s_ref[...]).astype(o_ref.dtype)


def _mm_sigmoid_kernel(a_ref, b_ref, o_ref):
    acc = jnp.dot(a_ref[...], b_ref[...], preferred_element_type=F32)
    o_ref[...] = jax.nn.sigmoid(acc).astype(o_ref.dtype)


def _mm_plain_kernel(a_ref, b_ref, o_ref):
    o_ref[...] = jnp.dot(a_ref[...], b_ref[...], preferred_element_type=F32).astype(o_ref.dtype)


def _mm_residual_kernel(a_ref, b_ref, x_ref, o_ref):
    o_ref[...] = x_ref[...] + jnp.dot(a_ref[...], b_ref[...], preferred_element_type=F32)


def _matmul(body, a, b, extra, extra_specs, out_dtype, bm, bn, name):
    m, k = a.shape
    n = b.shape[1]
    blocks = (_nbytes((bm, k), a.dtype) + _nbytes((k, bn), b.dtype) + _nbytes((bm, bn), out_dtype)
              + sum(_nbytes(s.block_shape, e.dtype) for s, e in zip(extra_specs, extra)))
    return pl.pallas_call(
        body,
        out_shape=jax.ShapeDtypeStruct((m, n), out_dtype),
        grid=(m // bm, n // bn),
        in_specs=[pl.BlockSpec((bm, k), lambda i, j: (i, 0)),
                  pl.BlockSpec((k, bn), lambda i, j: (0, j))] + list(extra_specs),
        out_specs=pl.BlockSpec((bm, bn), lambda i, j: (i, j)),
        compiler_params=_params(("parallel", "parallel"), blocks, _nbytes((bm, bn), F32)),
        name=name,
    )(a, b, *extra)


def _bias_kernel(tab_ref, o_ref, *, blk, window, rebase):
    h = pl.program_id(0)
    r = lax.broadcasted_iota(jnp.int32, (blk, blk), 0)
    c = lax.broadcasted_iota(jnp.int32, (blk, blk), 1)
    base = tab_ref[N_BUCKETS - 1, h] if rebase else 0.0
    for delta in (0, 1):
        rel = r - c + delta * blk
        n = jnp.maximum(rel, 0)
        nf = jnp.maximum(n, 1).astype(F32)
        large = MAX_EXACT + (jnp.log(nf / MAX_EXACT) / math.log(MAX_DIST / MAX_EXACT)
                             * (N_BUCKETS - MAX_EXACT)).astype(jnp.int32)
        large = jnp.minimum(large, N_BUCKETS - 1)
        bucket = jnp.where(n < MAX_EXACT, n, large)
        bias = jnp.zeros((blk, blk), F32)
        for b in range(N_BUCKETS):
            bias = jnp.where(bucket == b, tab_ref[b, h] - base, bias)
        mask = rel >= 0
        if window is not None:
            mask = mask & (rel < window)
        o_ref[0, delta] = jnp.where(mask, bias, NEG)


def _bias_tiles(tab, blk, window, rebase):
    heads = tab.shape[1]
    return pl.pallas_call(
        functools.partial(_bias_kernel, blk=blk, window=window, rebase=rebase),
        out_shape=jax.ShapeDtypeStruct((heads, 2, blk, blk), F32),
        grid=(heads,),
        in_specs=[pl.BlockSpec(memory_space=pltpu.SMEM)],
        out_specs=pl.BlockSpec((1, 2, blk, blk), lambda h: (h, 0, 0, 0)),
        compiler_params=_params(("parallel",), _nbytes((2, blk, blk), F32), 4 * _nbytes((blk, blk), F32)),
        name="bias_tiles",
    )(tab)


def _da_kernel(q_ref, k_ref, v_ref, bias_ref, lam_ref, g_ref, o_ref, m_ref, l_ref, acc_ref, *, lam_init):
    blk = q_ref.shape[0]
    qi = pl.program_id(2)
    q = q_ref[...]
    m_ref[...] = jnp.full(m_ref.shape, NEG, F32)
    l_ref[...] = jnp.zeros(l_ref.shape, F32)
    acc_ref[...] = jnp.zeros(acc_ref.shape, F32)

    def tile(kj, bias):
        start = pl.multiple_of(kj * blk, blk)
        k = k_ref[pl.ds(start, blk), :]
        v = v_ref[pl.ds(start, blk), :]
        for j in range(2):
            s = _dot_nt(q[:, j * HEAD_DIM:(j + 1) * HEAD_DIM], k[:, j * HEAD_DIM:(j + 1) * HEAD_DIM])
            if bias is not None:
                s = s + bias
            m_old = m_ref[j]
            m_new = jnp.maximum(m_old, jnp.max(s, axis=-1, keepdims=True))
            p = jnp.exp(s - m_new)
            alpha = jnp.exp(m_old - m_new)
            l_ref[j] = alpha * l_ref[j] + jnp.sum(p, axis=-1, keepdims=True)
            acc_ref[j] = alpha * acc_ref[j] + jnp.dot(p.astype(BF16), v, preferred_element_type=F32)
            m_ref[j] = m_new

    def far(kj, carry):
        tile(kj, None)
        return carry

    lax.fori_loop(0, jnp.maximum(qi - 1, 0), far, 0)

    @pl.when(qi >= 1)
    def _():
        tile(qi - 1, bias_ref[0, 1])

    tile(qi, bias_ref[0, 0])

    lp = lam_ref[...]
    lam = (jnp.exp(jnp.sum(lp[0:1] * lp[1:2], axis=-1, keepdims=True))
           - jnp.exp(jnp.sum(lp[2:3] * lp[3:4], axis=-1, keepdims=True)) + lam_init)
    o = acc_ref[0] / l_ref[0] - lam * (acc_ref[1] / l_ref[1])
    y = o * lax.rsqrt(jnp.mean(o * o, axis=-1, keepdims=True) + EPS)
    o_ref[...] = ((y * g_ref[...]) * (1.0 - lam_init)).astype(o_ref.dtype)


def _diff_attention(qkv, bias, lam_p, subln_g, lam_init, batch, seq):
    blk = DA_BLK
    nq = seq // blk
    blocks = (2 * _nbytes((blk, DA_V_DIM), BF16) + 2 * _nbytes((seq, DA_V_DIM), BF16)
              + _nbytes((2, blk, blk), F32))
    scratch = 2 * _nbytes((blk, DA_V_DIM), F32) + 4 * _nbytes((blk, 128), F32) + 4 * _nbytes((blk, blk), F32)
    k_col0 = DA_QK // DA_V_DIM
    v_col0 = 2 * DA_QK // DA_V_DIM
    return pl.pallas_call(
        functools.partial(_da_kernel, lam_init=lam_init),
        out_shape=jax.ShapeDtypeStruct((batch * seq, DA_V), BF16),
        grid=(batch, DA_HEADS, nq),
        in_specs=[pl.BlockSpec((blk, DA_V_DIM), lambda b, h, i: (b * nq + i, h)),
                  pl.BlockSpec((seq, DA_V_DIM), lambda b, h, i: (b, k_col0 + h)),
                  pl.BlockSpec((seq, DA_V_DIM), lambda b, h, i: (b, v_col0 + h)),
                  pl.BlockSpec((1, 2, blk, blk), lambda b, h, i: (h, 0, 0, 0)),
                  pl.BlockSpec((4, HEAD_DIM), lambda b, h, i: (0, 0)),
                  pl.BlockSpec((1, DA_V_DIM), lambda b, h, i: (0, 0))],
        out_specs=pl.BlockSpec((blk, DA_V_DIM), lambda b, h, i: (b * nq + i, h)),
        scratch_shapes=[pltpu.VMEM((2, blk, 1), F32), pltpu.VMEM((2, blk, 1), F32),
                        pltpu.VMEM((2, blk, DA_V_DIM), F32)],
        compiler_params=_params(("parallel", "parallel", "parallel"), blocks, scratch),
        name="diff_attention",
    )(qkv, qkv, qkv, bias, lam_p, subln_g.reshape(1, DA_V_DIM))


def _swa_kernel(sink_ref, q_ref, kc_ref, kp_ref, vc_ref, vp_ref, bias_ref, o_ref):
    n = pl.program_id(1)
    no_prev = jnp.where(n == 0, NEG, 0.0).astype(F32)
    for hk in range(SW_KV_HEADS):
        cols = slice(hk * HEAD_DIM, (hk + 1) * HEAD_DIM)
        kc, kp, vc, vp = kc_ref[:, cols], kp_ref[:, cols], vc_ref[:, cols], vp_ref[:, cols]
        for g in range(SW_GROUP):
            hq = hk * SW_GROUP + g
            qcols = slice(hq * HEAD_DIM, (hq + 1) * HEAD_DIM)
            qh = q_ref[:, qcols]
            sc = _dot_nt(qh, kc) + bias_ref[hq, 0]
            sp = _dot_nt(qh, kp) + bias_ref[hq, 1] + no_prev
            sink = sink_ref[hq]
            m = jnp.maximum(jnp.maximum(jnp.max(sc, axis=-1, keepdims=True),
                                        jnp.max(sp, axis=-1, keepdims=True)), sink)
            ec = jnp.exp(sc - m)
            ep = jnp.exp(sp - m)
            den = (jnp.sum(ec, axis=-1, keepdims=True) + jnp.sum(ep, axis=-1, keepdims=True)
                   + jnp.exp(sink - m))
            o = (jnp.dot(ec.astype(BF16), vc, preferred_element_type=F32)
                 + jnp.dot(ep.astype(BF16), vp, preferred_element_type=F32))
            o_ref[:, qcols] = (o / den).astype(o_ref.dtype)


def _sliding_attention(qkv, bias, sinks, batch, seq):
    nb = seq // WINDOW
    q_col = 3 * DA_QK // SW_Q
    k_col = (3 * DA_QK + SW_Q) // SW_KV
    v_col = k_col + 1
    cur = lambda b, n: b * nb + n
    prev = lambda b, n: b * nb + jnp.maximum(n - 1, 0)
    blocks = (2 * _nbytes((WINDOW, SW_Q), BF16) + 4 * _nbytes((WINDOW, SW_KV), BF16)
              + _nbytes((SW_Q_HEADS, 2, WINDOW, WINDOW), F32))
    return pl.pallas_call(
        _swa_kernel,
        out_shape=jax.ShapeDtypeStruct((batch * seq, SW_Q), BF16),
        grid=(batch, nb),
        in_specs=[pl.BlockSpec(memory_space=pltpu.SMEM),
                  pl.BlockSpec((WINDOW, SW_Q), lambda b, n: (cur(b, n), q_col)),
                  pl.BlockSpec((WINDOW, SW_KV), lambda b, n: (cur(b, n), k_col)),
                  pl.BlockSpec((WINDOW, SW_KV), lambda b, n: (prev(b, n), k_col)),
                  pl.BlockSpec((WINDOW, SW_KV), lambda b, n: (cur(b, n), v_col)),
                  pl.BlockSpec((WINDOW, SW_KV), lambda b, n: (prev(b, n), v_col)),
                  pl.BlockSpec((SW_Q_HEADS, 2, WINDOW, WINDOW), lambda b, n: (0, 0, 0, 0))],
        out_specs=pl.BlockSpec((WINDOW, SW_Q), lambda b, n: (cur(b, n), 0)),
        compiler_params=_params(("parallel", "parallel"), blocks, 16 * _nbytes((WINDOW, WINDOW), F32)),
        name="sliding_attention",
    )(sinks, qkv, qkv, qkv, qkv, qkv, bias)


def _merge_kernel(oa_ref, ob_ref, wa_ref, wb_ref, ga_ref, gb_ref, o_ref):
    a = jnp.dot(oa_ref[...], wa_ref[...], preferred_element_type=F32)
    b = jnp.dot(ob_ref[...], wb_ref[...], preferred_element_type=F32)
    o_ref[...] = (ga_ref[...] * a + gb_ref[...] * b).astype(o_ref.dtype)


def _merge(o_a, o_b, w_a, w_b, gates, bm=1024, bn=1024):
    t, ka = o_a.shape
    kb = o_b.shape[1]
    d = w_a.shape[1]
    nj = d // bn
    blocks = (_nbytes((bm, ka), BF16) + _nbytes((bm, kb), BF16) + _nbytes((ka, bn), BF16)
              + _nbytes((kb, bn), BF16) + 2 * _nbytes((bm, bn), gates.dtype) + _nbytes((bm, bn), BF16))
    return pl.pallas_call(
        _merge_kernel,
        out_shape=jax.ShapeDtypeStruct((t, d), BF16),
        grid=(t // bm, nj),
        in_specs=[pl.BlockSpec((bm, ka), lambda i, j: (i, 0)),
                  pl.BlockSpec((bm, kb), lambda i, j: (i, 0)),
                  pl.BlockSpec((ka, bn), lambda i, j: (0, j)),
                  pl.BlockSpec((kb, bn), lambda i, j: (0, j)),
                  pl.BlockSpec((bm, bn), lambda i, j: (i, j)),
                  pl.BlockSpec((bm, bn), lambda i, j: (i, nj + j))],
        out_specs=pl.BlockSpec((bm, bn), lambda i, j: (i, j)),
        compiler_params=_params(("parallel", "parallel"), blocks, 2 * _nbytes((bm, bn), F32)),
        name="merge",
    )(o_a, o_b, w_a, w_b, gates, gates)


def _sort_pairs(n):
    pairs = []

    def merge(lo, hi, r):
        step = r * 2
        if step < hi - lo:
            merge(lo, hi, step)
            merge(lo + r, hi, step)
            pairs.extend((i, i + r) for i in range(lo + r, hi - r, step))
        else:
            pairs.append((lo, lo + r))

    def sort(lo, hi):
        if hi - lo >= 1:
            mid = lo + (hi - lo) // 2
            sort(lo, mid)
            sort(mid + 1, hi)
            merge(lo, hi, 1)

    sort(0, n - 1)
    return pairs


_SORT16 = _sort_pairs(PEER_TOPK)


def _sort_desc(xs):
    xs = list(xs)
    for i, j in _SORT16:
        xs[i], xs[j] = jnp.maximum(xs[i], xs[j]), jnp.minimum(xs[i], xs[j])
    return xs


def _merge_top(a, b):
    k = PEER_TOPK
    xs = [jnp.maximum(a[i], b[k - 1 - i]) for i in range(k)]
    d = k // 2
    while d >= 1:
        for i in range(k):
            if not i & d:
                xs[i], xs[i + d] = jnp.maximum(xs[i], xs[i + d]), jnp.minimum(xs[i], xs[i + d])
        d //= 2
    return xs


def _top16_over_rows(s):
    groups = [s[a * 8:(a + 1) * 8, :] for a in range(s.shape[0] // 8)]
    xs = _sort_desc(groups)
    for shift in (4, 2, 1):
        xs = _merge_top(xs, [pltpu.roll(x, shift, 0) for x in xs])
    return xs


def _route_kernel(q_ref, k1_ref, k2_ref, s1_ref, s2_ref, e1_ref, e2_ref, tau_ref):
    q = q_ref[...]
    s1 = _dot_nt(k1_ref[...], q[:, :KEY_DIM])
    s2 = _dot_nt(k2_ref[...], q[:, KEY_DIM:])
    v1 = _top16_over_rows(s1)
    v2 = _top16_over_rows(s2)
    k = PEER_TOPK
    top = [v1[0] + v2[b] for b in range(k)]
    rest = [v1[a] + v2[b] for a in range(1, k) for b in range(k) if (a + 1) * (b + 1) <= k]
    pad = jnp.full(top[0].shape, -jnp.inf, F32)
    rest = rest + [pad] * (-len(rest) % k)
    for g in range(len(rest) // k):
        top = _merge_top(top, _sort_desc(rest[g * k:(g + 1) * k]))
    z = jnp.ones_like(top[0])
    for c in top[1:]:
        z = z + jnp.exp(c - top[0])
    s1_ref[0] = s1
    s2_ref[0] = s2
    e1_ref[0] = jnp.exp(s1 - v1[0][0:1]) / z[0:1]
    e2_ref[0] = jnp.exp(s2 - v2[0][0:1])
    tau_ref[0] = top[k - 1][0:1]


def _route(q, k1, k2, tm=1024):
    t = q.shape[0]
    big = jax.ShapeDtypeStruct((PEER_HEADS, N_KEYS, t), F32)
    big_spec = pl.BlockSpec((1, N_KEYS, tm), lambda i, h: (h, 0, i))
    blocks = _nbytes((tm, 2 * KEY_DIM), BF16) + 4 * _nbytes((N_KEYS, tm), F32)
    return pl.pallas_call(
        _route_kernel,
        out_shape=(big, big, big, big, jax.ShapeDtypeStruct((PEER_HEADS, 1, t), F32)),
        grid=(t // tm, PEER_HEADS),
        in_specs=[pl.BlockSpec((tm, 2 * KEY_DIM), lambda i, h: (i, h)),
                  pl.BlockSpec((N_KEYS, KEY_DIM), lambda i, h: (0, 0)),
                  pl.BlockSpec((N_KEYS, KEY_DIM), lambda i, h: (0, 0))],
        out_specs=(big_spec, big_spec, big_spec, big_spec,
                   pl.BlockSpec((1, 1, tm), lambda i, h: (h, 0, i))),
        compiler_params=_params(("parallel", "parallel"), blocks, 24 * _nbytes((N_KEYS, tm), F32)),
        name="peer_route",
    )(q, k1, k2)


def _peer_up_kernel(u_ref, h_ref, s1_ref, e1_ref, s2_ref, e2_ref, tau_ref, o_ref):
    act = _dot_nt(u_ref[...], h_ref[...])
    for r in range(u_ref.shape[0] // N_KEYS):
        gate = None
        for h in range(PEER_HEADS):
            score = s2_ref[h] + s1_ref[h, r:r + 1, :]
            term = jnp.where(score >= tau_ref[h], e2_ref[h] * e1_ref[h, r:r + 1, :], 0.0)
            gate = term if gate is None else gate + term
        rows = slice(r * N_KEYS, (r + 1) * N_KEYS)
        o_ref[rows, :] = (gate * jax.nn.gelu(act[rows, :])).astype(o_ref.dtype)


def _peer_up(u, hn, s1, e1, s2, e2, tau, te=1024, tm=512):
    n_exp, d = u.shape
    t = hn.shape[0]
    rows = te // N_KEYS
    row_spec = pl.BlockSpec((PEER_HEADS, rows, tm), lambda i, j: (0, j, i))
    full_spec = pl.BlockSpec((PEER_HEADS, N_KEYS, tm), lambda i, j: (0, 0, i))
    blocks = (_nbytes((te, d), BF16) + _nbytes((tm, d), BF16) + 2 * _nbytes((PEER_HEADS, N_KEYS, tm), F32)
              + 2 * _nbytes((PEER_HEADS, rows, tm), F32) + _nbytes((te, tm), BF16))
    return pl.pallas_call(
        _peer_up_kernel,
        out_shape=jax.ShapeDtypeStruct((n_exp, t), BF16),
        grid=(t // tm, n_exp // te),
        in_specs=[pl.BlockSpec((te, d), lambda i, j: (j, 0)),
                  pl.BlockSpec((tm, d), lambda i, j: (i, 0)),
                  row_spec, row_spec, full_spec, full_spec,
                  pl.BlockSpec((PEER_HEADS, 1, tm), lambda i, j: (0, 0, i))],
        out_specs=pl.BlockSpec((te, tm), lambda i, j: (j, i)),
        compiler_params=_params(("parallel", "parallel"), blocks, 2 * _nbytes((te, tm), F32)),
        name="peer_up",
    )(u, hn, s1, e1, s2, e2, tau)


def _peer_down_kernel(vt_ref, w_ref, x_ref, o_ref, acc_ref):
    kk = pl.program_id(2)

    @pl.when(kk == 0)
    def _():
        acc_ref[...] = jnp.zeros(acc_ref.shape, F32)

    acc_ref[...] += jnp.dot(vt_ref[...], w_ref[...], preferred_element_type=F32)

    @pl.when(kk == pl.num_programs(2) - 1)
    def _():
        o_ref[...] = x_ref[...] + acc_ref[...].T


def _peer_down(vt, wt, x, bd=1024, bt=1024, tk=2048):
    d, n_exp = vt.shape
    t = wt.shape[1]
    blocks = (_nbytes((bd, tk), BF16) + _nbytes((tk, bt), BF16) + 2 * _nbytes((bt, bd), F32))
    return pl.pallas_call(
        _peer_down_kernel,
        out_shape=jax.ShapeDtypeStruct((t, d), F32),
        grid=(d // bd, t // bt, n_exp // tk),
        in_specs=[pl.BlockSpec((bd, tk), lambda i, j, k: (i, k)),
                  pl.BlockSpec((tk, bt), lambda i, j, k: (k, j)),
                  pl.BlockSpec((bt, bd), lambda i, j, k: (j, i))],
        out_specs=pl.BlockSpec((bt, bd), lambda i, j, k: (j, i)),
        scratch_shapes=[pltpu.VMEM((bd, bt), F32)],
        compiler_params=_params(("parallel", "parallel", "arbitrary"), blocks, 3 * _nbytes((bd, bt), F32)),
        name="peer_down",
    )(vt, wt, x)


def _ple_kernel(h_ref, wg_ref, p_ref, wp_ref, x_ref, o_ref):
    gate = jax.nn.sigmoid(jnp.dot(h_ref[...], wg_ref[...], preferred_element_type=F32))
    emb = jnp.dot(p_ref[...], wp_ref[...], preferred_element_type=F32)
    o_ref[...] = x_ref[...] + gate * emb


def _ple(hp, w_gate, p, w_proj, x, bm=512, bn=1024):
    t, d = hp.shape
    pd = p.shape[1]
    n = w_gate.shape[1]
    blocks = (_nbytes((bm, d), BF16) + _nbytes((d, bn), BF16) + _nbytes((bm, pd), BF16)
              + _nbytes((pd, bn), BF16) + 2 * _nbytes((bm, bn), F32))
    return pl.pallas_call(
        _ple_kernel,
        out_shape=jax.ShapeDtypeStruct((t, n), F32),
        grid=(t // bm, n // bn),
        in_specs=[pl.BlockSpec((bm, d), lambda i, j: (i, 0)),
                  pl.BlockSpec((d, bn), lambda i, j: (0, j)),
                  pl.BlockSpec((bm, pd), lambda i, j: (i, 0)),
                  pl.BlockSpec((pd, bn), lambda i, j: (0, j)),
                  pl.BlockSpec((bm, bn), lambda i, j: (i, j))],
        out_specs=pl.BlockSpec((bm, bn), lambda i, j: (i, j)),
        compiler_params=_params(("parallel", "parallel"), blocks, 2 * _nbytes((bm, bn), F32)),
        name="ple",
    )(hp, w_gate, p, w_proj, x)


def _qkv_column_scale():
    s = HEAD_DIM ** -0.5
    parts = [(DA_QK, s), (DA_QK, 1.0), (DA_V, 1.0), (SW_Q, s), (SW_KV, 1.0), (SW_KV, 1.0)]
    return jnp.concatenate([jnp.full((1, w), v, F32) for w, v in parts], axis=1)


@jax.jit
def kernel(x, p, positions, rel_bias, norm_mix, w_in, da_lambda, da_subln, sw_sinks, w_br_a, w_br_b, w_out,
           norm_ffn, peer_wq, peer_k1, peer_k2, peer_u, peer_v, norm_ple, ple_gate, ple_proj, norm_final):
    del positions
    batch, seq, d = x.shape
    t = batch * seq
    depth = w_in.shape[0]
    xf = x.reshape(t, d)
    da_bias = _bias_tiles(rel_bias[:, :DA_HEADS], DA_BLK, None, True)
    sw_bias = _bias_tiles(rel_bias[:, DA_HEADS:], WINDOW, WINDOW, False)
    col_scale = _qkv_column_scale()
    tile = pl.BlockSpec((1, 1024), lambda i, j: (0, j))
    for i in range(depth):
        lam_init = 0.8 - 0.6 * math.exp(-0.3 * i)
        h = _rmsnorm(xf, norm_mix[i], BF16)
        qkv = _matmul(_mm_scale_kernel, h, w_in[i, :, :QKV_WIDTH].astype(BF16), [col_scale], [tile],
                      BF16, 1024, 1024, "proj_qkv")
        gates = _matmul(_mm_sigmoid_kernel, h, w_in[i, :, QKV_WIDTH:].astype(BF16), [], [],
                        F32, 1024, 1024, "proj_gates")
        o_a = _diff_attention(qkv, da_bias, da_lambda[i], da_subln[i], lam_init, batch, seq)
        o_b = _sliding_attention(qkv, sw_bias, sw_sinks[i], batch, seq)
        merged = _merge(o_a, o_b, w_br_a[i].astype(BF16), w_br_b[i].astype(BF16), gates)
        xf = _matmul(_mm_residual_kernel, merged, w_out[i].astype(BF16), [xf],
                     [pl.BlockSpec((512, 1024), lambda i, j: (i, j))], F32, 512, 1024, "proj_out")
        hn = _rmsnorm(xf, norm_ffn[i], BF16)
        q = _matmul(_mm_plain_kernel, hn, peer_wq[i].astype(BF16), [], [], BF16, 1024, 1024, "peer_query")
        s1, s2, e1, e2, tau = _route(q, peer_k1[i].astype(BF16), peer_k2[i].astype(BF16))
        wt = _peer_up(peer_u[i].astype(BF16), hn, s1, e1, s2, e2, tau)
        xf = _peer_down(peer_v[i].T.astype(BF16), wt, xf)
        hp = _rmsnorm(xf, norm_ple[i], BF16)
        xf = _ple(hp, ple_gate[i].astype(BF16), p[i].reshape(t, -1).astype(BF16), ple_proj[i].astype(BF16), xf)
    return _rmsnorm(xf, norm_final, F32).reshape(batch, seq, d)
```

```python
import functools
import math

import jax
import jax.numpy as jnp
from jax import lax
from jax.experimental import pallas as pl
from jax.experimental.pallas import tpu as pltpu

F32 = jnp.float32
BF16 = jnp.bfloat16

HEAD_DIM = 128
DA_HEADS = 8
DA_V_DIM = 2 * HEAD_DIM
SW_Q_HEADS = 16
SW_KV_HEADS = 4
SW_GROUP = SW_Q_HEADS // SW_KV_HEADS
WINDOW = 128
N_BUCKETS = 32
MAX_EXACT = N_BUCKETS // 2
MAX_DIST = 128
NEG = -1e30
DA_QK = DA_HEADS * 2 * HEAD_DIM
DA_V = DA_HEADS * DA_V_DIM
SW_Q = SW_Q_HEADS * HEAD_DIM
SW_KV = SW_KV_HEADS * HEAD_DIM
QKV_WIDTH = 3 * DA_QK + SW_Q + 2 * SW_KV
PEER_HEADS = 8
N_KEYS = 128
PEER_TOPK = 16
KEY_DIM = 128
EPS = 1e-6

V7X_VMEM_REQUEST_CAP = 60 * 1024 * 1024
DA_BLK = 512
PEER_UP_TE = 1024
PEER_UP_CHUNKS = PEER_UP_TE // N_KEYS
PEER_UP_UNROLL = 4


def _nbytes(shape, dtype):
    return math.prod(shape) * jnp.dtype(dtype).itemsize


def _params(semantics, block_bytes, scratch_bytes=0, flags=None):
    need = int(1.25 * (2 * block_bytes + scratch_bytes)) + (4 << 20)
    return pltpu.CompilerParams(dimension_semantics=semantics,
                                vmem_limit_bytes=min(need, V7X_VMEM_REQUEST_CAP), flags=flags)


def _dot_nt(a, b):
    return lax.dot_general(a, b, (((1,), (1,)), ((), ())), preferred_element_type=F32)


def _rmsnorm_kernel(x_ref, g_ref, o_ref):
    x = x_ref[...]
    y = x * lax.rsqrt(jnp.mean(x * x, axis=-1, keepdims=True) + EPS)
    o_ref[...] = (y * g_ref[...]).astype(o_ref.dtype)


def _rmsnorm(x, g, out_dtype, rows=256):
    t, d = x.shape
    blocks = _nbytes((rows, d), F32) + _nbytes((rows, d), out_dtype)
    return pl.pallas_call(
        _rmsnorm_kernel,
        out_shape=jax.ShapeDtypeStruct((t, d), out_dtype),
        grid=(t // rows,),
        in_specs=[pl.BlockSpec((rows, d), lambda i: (i, 0)),
                  pl.BlockSpec((1, d), lambda i: (0, 0))],
        out_specs=pl.BlockSpec((rows, d), lambda i: (i, 0)),
        compiler_params=_params(("parallel",), blocks, _nbytes((rows, d), F32)),
        name="rmsnorm",
    )(x, g.reshape(1, d))


def _rmsnorm_chunked_kernel(x_ref, g_ref, o_ref, oc_ref):
    x = x_ref[...]
    y = (x * lax.rsqrt(jnp.mean(x * x, axis=-1, keepdims=True) + EPS) * g_ref[...]).astype(o_ref.dtype)
    o_ref[...] = y
    dc = oc_ref.shape[2]
    for k in range(oc_ref.shape[0]):
        oc_ref[k] = y[:, k * dc:(k + 1) * dc]


def _rmsnorm_chunked(x, g, chunks, rows=256):
    t, d = x.shape
    dc = d // chunks
    blocks = _nbytes((rows, d), F32) + 2 * _nbytes((rows, d), BF16)
    return pl.pallas_call(
        _rmsnorm_chunked_kernel,
        out_shape=(jax.ShapeDtypeStruct((t, d), BF16), jax.ShapeDtypeStruct((chunks, t, dc), BF16)),
        grid=(t // rows,),
        in_specs=[pl.BlockSpec((rows, d), lambda i: (i, 0)),
                  pl.BlockSpec((1, d), lambda i: (0, 0))],
        out_specs=(pl.BlockSpec((rows, d), lambda i: (i, 0)),
                   pl.BlockSpec((chunks, rows, dc), lambda i: (0, i, 0))),
        compiler_params=_params(("parallel",), blocks, _nbytes((rows, d), F32)),
        name="rmsnorm_chunked",
    )(x, g.reshape(1, d))


CAST_BLOCK_BYTES = 8 << 20


def _cast_kernel(x_ref, o_ref):
    o_ref[...] = x_ref[...].astype(o_ref.dtype)


def _cast_chunked_kernel(x_ref, o_ref):
    dc = o_ref.shape[2]
    for k in range(o_ref.shape[0]):
        o_ref[k] = x_ref[:, k * dc:(k + 1) * dc].astype(o_ref.dtype)


def _cast_transposed_kernel(x_ref, o_ref):
    o_ref[...] = x_ref[...].T.astype(o_ref.dtype)


def _to_bf16(w, layer, col0=0, ncols=None, bc=None, layout="plain", chunks=None):
    _, r, c = w.shape
    ncols = c if ncols is None else ncols
    bc = ncols if bc is None else bc
    br = min(r, CAST_BLOCK_BYTES // (bc * 4))
    grid = (r // br, ncols // bc)
    in_spec = pl.BlockSpec((None, br, bc), lambda i, j: (layer, i, col0 // bc + j))
    if layout == "plain":
        body, shape = _cast_kernel, (r, ncols)
        out_spec = pl.BlockSpec((br, bc), lambda i, j: (i, j))
    elif layout == "chunked":
        assert bc == ncols
        body, shape = _cast_chunked_kernel, (chunks, r, ncols // chunks)
        out_spec = pl.BlockSpec((chunks, br, ncols // chunks), lambda i, j: (0, i, 0))
    else:
        assert bc == ncols
        body, shape = _cast_transposed_kernel, (ncols, r)
        out_spec = pl.BlockSpec((ncols, br), lambda i, j: (0, i))
    return pl.pallas_call(
        body,
        out_shape=jax.ShapeDtypeStruct(shape, BF16),
        grid=grid,
        in_specs=[in_spec],
        out_specs=out_spec,
        compiler_params=_params(("parallel", "parallel"), _nbytes((br, bc), F32) + _nbytes((br, bc), BF16),
                                _nbytes((br, bc), F32)),
        name="to_bf16_" + layout,
    )(w)


def _mm_scale_kernel(a_ref, b_ref, s_ref, o_ref):
    acc = jnp.dot(a_ref[...], b_ref[...], preferred_element_type=F32)
    o_ref[...] = (acc * s_ref[...]).astype(o_ref.dtype)


def _mm_sigmoid_kernel(a_ref, b_ref, o_ref):
    acc = jnp.dot(a_ref[...], b_ref[...], preferred_element_type=F32)
    o_ref[...] = jax.nn.sigmoid(acc).astype(o_ref.dtype)


def _mm_plain_kernel(a_ref, b_ref, o_ref):
    o_ref[...] = jnp.dot(a_ref[...], b_ref[...], preferred_element_type=F32).astype(o_ref.dtype)


def _mm_residual_kernel(a_ref, b_ref, x_ref, o_ref):
    o_ref[...] = x_ref[...] + jnp.dot(a_ref[...], b_ref[...], preferred_element_type=F32)


def _matmul(body, a, b, extra, extra_specs, out_dtype, bm, bn, name):
    m, k = a.shape
    n = b.shape[1]
    blocks = (_nbytes((bm, k), a.dtype) + _nbytes((k, bn), b.dtype) + _nbytes((bm, bn), out_dtype)
              + sum(_nbytes(s.block_shape, e.dtype) for s, e in zip(extra_specs, extra)))
    return pl.pallas_call(
        body,
        out_shape=jax.ShapeDtypeStruct((m, n), out_dtype),
        grid=(m // bm, n // bn),
        in_specs=[pl.BlockSpec((bm, k), lambda i, j: (i, 0)),
                  pl.BlockSpec((k, bn), lambda i, j: (0, j))] + list(extra_specs),
        out_specs=pl.BlockSpec((bm, bn), lambda i, j: (i, j)),
        compiler_params=_params(("parallel", "parallel"), blocks, _nbytes((bm, bn), F32)),
        name=name,
    )(a, b, *extra)


def _bias_kernel(tab_ref, o_ref, *, blk, window, rebase):
    h = pl.program_id(0)
    r = lax.broadcasted_iota(jnp.int32, (blk, blk), 0)
    c = lax.broadcasted_iota(jnp.int32, (blk, blk), 1)
    base = tab_ref[N_BUCKETS - 1, h] if rebase else 0.0
    for delta in (0, 1):
        rel = r - c + delta * blk
        n = jnp.maximum(rel, 0)
        nf = jnp.maximum(n, 1).astype(F32)
        large = MAX_EXACT + (jnp.log(nf / MAX_EXACT) / math.log(MAX_DIST / MAX_EXACT)
                             * (N_BUCKETS - MAX_EXACT)).astype(jnp.int32)
        large = jnp.minimum(large, N_BUCKETS - 1)
        bucket = jnp.where(n < MAX_EXACT, n, large)
        bias = jnp.zeros((blk, blk), F32)
        for b in range(N_BUCKETS):
            bias = jnp.where(bucket == b, tab_ref[b, h] - base, bias)
        mask = rel >= 0
        if window is not None:
            mask = mask & (rel < window)
        o_ref[0, delta] = jnp.where(mask, bias, NEG)


def _bias_tiles(tab, blk, window, rebase):
    heads = tab.shape[1]
    return pl.pallas_call(
        functools.partial(_bias_kernel, blk=blk, window=window, rebase=rebase),
        out_shape=jax.ShapeDtypeStruct((heads, 2, blk, blk), F32),
        grid=(heads,),
        in_specs=[pl.BlockSpec(memory_space=pltpu.SMEM)],
        out_specs=pl.BlockSpec((1, 2, blk, blk), lambda h: (h, 0, 0, 0)),
        compiler_params=_params(("parallel",), _nbytes((2, blk, blk), F32), 4 * _nbytes((blk, blk), F32)),
        name="bias_tiles",
    )(tab)


def _da_kernel(q_ref, k_ref, v_ref, bias_ref, lam_ref, g_ref, o_ref, *, lam_init):
    blk = q_ref.shape[0]
    qi = pl.program_id(2)
    lp = lam_ref[...]
    lam = (jnp.exp(jnp.sum(lp[0:1] * lp[1:2], axis=-1, keepdims=True))
           - jnp.exp(jnp.sum(lp[2:3] * lp[3:4], axis=-1, keepdims=True)) + lam_init)

    def softmax_pv(j, case):
        dims = slice(j * HEAD_DIM, (j + 1) * HEAD_DIM)
        q = q_ref[:, dims]
        spans = [(slice(case * blk, (case + 1) * blk), bias_ref[0, 0])]
        if case >= 1:
            spans.append((slice((case - 1) * blk, case * blk), bias_ref[0, 1]))
        if case >= 2:
            spans.append((slice(0, (case - 1) * blk), None))
        scores = []
        for rows, bias in spans:
            s = _dot_nt(q, k_ref[rows, dims])
            scores.append(s if bias is None else s + bias)
        m = functools.reduce(jnp.maximum, [jnp.max(s, axis=-1, keepdims=True) for s in scores])
        probs = [jnp.exp(s - m) for s in scores]
        norm = sum(jnp.sum(p, axis=-1, keepdims=True) for p in probs)
        out = sum(jnp.dot(p.astype(BF16), v_ref[rows, :], preferred_element_type=F32)
                  for p, (rows, _) in zip(probs, spans))
        return out / norm

    for case in range(k_ref.shape[0] // blk):
        @pl.when(qi == case)
        def _(case=case):
            o = softmax_pv(0, case) - lam * softmax_pv(1, case)
            y = o * lax.rsqrt(jnp.mean(o * o, axis=-1, keepdims=True) + EPS)
            o_ref[...] = ((y * g_ref[...]) * (1.0 - lam_init)).astype(o_ref.dtype)


def _diff_attention(qkv, bias, lam_p, subln_g, lam_init, batch, seq):
    blk = DA_BLK
    nq = seq // blk
    blocks = (2 * _nbytes((blk, DA_V_DIM), BF16) + 2 * _nbytes((seq, DA_V_DIM), BF16)
              + _nbytes((2, blk, blk), F32))
    scratch = 6 * _nbytes((blk, seq), F32)
    k_col0 = DA_QK // DA_V_DIM
    v_col0 = 2 * DA_QK // DA_V_DIM
    return pl.pallas_call(
        functools.partial(_da_kernel, lam_init=lam_init),
        out_shape=jax.ShapeDtypeStruct((batch * seq, DA_V), BF16),
        grid=(batch, DA_HEADS, nq),
        in_specs=[pl.BlockSpec((blk, DA_V_DIM), lambda b, h, i: (b * nq + i, h)),
                  pl.BlockSpec((seq, DA_V_DIM), lambda b, h, i: (b, k_col0 + h)),
                  pl.BlockSpec((seq, DA_V_DIM), lambda b, h, i: (b, v_col0 + h)),
                  pl.BlockSpec((1, 2, blk, blk), lambda b, h, i: (h, 0, 0, 0)),
                  pl.BlockSpec((4, HEAD_DIM), lambda b, h, i: (0, 0)),
                  pl.BlockSpec((1, DA_V_DIM), lambda b, h, i: (0, 0))],
        out_specs=pl.BlockSpec((blk, DA_V_DIM), lambda b, h, i: (b * nq + i, h)),
        compiler_params=_params(("parallel", "parallel", "parallel"), blocks, scratch),
        name="diff_attention",
    )(qkv, qkv, qkv, bias, lam_p, subln_g.reshape(1, DA_V_DIM))


def _swa_kernel(q_ref, kc_ref, kp_ref, vc_ref, vp_ref, bias_ref, sink_ref, o_ref):
    n = pl.program_id(1)
    is_prev = lax.broadcasted_iota(jnp.int32, (1, 2 * WINDOW), 1) < WINDOW
    no_prev = jnp.where(is_prev & (n == 0), NEG, 0.0).astype(F32)
    for hk in range(SW_KV_HEADS):
        cols = slice(hk * HEAD_DIM, (hk + 1) * HEAD_DIM)
        heads = [slice((hk * SW_GROUP + g) * HEAD_DIM, (hk * SW_GROUP + g + 1) * HEAD_DIM) for g in range(SW_GROUP)]
        q = jnp.concatenate([q_ref[:, hd] for hd in heads], axis=0)
        k = jnp.concatenate([kp_ref[:, cols], kc_ref[:, cols]], axis=0)
        v = jnp.concatenate([vp_ref[:, cols], vc_ref[:, cols]], axis=0)
        s = _dot_nt(q, k) + bias_ref[hk] + no_prev
        sink = sink_ref[hk]
        m = jnp.maximum(jnp.max(s, axis=-1, keepdims=True), sink)
        e = jnp.exp(s - m)
        den = jnp.sum(e, axis=-1, keepdims=True) + jnp.exp(sink - m)
        o = jnp.dot(e.astype(BF16), v, preferred_element_type=F32) / den
        for g, hd in enumerate(heads):
            o_ref[:, hd] = o[g * WINDOW:(g + 1) * WINDOW, :].astype(o_ref.dtype)


def _sliding_attention(qkv, bias_tiles, sinks, batch, seq):
    nb = seq // WINDOW
    q_col = 3 * DA_QK // SW_Q
    k_col = (3 * DA_QK + SW_Q) // SW_KV
    v_col = k_col + 1
    cur = lambda b, n: b * nb + n
    prev = lambda b, n: b * nb + jnp.maximum(n - 1, 0)
    rows = SW_GROUP * WINDOW
    bias = bias_tiles.reshape(SW_KV_HEADS, SW_GROUP, 2, WINDOW, WINDOW)[:, :, ::-1]
    bias = bias.transpose(0, 1, 3, 2, 4).reshape(SW_KV_HEADS, rows, 2 * WINDOW)
    sink_cols = jnp.repeat(sinks.astype(F32).reshape(SW_KV_HEADS, SW_GROUP), WINDOW, axis=1).reshape(
        SW_KV_HEADS, rows, 1)
    blocks = (2 * _nbytes((WINDOW, SW_Q), BF16) + 4 * _nbytes((WINDOW, SW_KV), BF16)
              + _nbytes((SW_KV_HEADS, rows, 2 * WINDOW), F32) + _nbytes((SW_KV_HEADS, rows, 128), F32))
    return pl.pallas_call(
        _swa_kernel,
        out_shape=jax.ShapeDtypeStruct((batch * seq, SW_Q), BF16),
        grid=(batch, nb),
        in_specs=[pl.BlockSpec((WINDOW, SW_Q), lambda b, n: (cur(b, n), q_col)),
                  pl.BlockSpec((WINDOW, SW_KV), lambda b, n: (cur(b, n), k_col)),
                  pl.BlockSpec((WINDOW, SW_KV), lambda b, n: (prev(b, n), k_col)),
                  pl.BlockSpec((WINDOW, SW_KV), lambda b, n: (cur(b, n), v_col)),
                  pl.BlockSpec((WINDOW, SW_KV), lambda b, n: (prev(b, n), v_col)),
                  pl.BlockSpec((SW_KV_HEADS, rows, 2 * WINDOW), lambda b, n: (0, 0, 0)),
                  pl.BlockSpec((SW_KV_HEADS, rows, 1), lambda b, n: (0, 0, 0))],
        out_specs=pl.BlockSpec((WINDOW, SW_Q), lambda b, n: (cur(b, n), 0)),
        compiler_params=_params(("parallel", "parallel"), blocks, 16 * _nbytes((rows, 2 * WINDOW), F32)),
        name="sliding_attention",
    )(qkv, qkv, qkv, qkv, qkv, bias, sink_cols)


def _merge_kernel(oa_ref, ob_ref, wa_ref, wb_ref, ga_ref, gb_ref, o_ref):
    a = jnp.dot(oa_ref[...], wa_ref[...], preferred_element_type=F32)
    b = jnp.dot(ob_ref[...], wb_ref[...], preferred_element_type=F32)
    o_ref[...] = (ga_ref[...] * a + gb_ref[...] * b).astype(o_ref.dtype)


def _merge(o_a, o_b, w_a, w_b, gates, bm=1024, bn=1024):
    t, ka = o_a.shape
    kb = o_b.shape[1]
    d = w_a.shape[1]
    nj = d // bn
    blocks = (_nbytes((bm, ka), BF16) + _nbytes((bm, kb), BF16) + _nbytes((ka, bn), BF16)
              + _nbytes((kb, bn), BF16) + 2 * _nbytes((bm, bn), gates.dtype) + _nbytes((bm, bn), BF16))
    return pl.pallas_call(
        _merge_kernel,
        out_shape=jax.ShapeDtypeStruct((t, d), BF16),
        grid=(t // bm, nj),
        in_specs=[pl.BlockSpec((bm, ka), lambda i, j: (i, 0)),
                  pl.BlockSpec((bm, kb), lambda i, j: (i, 0)),
                  pl.BlockSpec((ka, bn), lambda i, j: (0, j)),
                  pl.BlockSpec((kb, bn), lambda i, j: (0, j)),
                  pl.BlockSpec((bm, bn), lambda i, j: (i, j)),
                  pl.BlockSpec((bm, bn), lambda i, j: (i, nj + j))],
        out_specs=pl.BlockSpec((bm, bn), lambda i, j: (i, j)),
        compiler_params=_params(("parallel", "parallel"), blocks, 2 * _nbytes((bm, bn), F32)),
        name="merge",
    )(o_a, o_b, w_a, w_b, gates, gates)


def _sort_pairs(n):
    pairs = []

    def merge(lo, hi, r):
        step = r * 2
        if step < hi - lo:
            merge(lo, hi, step)
            merge(lo + r, hi, step)
            pairs.extend((i, i + r) for i in range(lo + r, hi - r, step))
        else:
            pairs.append((lo, lo + r))

    def sort(lo, hi):
        if hi - lo >= 1:
            mid = lo + (hi - lo) // 2
            sort(lo, mid)
            sort(mid + 1, hi)
            merge(lo, hi, 1)

    sort(0, n - 1)
    return pairs


_SORT16 = _sort_pairs(PEER_TOPK)


def _sort_desc(xs):
    xs = list(xs)
    for i, j in _SORT16:
        xs[i], xs[j] = jnp.maximum(xs[i], xs[j]), jnp.minimum(xs[i], xs[j])
    return xs


def _merge_top(a, b):
    k = PEER_TOPK
    xs = [jnp.maximum(a[i], b[k - 1 - i]) for i in range(k)]
    d = k // 2
    while d >= 1:
        for i in range(k):
            if not i & d:
                xs[i], xs[i + d] = jnp.maximum(xs[i], xs[i + d]), jnp.minimum(xs[i], xs[i + d])
        d //= 2
    return xs


def _top16_over_rows(s):
    groups = [s[a * 8:(a + 1) * 8, :] for a in range(s.shape[0] // 8)]
    xs = _sort_desc(groups)
    for shift in (4, 2, 1):
        xs = _merge_top(xs, [pltpu.roll(x, shift, 0) for x in xs])
    return xs


def _route_kernel(q_ref, k1_ref, k2_ref, s1_ref, s2_ref, e1_ref, e2_ref, tau_ref):
    q = q_ref[...]
    s1 = _dot_nt(k1_ref[...], q[:, :KEY_DIM])
    s2 = _dot_nt(k2_ref[...], q[:, KEY_DIM:])
    v1 = _top16_over_rows(s1)
    v2 = _top16_over_rows(s2)
    k = PEER_TOPK
    top = [v1[0] + v2[b] for b in range(k)]
    rest = [v1[a] + v2[b] for a in range(1, k) for b in range(k) if (a + 1) * (b + 1) <= k]
    pad = jnp.full(top[0].shape, -jnp.inf, F32)
    rest = rest + [pad] * (-len(rest) % k)
    for g in range(len(rest) // k):
        top = _merge_top(top, _sort_desc(rest[g * k:(g + 1) * k]))
    z = jnp.ones_like(top[0])
    for c in top[1:]:
        z = z + jnp.exp(c - top[0])
    s1_ref[0] = s1
    s2_ref[0] = s2
    e1_ref[0] = jnp.exp(s1 - v1[0][0:1]) / z[0:1]
    e2_ref[0] = jnp.exp(s2 - v2[0][0:1])
    tau_ref[0] = top[k - 1][0:1]


def _route(q, k1, k2, tm=1024):
    t = q.shape[0]
    big = jax.ShapeDtypeStruct((PEER_HEADS, N_KEYS, t), F32)
    big_spec = pl.BlockSpec((1, N_KEYS, tm), lambda i, h: (h, 0, i))
    blocks = _nbytes((tm, 2 * KEY_DIM), BF16) + 4 * _nbytes((N_KEYS, tm), F32)
    return pl.pallas_call(
        _route_kernel,
        out_shape=(big, big, big, big, jax.ShapeDtypeStruct((PEER_HEADS, 1, t), F32)),
        grid=(t // tm, PEER_HEADS),
        in_specs=[pl.BlockSpec((tm, 2 * KEY_DIM), lambda i, h: (i, h)),
                  pl.BlockSpec((N_KEYS, KEY_DIM), lambda i, h: (0, 0)),
                  pl.BlockSpec((N_KEYS, KEY_DIM), lambda i, h: (0, 0))],
        out_specs=(big_spec, big_spec, big_spec, big_spec,
                   pl.BlockSpec((1, 1, tm), lambda i, h: (h, 0, i))),
        compiler_params=_params(("parallel", "parallel"), blocks, 24 * _nbytes((N_KEYS, tm), F32)),
        name="peer_route",
    )(q, k1, k2)


def _peer_up_kernel(u_ref, h_ref, s1_ref, e1_ref, s2_ref, e2_ref, tau_ref, o_ref, act_a, act_b):
    s = pl.program_id(0)

    @pl.when(s == 0)
    def _():
        act_a[...] = jnp.zeros(act_a.shape, F32)
        act_b[...] = jnp.zeros(act_b.shape, F32)

    def step(act_new, act_old):
        def chunk(r, carry):
            act_new[...] += _dot_nt(u_ref[r], h_ref[r])
            rows = pl.ds(pl.multiple_of(r * N_KEYS, N_KEYS), N_KEYS)
            s1_rows = [s1_ref[h, pl.ds(r, 1), :] for h in range(PEER_HEADS)]
            e1_rows = [e1_ref[h, pl.ds(r, 1), :] for h in range(PEER_HEADS)]
            for c in range(act_old.shape[1] // 128):
                cols = slice(c * 128, (c + 1) * 128)
                act = act_old[rows, cols]
                act_old[rows, cols] = jnp.zeros_like(act)
                gate = None
                for h in range(PEER_HEADS):
                    score = s2_ref[h, :, cols] + s1_rows[h][:, cols]
                    term = jnp.where(score >= tau_ref[h, :, cols], e2_ref[h, :, cols] * e1_rows[h][:, cols], 0.0)
                    gate = term if gate is None else gate + term
                o_ref[rows, cols] = (gate * jax.nn.gelu(act)).astype(o_ref.dtype)
            return carry

        lax.fori_loop(0, u_ref.shape[0], chunk, 0, unroll=PEER_UP_UNROLL)

    @pl.when(s % 2 == 0)
    def _():
        step(act_a, act_b)

    @pl.when(s % 2 == 1)
    def _():
        step(act_b, act_a)


def _peer_up(u, hn, s1, e1, s2, e2, tau, te=PEER_UP_TE, tm=512):
    chunks, n_exp, dc = u.shape
    t = hn.shape[1]
    rows = te // N_KEYS
    assert chunks == rows
    nj = n_exp // te
    steps = (t // tm) * nj

    def tile_of(step):
        return step // nj, step % nj

    def now(s):
        return tile_of(jnp.minimum(s, steps - 1))

    def lag(s):
        return tile_of(jnp.maximum(s - 1, 0))

    row_spec = pl.BlockSpec((PEER_HEADS, rows, tm), lambda s: (0, lag(s)[1], lag(s)[0]))
    full_spec = pl.BlockSpec((PEER_HEADS, N_KEYS, tm), lambda s: (0, 0, lag(s)[0]))
    blocks = (_nbytes((chunks, te, dc), BF16) + _nbytes((chunks, tm, dc), BF16)
              + 2 * _nbytes((PEER_HEADS, N_KEYS, tm), F32)
              + 2 * _nbytes((PEER_HEADS, rows, tm), F32) + _nbytes((te, tm), BF16))
    return pl.pallas_call(
        _peer_up_kernel,
        out_shape=jax.ShapeDtypeStruct((n_exp, t), BF16),
        grid=(steps + 1,),
        in_specs=[pl.BlockSpec((chunks, te, dc), lambda s: (0, now(s)[1], 0)),
                  pl.BlockSpec((chunks, tm, dc), lambda s: (0, now(s)[0], 0)),
                  row_spec, row_spec, full_spec, full_spec,
                  pl.BlockSpec((PEER_HEADS, 1, tm), lambda s: (0, 0, lag(s)[0]))],
        out_specs=pl.BlockSpec((te, tm), lambda s: (lag(s)[1], lag(s)[0])),
        scratch_shapes=[pltpu.VMEM((te, tm), F32), pltpu.VMEM((te, tm), F32)],
        compiler_params=_params(("arbitrary",), blocks, 3 * _nbytes((te, tm), F32)),
        name="peer_up",
    )(u, hn, s1, e1, s2, e2, tau)


def _peer_down_kernel(vt_ref, w_ref, x_ref, o_ref, acc_ref):
    kk = pl.program_id(2)

    @pl.when(kk == 0)
    def _():
        acc_ref[...] = jnp.zeros(acc_ref.shape, F32)

    acc_ref[...] += jnp.dot(vt_ref[...], w_ref[...], preferred_element_type=F32)

    @pl.when(kk == pl.num_programs(2) - 1)
    def _():
        o_ref[...] = x_ref[...] + acc_ref[...].T


def _peer_down(vt, wt, x, bd=1024, bt=1024, tk=2048):
    d, n_exp = vt.shape
    t = wt.shape[1]
    blocks = (_nbytes((bd, tk), BF16) + _nbytes((tk, bt), BF16) + 2 * _nbytes((bt, bd), F32))
    return pl.pallas_call(
        _peer_down_kernel,
        out_shape=jax.ShapeDtypeStruct((t, d), F32),
        grid=(d // bd, t // bt, n_exp // tk),
        in_specs=[pl.BlockSpec((bd, tk), lambda i, j, k: (i, k)),
                  pl.BlockSpec((tk, bt), lambda i, j, k: (k, j)),
                  pl.BlockSpec((bt, bd), lambda i, j, k: (j, i))],
        out_specs=pl.BlockSpec((bt, bd), lambda i, j, k: (j, i)),
        scratch_shapes=[pltpu.VMEM((bd, bt), F32)],
        compiler_params=_params(("parallel", "parallel", "arbitrary"), blocks, 3 * _nbytes((bd, bt), F32)),
        name="peer_down",
    )(vt, wt, x)


def _ple_kernel(h_ref, wg_ref, p_ref, wp_ref, x_ref, o_ref):
    gate = jax.nn.sigmoid(jnp.dot(h_ref[...], wg_ref[...], preferred_element_type=F32))
    emb = jnp.dot(p_ref[...], wp_ref[...], preferred_element_type=F32)
    o_ref[...] = x_ref[...] + gate * emb


def _ple(hp, w_gate, p, w_proj, x, bm=512, bn=1024):
    t, d = hp.shape
    pd = p.shape[1]
    n = w_gate.shape[1]
    blocks = (_nbytes((bm, d), BF16) + _nbytes((d, bn), BF16) + _nbytes((bm, pd), BF16)
              + _nbytes((pd, bn), BF16) + 2 * _nbytes((bm, bn), F32))
    return pl.pallas_call(
        _ple_kernel,
        out_shape=jax.ShapeDtypeStruct((t, n), F32),
        grid=(t // bm, n // bn),
        in_specs=[pl.BlockSpec((bm, d), lambda i, j: (i, 0)),
                  pl.BlockSpec((d, bn), lambda i, j: (0, j)),
                  pl.BlockSpec((bm, pd), lambda i, j: (i, 0)),
                  pl.BlockSpec((pd, bn), lambda i, j: (0, j)),
                  pl.BlockSpec((bm, bn), lambda i, j: (i, j))],
        out_specs=pl.BlockSpec((bm, bn), lambda i, j: (i, j)),
        compiler_params=_params(("parallel", "parallel"), blocks, 2 * _nbytes((bm, bn), F32)),
        name="ple",
    )(hp, w_gate, p, w_proj, x)


def _qkv_column_scale():
    s = HEAD_DIM ** -0.5
    parts = [(DA_QK, s), (DA_QK, 1.0), (DA_V, 1.0), (SW_Q, s), (SW_KV, 1.0), (SW_KV, 1.0)]
    return jnp.concatenate([jnp.full((1, w), v, F32) for w, v in parts], axis=1)


@jax.jit
def kernel(x, p, positions, rel_bias, norm_mix, w_in, da_lambda, da_subln, sw_sinks, w_br_a, w_br_b, w_out,
           norm_ffn, peer_wq, peer_k1, peer_k2, peer_u, peer_v, norm_ple, ple_gate, ple_proj, norm_final):
    del positions
    batch, seq, d = x.shape
    t = batch * seq
    depth = w_in.shape[0]
    xf = x.reshape(t, d)
    da_bias = _bias_tiles(rel_bias[:, :DA_HEADS], DA_BLK, None, True)
    sw_bias = _bias_tiles(rel_bias[:, DA_HEADS:], WINDOW, WINDOW, False)
    col_scale = _qkv_column_scale()
    tile = pl.BlockSpec((1, 1024), lambda i, j: (0, j))
    for i in range(depth):
        lam_init = 0.8 - 0.6 * math.exp(-0.3 * i)
        h = _rmsnorm(xf, norm_mix[i], BF16)
        qkv = _matmul(_mm_scale_kernel, h, _to_bf16(w_in, i, 0, QKV_WIDTH, 1024), [col_scale], [tile],
                      BF16, 1024, 1024, "proj_qkv")
        gates = _matmul(_mm_sigmoid_kernel, h, _to_bf16(w_in, i, QKV_WIDTH, w_in.shape[2] - QKV_WIDTH, 1024),
                        [], [], F32, 1024, 1024, "proj_gates")
        o_a = _diff_attention(qkv, da_bias, da_lambda[i], da_subln[i], lam_init, batch, seq)
        o_b = _sliding_attention(qkv, sw_bias, sw_sinks[i], batch, seq)
        merged = _merge(o_a, o_b, _to_bf16(w_br_a, i), _to_bf16(w_br_b, i), gates)
        xf = _matmul(_mm_residual_kernel, merged, _to_bf16(w_out, i), [xf],
                     [pl.BlockSpec((512, 1024), lambda i, j: (i, j))], F32, 512, 1024, "proj_out")
        hn, hn_chunked = _rmsnorm_chunked(xf, norm_ffn[i], PEER_UP_CHUNKS)
        q = _matmul(_mm_plain_kernel, hn, _to_bf16(peer_wq, i), [], [], BF16, 1024, 1024, "peer_query")
        s1, s2, e1, e2, tau = _route(q, peer_k1[i].astype(BF16), peer_k2[i].astype(BF16))
        wt = _peer_up(_to_bf16(peer_u, i, layout="chunked", chunks=PEER_UP_CHUNKS), hn_chunked, s1, e1, s2, e2, tau)
        xf = _peer_down(_to_bf16(peer_v, i, layout="transposed"), wt, xf)
        hp = _rmsnorm(xf, norm_ple[i], BF16)
        xf = _ple(hp, _to_bf16(ple_gate, i), p[i].reshape(t, -1).astype(BF16), ple_proj[i].astype(BF16), xf)
    return _rmsnorm(xf, norm_final, F32).reshape(batch, seq, d)
```

```python
import functools
import math

import jax
import jax.numpy as jnp
from jax import lax
from jax.experimental import pallas as pl
from jax.experimental.pallas import tpu as pltpu

F32 = jnp.float32
BF16 = jnp.bfloat16

HEAD_DIM = 128
DA_HEADS = 8
DA_V_DIM = 2 * HEAD_DIM
SW_Q_HEADS = 16
SW_KV_HEADS = 4
SW_GROUP = SW_Q_HEADS // SW_KV_HEADS
WINDOW = 128
N_BUCKETS = 32
MAX_EXACT = N_BUCKETS // 2
MAX_DIST = 128
NEG = -1e30
DA_QK = DA_HEADS * 2 * HEAD_DIM
DA_V = DA_HEADS * DA_V_DIM
SW_Q = SW_Q_HEADS * HEAD_DIM
SW_KV = SW_KV_HEADS * HEAD_DIM
QKV_WIDTH = 3 * DA_QK + SW_Q + 2 * SW_KV
PEER_HEADS = 8
N_KEYS = 128
PEER_TOPK = 16
KEY_DIM = 128
EPS = 1e-6

V7X_VMEM_REQUEST_CAP = 60 * 1024 * 1024
BF16_ROWS = 16
DA_BLK = 512
PEER_UP_TE = 1024
PEER_UP_CHUNKS = PEER_UP_TE // N_KEYS
PEER_UP_UNROLL = 4


def _nbytes(shape, dtype):
    return math.prod(shape) * jnp.dtype(dtype).itemsize


def _params(semantics, block_bytes, scratch_bytes=0, flags=None):
    need = int(1.25 * (2 * block_bytes + scratch_bytes)) + (4 << 20)
    return pltpu.CompilerParams(dimension_semantics=semantics,
                                vmem_limit_bytes=min(need, V7X_VMEM_REQUEST_CAP), flags=flags)


def _dot_nt(a, b):
    return lax.dot_general(a, b, (((1,), (1,)), ((), ())), preferred_element_type=F32)


def _rmsnorm_kernel(x_ref, g_ref, o_ref):
    x = x_ref[...]
    y = x * lax.rsqrt(jnp.mean(x * x, axis=-1, keepdims=True) + EPS)
    o_ref[...] = (y * g_ref[...]).astype(o_ref.dtype)


def _rmsnorm(x, g, out_dtype, rows=256):
    t, d = x.shape
    blocks = _nbytes((rows, d), F32) + _nbytes((rows, d), out_dtype)
    return pl.pallas_call(
        _rmsnorm_kernel,
        out_shape=jax.ShapeDtypeStruct((t, d), out_dtype),
        grid=(t // rows,),
        in_specs=[pl.BlockSpec((rows, d), lambda i: (i, 0)),
                  pl.BlockSpec((1, d), lambda i: (0, 0))],
        out_specs=pl.BlockSpec((rows, d), lambda i: (i, 0)),
        compiler_params=_params(("parallel",), blocks, _nbytes((rows, d), F32)),
        name="rmsnorm",
    )(x, g.reshape(1, d))


def _rmsnorm_chunked_kernel(x_ref, g_ref, o_ref, oc_ref):
    x = x_ref[...]
    y = (x * lax.rsqrt(jnp.mean(x * x, axis=-1, keepdims=True) + EPS) * g_ref[...]).astype(o_ref.dtype)
    o_ref[...] = y
    dc = oc_ref.shape[2]
    for k in range(oc_ref.shape[0]):
        oc_ref[k] = y[:, k * dc:(k + 1) * dc]


def _rmsnorm_chunked(x, g, chunks, rows=256):
    t, d = x.shape
    dc = d // chunks
    blocks = _nbytes((rows, d), F32) + 2 * _nbytes((rows, d), BF16)
    return pl.pallas_call(
        _rmsnorm_chunked_kernel,
        out_shape=(jax.ShapeDtypeStruct((t, d), BF16), jax.ShapeDtypeStruct((chunks, t, dc), BF16)),
        grid=(t // rows,),
        in_specs=[pl.BlockSpec((rows, d), lambda i: (i, 0)),
                  pl.BlockSpec((1, d), lambda i: (0, 0))],
        out_specs=(pl.BlockSpec((rows, d), lambda i: (i, 0)),
                   pl.BlockSpec((chunks, rows, dc), lambda i: (0, i, 0))),
        compiler_params=_params(("parallel",), blocks, _nbytes((rows, d), F32)),
        name="rmsnorm_chunked",
    )(x, g.reshape(1, d))


CAST_BLOCK_BYTES = 8 << 20


def _cast_kernel(x_ref, o_ref):
    o_ref[...] = x_ref[...].astype(o_ref.dtype)


def _cast_chunked_kernel(x_ref, o_ref):
    dc = o_ref.shape[2]
    for k in range(o_ref.shape[0]):
        o_ref[k] = x_ref[:, k * dc:(k + 1) * dc].astype(o_ref.dtype)


def _cast_transposed_kernel(x_ref, o_ref):
    o_ref[...] = x_ref[...].T.astype(o_ref.dtype)


def _to_bf16(w, layer, col0=0, ncols=None, bc=None, layout="plain", chunks=None):
    _, r, c = w.shape
    ncols = c if ncols is None else ncols
    bc = ncols if bc is None else bc
    br = min(r, CAST_BLOCK_BYTES // (bc * 4))
    grid = (r // br, ncols // bc)
    in_spec = pl.BlockSpec((None, br, bc), lambda i, j: (layer, i, col0 // bc + j))
    if layout == "plain":
        body, shape = _cast_kernel, (r, ncols)
        out_spec = pl.BlockSpec((br, bc), lambda i, j: (i, j))
    elif layout == "chunked":
        assert bc == ncols
        body, shape = _cast_chunked_kernel, (chunks, r, ncols // chunks)
        out_spec = pl.BlockSpec((chunks, br, ncols // chunks), lambda i, j: (0, i, 0))
    else:
        assert bc == ncols
        body, shape = _cast_transposed_kernel, (ncols, r)
        out_spec = pl.BlockSpec((ncols, br), lambda i, j: (0, i))
    return pl.pallas_call(
        body,
        out_shape=jax.ShapeDtypeStruct(shape, BF16),
        grid=grid,
        in_specs=[in_spec],
        out_specs=out_spec,
        compiler_params=_params(("parallel", "parallel"), _nbytes((br, bc), F32) + _nbytes((br, bc), BF16),
                                _nbytes((br, bc), F32)),
        name="to_bf16_" + layout,
    )(w)


def _mm_scale_kernel(a_ref, b_ref, s_ref, o_ref):
    acc = jnp.dot(a_ref[...], b_ref[...], preferred_element_type=F32)
    o_ref[...] = (acc * s_ref[...]).astype(o_ref.dtype)


def _mm_sigmoid_kernel(a_ref, b_ref, o_ref):
    acc = jnp.dot(a_ref[...], b_ref[...], preferred_element_type=F32)
    o_ref[...] = jax.nn.sigmoid(acc).astype(o_ref.dtype)


def _mm_plain_kernel(a_ref, b_ref, o_ref):
    o_ref[...] = jnp.dot(a_ref[...], b_ref[...], preferred_element_type=F32).astype(o_ref.dtype)


def _mm_residual_kernel(a_ref, b_ref, x_ref, o_ref):
    o_ref[...] = x_ref[...] + jnp.dot(a_ref[...], b_ref[...], preferred_element_type=F32)


def _matmul(body, a, b, extra, extra_specs, out_dtype, bm, bn, name):
    m, k = a.shape
    n = b.shape[1]
    blocks = (_nbytes((bm, k), a.dtype) + _nbytes((k, bn), b.dtype) + _nbytes((bm, bn), out_dtype)
              + sum(_nbytes(s.block_shape, e.dtype) for s, e in zip(extra_specs, extra)))
    return pl.pallas_call(
        body,
        out_shape=jax.ShapeDtypeStruct((m, n), out_dtype),
        grid=(m // bm, n // bn),
        in_specs=[pl.BlockSpec((bm, k), lambda i, j: (i, 0)),
                  pl.BlockSpec((k, bn), lambda i, j: (0, j))] + list(extra_specs),
        out_specs=pl.BlockSpec((bm, bn), lambda i, j: (i, j)),
        compiler_params=_params(("parallel", "parallel"), blocks, _nbytes((bm, bn), F32)),
        name=name,
    )(a, b, *extra)


def _bias_kernel(tab_ref, o_ref, *, blk, window, rebase):
    h = pl.program_id(0)
    r = lax.broadcasted_iota(jnp.int32, (blk, blk), 0)
    c = lax.broadcasted_iota(jnp.int32, (blk, blk), 1)
    base = tab_ref[N_BUCKETS - 1, h] if rebase else 0.0
    for delta in (0, 1):
        rel = r - c + delta * blk
        n = jnp.maximum(rel, 0)
        nf = jnp.maximum(n, 1).astype(F32)
        large = MAX_EXACT + (jnp.log(nf / MAX_EXACT) / math.log(MAX_DIST / MAX_EXACT)
                             * (N_BUCKETS - MAX_EXACT)).astype(jnp.int32)
        large = jnp.minimum(large, N_BUCKETS - 1)
        bucket = jnp.where(n < MAX_EXACT, n, large)
        bias = jnp.zeros((blk, blk), F32)
        for b in range(N_BUCKETS):
            bias = jnp.where(bucket == b, tab_ref[b, h] - base, bias)
        mask = rel >= 0
        if window is not None:
            mask = mask & (rel < window)
        o_ref[0, delta] = jnp.where(mask, bias, NEG)


def _bias_tiles(tab, blk, window, rebase):
    heads = tab.shape[1]
    return pl.pallas_call(
        functools.partial(_bias_kernel, blk=blk, window=window, rebase=rebase),
        out_shape=jax.ShapeDtypeStruct((heads, 2, blk, blk), F32),
        grid=(heads,),
        in_specs=[pl.BlockSpec(memory_space=pltpu.SMEM)],
        out_specs=pl.BlockSpec((1, 2, blk, blk), lambda h: (h, 0, 0, 0)),
        compiler_params=_params(("parallel",), _nbytes((2, blk, blk), F32), 4 * _nbytes((blk, blk), F32)),
        name="bias_tiles",
    )(tab)


def _da_kernel(q_ref, k_ref, v_ref, bias_ref, lam_ref, g_ref, o_ref, *, lam_init):
    blk = q_ref.shape[0]
    qi = pl.program_id(2)
    lp = lam_ref[...]
    lam = (jnp.exp(jnp.sum(lp[0:1] * lp[1:2], axis=-1, keepdims=True))
           - jnp.exp(jnp.sum(lp[2:3] * lp[3:4], axis=-1, keepdims=True)) + lam_init)

    def softmax_pv(j, case):
        dims = slice(j * HEAD_DIM, (j + 1) * HEAD_DIM)
        q = q_ref[:, dims]
        spans = [(slice(case * blk, (case + 1) * blk), bias_ref[0, 0])]
        if case >= 1:
            spans.append((slice((case - 1) * blk, case * blk), bias_ref[0, 1]))
        if case >= 2:
            spans.append((slice(0, (case - 1) * blk), None))
        scores = []
        for rows, bias in spans:
            s = _dot_nt(q, k_ref[rows, dims])
            scores.append(s if bias is None else s + bias)
        m = functools.reduce(jnp.maximum, [jnp.max(s, axis=-1, keepdims=True) for s in scores])
        probs = [jnp.exp(s - m) for s in scores]
        norm = sum(jnp.sum(p, axis=-1, keepdims=True) for p in probs)
        out = sum(jnp.dot(p.astype(BF16), v_ref[rows, :], preferred_element_type=F32)
                  for p, (rows, _) in zip(probs, spans))
        return out / norm

    for case in range(k_ref.shape[0] // blk):
        @pl.when(qi == case)
        def _(case=case):
            o = softmax_pv(0, case) - lam * softmax_pv(1, case)
            y = o * lax.rsqrt(jnp.mean(o * o, axis=-1, keepdims=True) + EPS)
            o_ref[...] = ((y * g_ref[...]) * (1.0 - lam_init)).astype(o_ref.dtype)


def _diff_attention(qkv, bias, lam_p, subln_g, lam_init, batch, seq):
    blk = DA_BLK
    nq = seq // blk
    blocks = (2 * _nbytes((blk, DA_V_DIM), BF16) + 2 * _nbytes((seq, DA_V_DIM), BF16)
              + _nbytes((2, blk, blk), F32))
    scratch = 6 * _nbytes((blk, seq), F32)
    k_col0 = DA_QK // DA_V_DIM
    v_col0 = 2 * DA_QK // DA_V_DIM
    return pl.pallas_call(
        functools.partial(_da_kernel, lam_init=lam_init),
        out_shape=jax.ShapeDtypeStruct((batch * seq, DA_V), BF16),
        grid=(batch, DA_HEADS, nq),
        in_specs=[pl.BlockSpec((blk, DA_V_DIM), lambda b, h, i: (b * nq + i, h)),
                  pl.BlockSpec((seq, DA_V_DIM), lambda b, h, i: (b, k_col0 + h)),
                  pl.BlockSpec((seq, DA_V_DIM), lambda b, h, i: (b, v_col0 + h)),
                  pl.BlockSpec((1, 2, blk, blk), lambda b, h, i: (h, 0, 0, 0)),
                  pl.BlockSpec((4, HEAD_DIM), lambda b, h, i: (0, 0)),
                  pl.BlockSpec((1, DA_V_DIM), lambda b, h, i: (0, 0))],
        out_specs=pl.BlockSpec((blk, DA_V_DIM), lambda b, h, i: (b * nq + i, h)),
        compiler_params=_params(("parallel", "parallel", "parallel"), blocks, scratch),
        name="diff_attention",
    )(qkv, qkv, qkv, bias, lam_p, subln_g.reshape(1, DA_V_DIM))


def _swa_kernel(q_ref, kc_ref, kp_ref, vc_ref, vp_ref, bias_ref, sink_ref, o_ref):
    n = pl.program_id(1)
    is_prev = lax.broadcasted_iota(jnp.int32, (1, 2 * WINDOW), 1) < WINDOW
    no_prev = jnp.where(is_prev & (n == 0), NEG, 0.0).astype(F32)
    for hk in range(SW_KV_HEADS):
        cols = slice(hk * HEAD_DIM, (hk + 1) * HEAD_DIM)
        heads = [slice((hk * SW_GROUP + g) * HEAD_DIM, (hk * SW_GROUP + g + 1) * HEAD_DIM) for g in range(SW_GROUP)]
        q = jnp.concatenate([q_ref[:, hd] for hd in heads], axis=0)
        k = jnp.concatenate([kp_ref[:, cols], kc_ref[:, cols]], axis=0)
        v = jnp.concatenate([vp_ref[:, cols], vc_ref[:, cols]], axis=0)
        s = _dot_nt(q, k) + bias_ref[hk] + no_prev
        sink = sink_ref[hk]
        m = jnp.maximum(jnp.max(s, axis=-1, keepdims=True), sink)
        e = jnp.exp(s - m)
        den = jnp.sum(e, axis=-1, keepdims=True) + jnp.exp(sink - m)
        o = jnp.dot(e.astype(BF16), v, preferred_element_type=F32) / den
        for g, hd in enumerate(heads):
            o_ref[:, hd] = o[g * WINDOW:(g + 1) * WINDOW, :].astype(o_ref.dtype)


def _sliding_attention(qkv, bias_tiles, sinks, batch, seq):
    nb = seq // WINDOW
    q_col = 3 * DA_QK // SW_Q
    k_col = (3 * DA_QK + SW_Q) // SW_KV
    v_col = k_col + 1
    cur = lambda b, n: b * nb + n
    prev = lambda b, n: b * nb + jnp.maximum(n - 1, 0)
    rows = SW_GROUP * WINDOW
    bias = bias_tiles.reshape(SW_KV_HEADS, SW_GROUP, 2, WINDOW, WINDOW)[:, :, ::-1]
    bias = bias.transpose(0, 1, 3, 2, 4).reshape(SW_KV_HEADS, rows, 2 * WINDOW)
    sink_cols = jnp.repeat(sinks.astype(F32).reshape(SW_KV_HEADS, SW_GROUP), WINDOW, axis=1).reshape(
        SW_KV_HEADS, rows, 1)
    blocks = (2 * _nbytes((WINDOW, SW_Q), BF16) + 4 * _nbytes((WINDOW, SW_KV), BF16)
              + _nbytes((SW_KV_HEADS, rows, 2 * WINDOW), F32) + _nbytes((SW_KV_HEADS, rows, 128), F32))
    return pl.pallas_call(
        _swa_kernel,
        out_shape=jax.ShapeDtypeStruct((batch * seq, SW_Q), BF16),
        grid=(batch, nb),
        in_specs=[pl.BlockSpec((WINDOW, SW_Q), lambda b, n: (cur(b, n), q_col)),
                  pl.BlockSpec((WINDOW, SW_KV), lambda b, n: (cur(b, n), k_col)),
                  pl.BlockSpec((WINDOW, SW_KV), lambda b, n: (prev(b, n), k_col)),
                  pl.BlockSpec((WINDOW, SW_KV), lambda b, n: (cur(b, n), v_col)),
                  pl.BlockSpec((WINDOW, SW_KV), lambda b, n: (prev(b, n), v_col)),
                  pl.BlockSpec((SW_KV_HEADS, rows, 2 * WINDOW), lambda b, n: (0, 0, 0)),
                  pl.BlockSpec((SW_KV_HEADS, rows, 1), lambda b, n: (0, 0, 0))],
        out_specs=pl.BlockSpec((WINDOW, SW_Q), lambda b, n: (cur(b, n), 0)),
        compiler_params=_params(("parallel", "parallel"), blocks, 16 * _nbytes((rows, 2 * WINDOW), F32)),
        name="sliding_attention",
    )(qkv, qkv, qkv, qkv, qkv, bias, sink_cols)


def _merge_kernel(oa_ref, ob_ref, wa_ref, wb_ref, ga_ref, gb_ref, o_ref):
    a = jnp.dot(oa_ref[...], wa_ref[...], preferred_element_type=F32)
    b = jnp.dot(ob_ref[...], wb_ref[...], preferred_element_type=F32)
    o_ref[...] = (ga_ref[...] * a + gb_ref[...] * b).astype(o_ref.dtype)


def _merge(o_a, o_b, w_a, w_b, gates, bm=1024, bn=1024):
    t, ka = o_a.shape
    kb = o_b.shape[1]
    d = w_a.shape[1]
    nj = d // bn
    blocks = (_nbytes((bm, ka), BF16) + _nbytes((bm, kb), BF16) + _nbytes((ka, bn), BF16)
              + _nbytes((kb, bn), BF16) + 2 * _nbytes((bm, bn), gates.dtype) + _nbytes((bm, bn), BF16))
    return pl.pallas_call(
        _merge_kernel,
        out_shape=jax.ShapeDtypeStruct((t, d), BF16),
        grid=(t // bm, nj),
        in_specs=[pl.BlockSpec((bm, ka), lambda i, j: (i, 0)),
                  pl.BlockSpec((bm, kb), lambda i, j: (i, 0)),
                  pl.BlockSpec((ka, bn), lambda i, j: (0, j)),
                  pl.BlockSpec((kb, bn), lambda i, j: (0, j)),
                  pl.BlockSpec((bm, bn), lambda i, j: (i, j)),
                  pl.BlockSpec((bm, bn), lambda i, j: (i, nj + j))],
        out_specs=pl.BlockSpec((bm, bn), lambda i, j: (i, j)),
        compiler_params=_params(("parallel", "parallel"), blocks, 2 * _nbytes((bm, bn), F32)),
        name="merge",
    )(o_a, o_b, w_a, w_b, gates, gates)


def _sort_pairs(n):
    pairs = []

    def merge(lo, hi, r):
        step = r * 2
        if step < hi - lo:
            merge(lo, hi, step)
            merge(lo + r, hi, step)
            pairs.extend((i, i + r) for i in range(lo + r, hi - r, step))
        else:
            pairs.append((lo, lo + r))

    def sort(lo, hi):
        if hi - lo >= 1:
            mid = lo + (hi - lo) // 2
            sort(lo, mid)
            sort(mid + 1, hi)
            merge(lo, hi, 1)

    sort(0, n - 1)
    return pairs


_SORT16 = _sort_pairs(PEER_TOPK)


def _sort_desc(xs):
    xs = list(xs)
    for i, j in _SORT16:
        xs[i], xs[j] = jnp.maximum(xs[i], xs[j]), jnp.minimum(xs[i], xs[j])
    return xs


def _merge_top(a, b):
    k = PEER_TOPK
    xs = [jnp.maximum(a[i], b[k - 1 - i]) for i in range(k)]
    d = k // 2
    while d >= 1:
        for i in range(k):
            if not i & d:
                xs[i], xs[i + d] = jnp.maximum(xs[i], xs[i + d]), jnp.minimum(xs[i], xs[i + d])
        d //= 2
    return xs


def _top16_over_rows(s):
    groups = [s[a * 8:(a + 1) * 8, :] for a in range(s.shape[0] // 8)]
    xs = _sort_desc(groups)
    for shift in (4, 2, 1):
        xs = _merge_top(xs, [pltpu.roll(x, shift, 0) for x in xs])
    return xs


def _route_kernel(q_ref, k1_ref, k2_ref, cnt_ref, e1_ref, rank_ref, e2_ref):
    q = q_ref[...]
    s1 = _dot_nt(k1_ref[...], q[:, :KEY_DIM])
    s2 = _dot_nt(k2_ref[...], q[:, KEY_DIM:])
    v1 = _top16_over_rows(s1)
    v2 = _top16_over_rows(s2)
    k = PEER_TOPK
    top = [v1[0] + v2[b] for b in range(k)]
    rest = [v1[a] + v2[b] for a in range(1, k) for b in range(k) if (a + 1) * (b + 1) <= k]
    pad = jnp.full(top[0].shape, -jnp.inf, F32)
    rest = rest + [pad] * (-len(rest) % k)
    for g in range(len(rest) // k):
        top = _merge_top(top, _sort_desc(rest[g * k:(g + 1) * k]))
    z = jnp.ones_like(top[0])
    for c in top[1:]:
        z = z + jnp.exp(c - top[0])
    tau = top[k - 1][0:1]
    cnt = jnp.zeros_like(s1)
    rank = jnp.zeros_like(s2)
    for b in range(k):
        best = v2[b][0:1]
        cnt = cnt + jnp.where(s1 + best >= tau, 1.0, 0.0)
        rank = rank + jnp.where(best > s2, 1.0, 0.0)
    cnt_ref[0] = cnt
    rank_ref[0] = rank.astype(rank_ref.dtype)
    e1_ref[0] = jnp.exp(s1 - v1[0][0:1]) / z[0:1]
    e2_ref[0] = jnp.exp(s2 - v2[0][0:1]).astype(e2_ref.dtype)


def _route(q, k1, k2, tm=1024):
    t = q.shape[0]
    wide = jax.ShapeDtypeStruct((PEER_HEADS, N_KEYS, t), F32)
    narrow = jax.ShapeDtypeStruct((PEER_HEADS, N_KEYS, t), BF16)
    spec = pl.BlockSpec((1, N_KEYS, tm), lambda i, h: (h, 0, i))
    blocks = _nbytes((tm, 2 * KEY_DIM), BF16) + 3 * _nbytes((N_KEYS, tm), F32)
    return pl.pallas_call(
        _route_kernel,
        out_shape=(wide, wide, narrow, narrow),
        grid=(t // tm, PEER_HEADS),
        in_specs=[pl.BlockSpec((tm, 2 * KEY_DIM), lambda i, h: (i, h)),
                  pl.BlockSpec((N_KEYS, KEY_DIM), lambda i, h: (0, 0)),
                  pl.BlockSpec((N_KEYS, KEY_DIM), lambda i, h: (0, 0))],
        out_specs=(spec, spec, spec, spec),
        compiler_params=_params(("parallel", "parallel"), blocks, 24 * _nbytes((N_KEYS, tm), F32)),
        name="peer_route",
    )(q, k1, k2)


def _peer_up_kernel(u_ref, h_ref, cnt_ref, e1_ref, rank_ref, e2_ref, o_ref, act_a, act_b):
    s = pl.program_id(0)

    @pl.when(s == 0)
    def _():
        act_a[...] = jnp.zeros(act_a.shape, F32)
        act_b[...] = jnp.zeros(act_b.shape, F32)

    def step(act_new, act_old):
        def chunk(r, carry):
            act_new[...] += _dot_nt(u_ref[r], h_ref[r])
            rows = pl.ds(pl.multiple_of(r * N_KEYS, N_KEYS), N_KEYS)
            cnt_rows = [cnt_ref[h, pl.ds(r, 1), :] for h in range(PEER_HEADS)]
            e1_rows = [e1_ref[h, pl.ds(r, 1), :] for h in range(PEER_HEADS)]
            packed = (N_KEYS // BF16_ROWS, BF16_ROWS, 128)
            for c in range(act_old.shape[1] // 128):
                cols = slice(c * 128, (c + 1) * 128)
                act = act_old[rows, cols]
                act_old[rows, cols] = jnp.zeros_like(act)
                gate = None
                for h in range(PEER_HEADS):
                    cnt = jnp.broadcast_to(cnt_rows[h][:, cols], packed[1:]).astype(BF16)
                    e1 = jnp.broadcast_to(e1_rows[h][:, cols], packed[1:]).astype(BF16)
                    routed = rank_ref[h, :, cols].reshape(packed) < cnt
                    term = jnp.where(routed, e2_ref[h, :, cols].reshape(packed) * e1, jnp.zeros((), BF16))
                    gate = term if gate is None else gate + term
                w = jax.nn.gelu(act).astype(BF16).reshape(packed) * gate
                o_ref[rows, cols] = w.reshape(N_KEYS, 128)
            return carry

        lax.fori_loop(0, u_ref.shape[0], chunk, 0, unroll=PEER_UP_UNROLL)

    @pl.when(s % 2 == 0)
    def _():
        step(act_a, act_b)

    @pl.when(s % 2 == 1)
    def _():
        step(act_b, act_a)


def _peer_up(u, hn, cnt, e1, rank, e2, te=PEER_UP_TE, tm=512):
    chunks, n_exp, dc = u.shape
    t = hn.shape[1]
    rows = te // N_KEYS
    assert chunks == rows
    nj = n_exp // te
    steps = (t // tm) * nj

    def tile_of(step):
        return step // nj, step % nj

    def now(s):
        return tile_of(jnp.minimum(s, steps - 1))

    def lag(s):
        return tile_of(jnp.maximum(s - 1, 0))

    row_spec = pl.BlockSpec((PEER_HEADS, rows, tm), lambda s: (0, lag(s)[1], lag(s)[0]))
    full_spec = pl.BlockSpec((PEER_HEADS, N_KEYS, tm), lambda s: (0, 0, lag(s)[0]))
    blocks = (_nbytes((chunks, te, dc), BF16) + _nbytes((chunks, tm, dc), BF16)
              + 2 * _nbytes((PEER_HEADS, N_KEYS, tm), BF16)
              + 2 * _nbytes((PEER_HEADS, rows, tm), F32) + _nbytes((te, tm), BF16))
    return pl.pallas_call(
        _peer_up_kernel,
        out_shape=jax.ShapeDtypeStruct((n_exp, t), BF16),
        grid=(steps + 1,),
        in_specs=[pl.BlockSpec((chunks, te, dc), lambda s: (0, now(s)[1], 0)),
                  pl.BlockSpec((chunks, tm, dc), lambda s: (0, now(s)[0], 0)),
                  row_spec, row_spec, full_spec, full_spec],
        out_specs=pl.BlockSpec((te, tm), lambda s: (lag(s)[1], lag(s)[0])),
        scratch_shapes=[pltpu.VMEM((te, tm), F32), pltpu.VMEM((te, tm), F32)],
        compiler_params=_params(("arbitrary",), blocks, 3 * _nbytes((te, tm), F32)),
        name="peer_up",
    )(u, hn, cnt, e1, rank, e2)


def _peer_down_kernel(vt_ref, w_ref, x_ref, o_ref, acc_ref):
    kk = pl.program_id(2)

    @pl.when(kk == 0)
    def _():
        acc_ref[...] = jnp.zeros(acc_ref.shape, F32)

    acc_ref[...] += jnp.dot(vt_ref[...], w_ref[...], preferred_element_type=F32)

    @pl.when(kk == pl.num_programs(2) - 1)
    def _():
        o_ref[...] = x_ref[...] + acc_ref[...].T


def _peer_down(vt, wt, x, bd=1024, bt=1024, tk=2048):
    d, n_exp = vt.shape
    t = wt.shape[1]
    blocks = (_nbytes((bd, tk), BF16) + _nbytes((tk, bt), BF16) + 2 * _nbytes((bt, bd), F32))
    return pl.pallas_call(
        _peer_down_kernel,
        out_shape=jax.ShapeDtypeStruct((t, d), F32),
        grid=(d // bd, t // bt, n_exp // tk),
        in_specs=[pl.BlockSpec((bd, tk), lambda i, j, k: (i, k)),
                  pl.BlockSpec((tk, bt), lambda i, j, k: (k, j)),
                  pl.BlockSpec((bt, bd), lambda i, j, k: (j, i))],
        out_specs=pl.BlockSpec((bt, bd), lambda i, j, k: (j, i)),
        scratch_shapes=[pltpu.VMEM((bd, bt), F32)],
        compiler_params=_params(("parallel", "parallel", "arbitrary"), blocks, 3 * _nbytes((bd, bt), F32)),
        name="peer_down",
    )(vt, wt, x)


def _ple_kernel(h_ref, wg_ref, p_ref, wp_ref, x_ref, o_ref):
    gate = jax.nn.sigmoid(jnp.dot(h_ref[...], wg_ref[...], preferred_element_type=F32))
    emb = jnp.dot(p_ref[...], wp_ref[...], preferred_element_type=F32)
    o_ref[...] = x_ref[...] + gate * emb


def _ple(hp, w_gate, p, w_proj, x, bm=512, bn=1024):
    t, d = hp.shape
    pd = p.shape[1]
    n = w_gate.shape[1]
    blocks = (_nbytes((bm, d), BF16) + _nbytes((d, bn), BF16) + _nbytes((bm, pd), BF16)
              + _nbytes((pd, bn), BF16) + 2 * _nbytes((bm, bn), F32))
    return pl.pallas_call(
        _ple_kernel,
        out_shape=jax.ShapeDtypeStruct((t, n), F32),
        grid=(t // bm, n // bn),
        in_specs=[pl.BlockSpec((bm, d), lambda i, j: (i, 0)),
                  pl.BlockSpec((d, bn), lambda i, j: (0, j)),
                  pl.BlockSpec((bm, pd), lambda i, j: (i, 0)),
                  pl.BlockSpec((pd, bn), lambda i, j: (0, j)),
                  pl.BlockSpec((bm, bn), lambda i, j: (i, j))],
        out_specs=pl.BlockSpec((bm, bn), lambda i, j: (i, j)),
        compiler_params=_params(("parallel", "parallel"), blocks, 2 * _nbytes((bm, bn), F32)),
        name="ple",
    )(hp, w_gate, p, w_proj, x)


def _qkv_column_scale():
    s = HEAD_DIM ** -0.5
    parts = [(DA_QK, s), (DA_QK, 1.0), (DA_V, 1.0), (SW_Q, s), (SW_KV, 1.0), (SW_KV, 1.0)]
    return jnp.concatenate([jnp.full((1, w), v, F32) for w, v in parts], axis=1)


@jax.jit
def kernel(x, p, positions, rel_bias, norm_mix, w_in, da_lambda, da_subln, sw_sinks, w_br_a, w_br_b, w_out,
           norm_ffn, peer_wq, peer_k1, peer_k2, peer_u, peer_v, norm_ple, ple_gate, ple_proj, norm_final):
    del positions
    batch, seq, d = x.shape
    t = batch * seq
    depth = w_in.shape[0]
    xf = x.reshape(t, d)
    da_bias = _bias_tiles(rel_bias[:, :DA_HEADS], DA_BLK, None, True)
    sw_bias = _bias_tiles(rel_bias[:, DA_HEADS:], WINDOW, WINDOW, False)
    col_scale = _qkv_column_scale()
    tile = pl.BlockSpec((1, 1024), lambda i, j: (0, j))
    for i in range(depth):
        lam_init = 0.8 - 0.6 * math.exp(-0.3 * i)
        h = _rmsnorm(xf, norm_mix[i], BF16)
        qkv = _matmul(_mm_scale_kernel, h, _to_bf16(w_in, i, 0, QKV_WIDTH, 1024), [col_scale], [tile],
                      BF16, 1024, 1024, "proj_qkv")
        gates = _matmul(_mm_sigmoid_kernel, h, _to_bf16(w_in, i, QKV_WIDTH, w_in.shape[2] - QKV_WIDTH, 1024),
                        [], [], F32, 1024, 1024, "proj_gates")
        o_a = _diff_attention(qkv, da_bias, da_lambda[i], da_subln[i], lam_init, batch, seq)
        o_b = _sliding_attention(qkv, sw_bias, sw_sinks[i], batch, seq)
        merged = _merge(o_a, o_b, _to_bf16(w_br_a, i), _to_bf16(w_br_b, i), gates)
        xf = _matmul(_mm_residual_kernel, merged, _to_bf16(w_out, i), [xf],
                     [pl.BlockSpec((512, 1024), lambda i, j: (i, j))], F32, 512, 1024, "proj_out")
        hn, hn_chunked = _rmsnorm_chunked(xf, norm_ffn[i], PEER_UP_CHUNKS)
        q = _matmul(_mm_plain_kernel, hn, _to_bf16(peer_wq, i), [], [], BF16, 1024, 1024, "peer_query")
        cnt, e1, rank, e2 = _route(q, peer_k1[i].astype(BF16), peer_k2[i].astype(BF16))
        wt = _peer_up(_to_bf16(peer_u, i, layout="chunked", chunks=PEER_UP_CHUNKS), hn_chunked, cnt, e1, rank, e2)
        xf = _peer_down(_to_bf16(peer_v, i, layout="transposed"), wt, xf)
        hp = _rmsnorm(xf, norm_ple[i], BF16)
        xf = _ple(hp, _to_bf16(ple_gate, i), p[i].reshape(t, -1).astype(BF16), ple_proj[i].astype(BF16), xf)
    return _rmsnorm(xf, norm_final, F32).reshape(batch, seq, d)
```

```python
import functools
import math

import jax
import jax.numpy as jnp
from jax import lax
from jax.experimental import pallas as pl
from jax.experimental.pallas import tpu as pltpu

F32 = jnp.float32
BF16 = jnp.bfloat16
FP8 = jnp.float8_e4m3fn
FP8_TARGET = 240.0
FP8_TINY = 1e-30

HEAD_DIM = 128
DA_HEADS = 8
DA_V_DIM = 2 * HEAD_DIM
SW_Q_HEADS = 16
SW_KV_HEADS = 4
SW_GROUP = SW_Q_HEADS // SW_KV_HEADS
WINDOW = 128
N_BUCKETS = 32
MAX_EXACT = N_BUCKETS // 2
MAX_DIST = 128
NEG = -1e30
DA_QK = DA_HEADS * 2 * HEAD_DIM
DA_V = DA_HEADS * DA_V_DIM
SW_Q = SW_Q_HEADS * HEAD_DIM
SW_KV = SW_KV_HEADS * HEAD_DIM
QKV_WIDTH = 3 * DA_QK + SW_Q + 2 * SW_KV
PEER_HEADS = 8
N_KEYS = 128
PEER_TOPK = 16
KEY_DIM = 128
EPS = 1e-6

V7X_VMEM_REQUEST_CAP = 60 * 1024 * 1024
BF16_ROWS = 16
DA_BLK = 512
PEER_UP_TE = 1024
PEER_UP_CHUNKS = PEER_UP_TE // N_KEYS
PEER_UP_UNROLL = 4


def _nbytes(shape, dtype):
    return math.prod(shape) * jnp.dtype(dtype).itemsize


def _params(semantics, block_bytes, scratch_bytes=0, flags=None):
    need = int(1.25 * (2 * block_bytes + scratch_bytes)) + (4 << 20)
    return pltpu.CompilerParams(dimension_semantics=semantics,
                                vmem_limit_bytes=min(need, V7X_VMEM_REQUEST_CAP), flags=flags)


def _dot_nt(a, b):
    return lax.dot_general(a, b, (((1,), (1,)), ((), ())), preferred_element_type=F32)


def _rmsnorm_kernel(x_ref, g_ref, o_ref):
    x = x_ref[...]
    y = x * lax.rsqrt(jnp.mean(x * x, axis=-1, keepdims=True) + EPS)
    o_ref[...] = (y * g_ref[...]).astype(o_ref.dtype)


def _rmsnorm(x, g, out_dtype, rows=256):
    t, d = x.shape
    blocks = _nbytes((rows, d), F32) + _nbytes((rows, d), out_dtype)
    return pl.pallas_call(
        _rmsnorm_kernel,
        out_shape=jax.ShapeDtypeStruct((t, d), out_dtype),
        grid=(t // rows,),
        in_specs=[pl.BlockSpec((rows, d), lambda i: (i, 0)),
                  pl.BlockSpec((1, d), lambda i: (0, 0))],
        out_specs=pl.BlockSpec((rows, d), lambda i: (i, 0)),
        compiler_params=_params(("parallel",), blocks, _nbytes((rows, d), F32)),
        name="rmsnorm",
    )(x, g.reshape(1, d))


def _rmsnorm_fp8_kernel(x_ref, g_ref, o_ref, oc_ref, inv_ref):
    x = x_ref[...]
    y = x * lax.rsqrt(jnp.mean(x * x, axis=-1, keepdims=True) + EPS) * g_ref[...]
    o_ref[...] = y.astype(o_ref.dtype)
    amax = jnp.maximum(jnp.max(jnp.abs(y), axis=-1, keepdims=True), FP8_TINY)
    inv_ref[...] = amax * (1.0 / FP8_TARGET)
    y8 = y * (FP8_TARGET / amax)
    dc = oc_ref.shape[2]
    for k in range(oc_ref.shape[0]):
        oc_ref[k] = y8[:, k * dc:(k + 1) * dc].astype(oc_ref.dtype)


def _rmsnorm_fp8(x, g, chunks, rows=256):
    t, d = x.shape
    dc = d // chunks
    blocks = _nbytes((rows, d), F32) + _nbytes((rows, d), BF16) + _nbytes((rows, d), FP8) + _nbytes((rows, 128), F32)
    return pl.pallas_call(
        _rmsnorm_fp8_kernel,
        out_shape=(jax.ShapeDtypeStruct((t, d), BF16), jax.ShapeDtypeStruct((chunks, t, dc), FP8),
                   jax.ShapeDtypeStruct((t, 1), F32)),
        grid=(t // rows,),
        in_specs=[pl.BlockSpec((rows, d), lambda i: (i, 0)),
                  pl.BlockSpec((1, d), lambda i: (0, 0))],
        out_specs=(pl.BlockSpec((rows, d), lambda i: (i, 0)),
                   pl.BlockSpec((chunks, rows, dc), lambda i: (0, i, 0)),
                   pl.BlockSpec((rows, 1), lambda i: (i, 0))),
        compiler_params=_params(("parallel",), blocks, 2 * _nbytes((rows, d), F32)),
        name="rmsnorm_fp8",
    )(x, g.reshape(1, d))


CAST_BLOCK_BYTES = 8 << 20


def _cast_kernel(s_ref, x_ref, o_ref):
    o_ref[...] = (x_ref[...] * s_ref[0, 0]).astype(o_ref.dtype)


def _cast_chunked_kernel(s_ref, x_ref, o_ref):
    dc = o_ref.shape[2]
    for k in range(o_ref.shape[0]):
        o_ref[k] = (x_ref[:, k * dc:(k + 1) * dc] * s_ref[0, 0]).astype(o_ref.dtype)


def _cast_transposed_kernel(s_ref, x_ref, o_ref):
    o_ref[...] = (x_ref[...] * s_ref[0, 0]).T.astype(o_ref.dtype)


def _narrow(w, layer, col0=0, ncols=None, bc=None, layout="plain", chunks=None, dtype=BF16, scale=None):
    _, r, c = w.shape
    ncols = c if ncols is None else ncols
    bc = ncols if bc is None else bc
    br = min(r, CAST_BLOCK_BYTES // (bc * 4))
    grid = (r // br, ncols // bc)
    in_spec = pl.BlockSpec((None, br, bc), lambda i, j: (layer, i, col0 // bc + j))
    if layout == "plain":
        body, shape = _cast_kernel, (r, ncols)
        out_spec = pl.BlockSpec((br, bc), lambda i, j: (i, j))
    elif layout == "chunked":
        assert bc == ncols
        body, shape = _cast_chunked_kernel, (chunks, r, ncols // chunks)
        out_spec = pl.BlockSpec((chunks, br, ncols // chunks), lambda i, j: (0, i, 0))
    else:
        assert bc == ncols
        body, shape = _cast_transposed_kernel, (ncols, r)
        out_spec = pl.BlockSpec((ncols, br), lambda i, j: (0, i))
    scale = jnp.ones((1, 1), F32) if scale is None else scale
    return pl.pallas_call(
        body,
        out_shape=jax.ShapeDtypeStruct(shape, dtype),
        grid=grid,
        in_specs=[pl.BlockSpec(memory_space=pltpu.SMEM), in_spec],
        out_specs=out_spec,
        compiler_params=_params(("parallel", "parallel"), _nbytes((br, bc), F32) + _nbytes((br, bc), dtype),
                                2 * _nbytes((br, bc), F32)),
        name="narrow_" + layout,
    )(scale, w)


def _abs_max_kernel(x_ref, o_ref):
    @pl.when(pl.program_id(0) == 0)
    def _():
        o_ref[...] = jnp.zeros(o_ref.shape, F32)

    o_ref[...] = jnp.maximum(o_ref[...], jnp.max(jnp.abs(x_ref[...]), axis=0, keepdims=True))


def _fp8_scale(w, layer):
    _, r, c = w.shape
    br = min(r, CAST_BLOCK_BYTES // (c * 4))
    col_max = pl.pallas_call(
        _abs_max_kernel,
        out_shape=jax.ShapeDtypeStruct((1, c), F32),
        grid=(r // br,),
        in_specs=[pl.BlockSpec((None, br, c), lambda i: (layer, i, 0))],
        out_specs=pl.BlockSpec((1, c), lambda i: (0, 0)),
        compiler_params=_params(("arbitrary",), _nbytes((br, c), F32), _nbytes((br, c), F32)),
        name="abs_max",
    )(w)
    return (FP8_TARGET / jnp.maximum(jnp.max(col_max), FP8_TINY)).reshape(1, 1)


def _mm_scale_kernel(a_ref, b_ref, s_ref, o_ref):
    acc = jnp.dot(a_ref[...], b_ref[...], preferred_element_type=F32)
    o_ref[...] = (acc * s_ref[...]).astype(o_ref.dtype)


def _mm_sigmoid_kernel(a_ref, b_ref, o_ref):
    acc = jnp.dot(a_ref[...], b_ref[...], preferred_element_type=F32)
    o_ref[...] = jax.nn.sigmoid(acc).astype(o_ref.dtype)


def _mm_plain_kernel(a_ref, b_ref, o_ref):
    o_ref[...] = jnp.dot(a_ref[...], b_ref[...], preferred_element_type=F32).astype(o_ref.dtype)


def _mm_residual_kernel(a_ref, b_ref, x_ref, o_ref):
    o_ref[...] = x_ref[...] + jnp.dot(a_ref[...], b_ref[...], preferred_element_type=F32)


def _matmul(body, a, b, extra, extra_specs, out_dtype, bm, bn, name):
    m, k = a.shape
    n = b.shape[1]
    blocks = (_nbytes((bm, k), a.dtype) + _nbytes((k, bn), b.dtype) + _nbytes((bm, bn), out_dtype)
              + sum(_nbytes(s.block_shape, e.dtype) for s, e in zip(extra_specs, extra)))
    return pl.pallas_call(
        body,
        out_shape=jax.ShapeDtypeStruct((m, n), out_dtype),
        grid=(m // bm, n // bn),
        in_specs=[pl.BlockSpec((bm, k), lambda i, j: (i, 0)),
                  pl.BlockSpec((k, bn), lambda i, j: (0, j))] + list(extra_specs),
        out_specs=pl.BlockSpec((bm, bn), lambda i, j: (i, j)),
        compiler_params=_params(("parallel", "parallel"), blocks, _nbytes((bm, bn), F32)),
        name=name,
    )(a, b, *extra)


def _bias_kernel(tab_ref, o_ref, *, blk, window, rebase):
    h = pl.program_id(0)
    r = lax.broadcasted_iota(jnp.int32, (blk, blk), 0)
    c = lax.broadcasted_iota(jnp.int32, (blk, blk), 1)
    base = tab_ref[N_BUCKETS - 1, h] if rebase else 0.0
    for delta in (0, 1):
        rel = r - c + delta * blk
        n = jnp.maximum(rel, 0)
        nf = jnp.maximum(n, 1).astype(F32)
        large = MAX_EXACT + (jnp.log(nf / MAX_EXACT) / math.log(MAX_DIST / MAX_EXACT)
                             * (N_BUCKETS - MAX_EXACT)).astype(jnp.int32)
        large = jnp.minimum(large, N_BUCKETS - 1)
        bucket = jnp.where(n < MAX_EXACT, n, large)
        bias = jnp.zeros((blk, blk), F32)
        for b in range(N_BUCKETS):
            bias = jnp.where(bucket == b, tab_ref[b, h] - base, bias)
        mask = rel >= 0
        if window is not None:
            mask = mask & (rel < window)
        o_ref[0, delta] = jnp.where(mask, bias, NEG)


def _bias_tiles(tab, blk, window, rebase):
    heads = tab.shape[1]
    return pl.pallas_call(
        functools.partial(_bias_kernel, blk=blk, window=window, rebase=rebase),
        out_shape=jax.ShapeDtypeStruct((heads, 2, blk, blk), F32),
        grid=(heads,),
        in_specs=[pl.BlockSpec(memory_space=pltpu.SMEM)],
        out_specs=pl.BlockSpec((1, 2, blk, blk), lambda h: (h, 0, 0, 0)),
        compiler_params=_params(("parallel",), _nbytes((2, blk, blk), F32), 4 * _nbytes((blk, blk), F32)),
        name="bias_tiles",
    )(tab)


def _da_kernel(q_ref, k_ref, v_ref, bias_ref, lam_ref, g_ref, o_ref, *, lam_init):
    blk = q_ref.shape[0]
    qi = pl.program_id(2)
    lp = lam_ref[...]
    lam = (jnp.exp(jnp.sum(lp[0:1] * lp[1:2], axis=-1, keepdims=True))
           - jnp.exp(jnp.sum(lp[2:3] * lp[3:4], axis=-1, keepdims=True)) + lam_init)

    def softmax_pv(j, case):
        dims = slice(j * HEAD_DIM, (j + 1) * HEAD_DIM)
        q = q_ref[:, dims]
        spans = [(slice(case * blk, (case + 1) * blk), bias_ref[0, 0])]
        if case >= 1:
            spans.append((slice((case - 1) * blk, case * blk), bias_ref[0, 1]))
        if case >= 2:
            spans.append((slice(0, (case - 1) * blk), None))
        scores = []
        for rows, bias in spans:
            s = _dot_nt(q, k_ref[rows, dims])
            scores.append(s if bias is None else s + bias)
        m = functools.reduce(jnp.maximum, [jnp.max(s, axis=-1, keepdims=True) for s in scores])
        probs = [jnp.exp(s - m) for s in scores]
        norm = sum(jnp.sum(p, axis=-1, keepdims=True) for p in probs)
        out = sum(jnp.dot(p.astype(BF16), v_ref[rows, :], preferred_element_type=F32)
                  for p, (rows, _) in zip(probs, spans))
        return out / norm

    for case in range(k_ref.shape[0] // blk):
        @pl.when(qi == case)
        def _(case=case):
            o = softmax_pv(0, case) - lam * softmax_pv(1, case)
            y = o * lax.rsqrt(jnp.mean(o * o, axis=-1, keepdims=True) + EPS)
            o_ref[...] = ((y * g_ref[...]) * (1.0 - lam_init)).astype(o_ref.dtype)


def _diff_attention(qkv, bias, lam_p, subln_g, lam_init, batch, seq):
    blk = DA_BLK
    nq = seq // blk
    blocks = (2 * _nbytes((blk, DA_V_DIM), BF16) + 2 * _nbytes((seq, DA_V_DIM), BF16)
              + _nbytes((2, blk, blk), F32))
    scratch = 6 * _nbytes((blk, seq), F32)
    k_col0 = DA_QK // DA_V_DIM
    v_col0 = 2 * DA_QK // DA_V_DIM
    return pl.pallas_call(
        functools.partial(_da_kernel, lam_init=lam_init),
        out_shape=jax.ShapeDtypeStruct((batch * seq, DA_V), BF16),
        grid=(batch, DA_HEADS, nq),
        in_specs=[pl.BlockSpec((blk, DA_V_DIM), lambda b, h, i: (b * nq + i, h)),
                  pl.BlockSpec((seq, DA_V_DIM), lambda b, h, i: (b, k_col0 + h)),
                  pl.BlockSpec((seq, DA_V_DIM), lambda b, h, i: (b, v_col0 + h)),
                  pl.BlockSpec((1, 2, blk, blk), lambda b, h, i: (h, 0, 0, 0)),
                  pl.BlockSpec((4, HEAD_DIM), lambda b, h, i: (0, 0)),
                  pl.BlockSpec((1, DA_V_DIM), lambda b, h, i: (0, 0))],
        out_specs=pl.BlockSpec((blk, DA_V_DIM), lambda b, h, i: (b * nq + i, h)),
        compiler_params=_params(("parallel", "parallel", "parallel"), blocks, scratch),
        name="diff_attention",
    )(qkv, qkv, qkv, bias, lam_p, subln_g.reshape(1, DA_V_DIM))


def _swa_kernel(q_ref, kc_ref, kp_ref, vc_ref, vp_ref, bias_ref, sink_ref, o_ref):
    n = pl.program_id(1)
    is_prev = lax.broadcasted_iota(jnp.int32, (1, 2 * WINDOW), 1) < WINDOW
    no_prev = jnp.where(is_prev & (n == 0), NEG, 0.0).astype(F32)
    for hk in range(SW_KV_HEADS):
        cols = slice(hk * HEAD_DIM, (hk + 1) * HEAD_DIM)
        heads = [slice((hk * SW_GROUP + g) * HEAD_DIM, (hk * SW_GROUP + g + 1) * HEAD_DIM) for g in range(SW_GROUP)]
        q = jnp.concatenate([q_ref[:, hd] for hd in heads], axis=0)
        k = jnp.concatenate([kp_ref[:, cols], kc_ref[:, cols]], axis=0)
        v = jnp.concatenate([vp_ref[:, cols], vc_ref[:, cols]], axis=0)
        s = _dot_nt(q, k) + bias_ref[hk] + no_prev
        sink = sink_ref[hk]
        m = jnp.maximum(jnp.max(s, axis=-1, keepdims=True), sink)
        e = jnp.exp(s - m)
        den = jnp.sum(e, axis=-1, keepdims=True) + jnp.exp(sink - m)
        o = jnp.dot(e.astype(BF16), v, preferred_element_type=F32) / den
        for g, hd in enumerate(heads):
            o_ref[:, hd] = o[g * WINDOW:(g + 1) * WINDOW, :].astype(o_ref.dtype)


def _sliding_attention(qkv, bias_tiles, sinks, batch, seq):
    nb = seq // WINDOW
    q_col = 3 * DA_QK // SW_Q
    k_col = (3 * DA_QK + SW_Q) // SW_KV
    v_col = k_col + 1
    cur = lambda b, n: b * nb + n
    prev = lambda b, n: b * nb + jnp.maximum(n - 1, 0)
    rows = SW_GROUP * WINDOW
    bias = bias_tiles.reshape(SW_KV_HEADS, SW_GROUP, 2, WINDOW, WINDOW)[:, :, ::-1]
    bias = bias.transpose(0, 1, 3, 2, 4).reshape(SW_KV_HEADS, rows, 2 * WINDOW)
    sink_cols = jnp.repeat(sinks.astype(F32).reshape(SW_KV_HEADS, SW_GROUP), WINDOW, axis=1).reshape(
        SW_KV_HEADS, rows, 1)
    blocks = (2 * _nbytes((WINDOW, SW_Q), BF16) + 4 * _nbytes((WINDOW, SW_KV), BF16)
              + _nbytes((SW_KV_HEADS, rows, 2 * WINDOW), F32) + _nbytes((SW_KV_HEADS, rows, 128), F32))
    return pl.pallas_call(
        _swa_kernel,
        out_shape=jax.ShapeDtypeStruct((batch * seq, SW_Q), BF16),
        grid=(batch, nb),
        in_specs=[pl.BlockSpec((WINDOW, SW_Q), lambda b, n: (cur(b, n), q_col)),
                  pl.BlockSpec((WINDOW, SW_KV), lambda b, n: (cur(b, n), k_col)),
                  pl.BlockSpec((WINDOW, SW_KV), lambda b, n: (prev(b, n), k_col)),
                  pl.BlockSpec((WINDOW, SW_KV), lambda b, n: (cur(b, n), v_col)),
                  pl.BlockSpec((WINDOW, SW_KV), lambda b, n: (prev(b, n), v_col)),
                  pl.BlockSpec((SW_KV_HEADS, rows, 2 * WINDOW), lambda b, n: (0, 0, 0)),
                  pl.BlockSpec((SW_KV_HEADS, rows, 1), lambda b, n: (0, 0, 0))],
        out_specs=pl.BlockSpec((WINDOW, SW_Q), lambda b, n: (cur(b, n), 0)),
        compiler_params=_params(("parallel", "parallel"), blocks, 16 * _nbytes((rows, 2 * WINDOW), F32)),
        name="sliding_attention",
    )(qkv, qkv, qkv, qkv, qkv, bias, sink_cols)


def _merge_kernel(oa_ref, ob_ref, wa_ref, wb_ref, ga_ref, gb_ref, o_ref):
    a = jnp.dot(oa_ref[...], wa_ref[...], preferred_element_type=F32)
    b = jnp.dot(ob_ref[...], wb_ref[...], preferred_element_type=F32)
    o_ref[...] = (ga_ref[...] * a + gb_ref[...] * b).astype(o_ref.dtype)


def _merge(o_a, o_b, w_a, w_b, gates, bm=1024, bn=1024):
    t, ka = o_a.shape
    kb = o_b.shape[1]
    d = w_a.shape[1]
    nj = d // bn
    blocks = (_nbytes((bm, ka), BF16) + _nbytes((bm, kb), BF16) + _nbytes((ka, bn), BF16)
              + _nbytes((kb, bn), BF16) + 2 * _nbytes((bm, bn), gates.dtype) + _nbytes((bm, bn), BF16))
    return pl.pallas_call(
        _merge_kernel,
        out_shape=jax.ShapeDtypeStruct((t, d), BF16),
        grid=(t // bm, nj),
        in_specs=[pl.BlockSpec((bm, ka), lambda i, j: (i, 0)),
                  pl.BlockSpec((bm, kb), lambda i, j: (i, 0)),
                  pl.BlockSpec((ka, bn), lambda i, j: (0, j)),
                  pl.BlockSpec((kb, bn), lambda i, j: (0, j)),
                  pl.BlockSpec((bm, bn), lambda i, j: (i, j)),
                  pl.BlockSpec((bm, bn), lambda i, j: (i, nj + j))],
        out_specs=pl.BlockSpec((bm, bn), lambda i, j: (i, j)),
        compiler_params=_params(("parallel", "parallel"), blocks, 2 * _nbytes((bm, bn), F32)),
        name="merge",
    )(o_a, o_b, w_a, w_b, gates, gates)


def _sort_pairs(n):
    pairs = []

    def merge(lo, hi, r):
        step = r * 2
        if step < hi - lo:
            merge(lo, hi, step)
            merge(lo + r, hi, step)
            pairs.extend((i, i + r) for i in range(lo + r, hi - r, step))
        else:
            pairs.append((lo, lo + r))

    def sort(lo, hi):
        if hi - lo >= 1:
            mid = lo + (hi - lo) // 2
            sort(lo, mid)
            sort(mid + 1, hi)
            merge(lo, hi, 1)

    sort(0, n - 1)
    return pairs


_SORT16 = _sort_pairs(PEER_TOPK)


def _sort_desc(xs):
    xs = list(xs)
    for i, j in _SORT16:
        xs[i], xs[j] = jnp.maximum(xs[i], xs[j]), jnp.minimum(xs[i], xs[j])
    return xs


def _merge_top(a, b):
    k = PEER_TOPK
    xs = [jnp.maximum(a[i], b[k - 1 - i]) for i in range(k)]
    d = k // 2
    while d >= 1:
        for i in range(k):
            if not i & d:
                xs[i], xs[i + d] = jnp.maximum(xs[i], xs[i + d]), jnp.minimum(xs[i], xs[i + d])
        d //= 2
    return xs


def _top16_over_rows(s):
    groups = [s[a * 8:(a + 1) * 8, :] for a in range(s.shape[0] // 8)]
    xs = _sort_desc(groups)
    for shift in (4, 2, 1):
        xs = _merge_top(xs, [pltpu.roll(x, shift, 0) for x in xs])
    return xs


def _route_kernel(q_ref, k1_ref, k2_ref, cnt_ref, e1_ref, rank_ref, e2_ref):
    q = q_ref[...]
    s1 = _dot_nt(k1_ref[...], q[:, :KEY_DIM])
    s2 = _dot_nt(k2_ref[...], q[:, KEY_DIM:])
    v1 = _top16_over_rows(s1)
    v2 = _top16_over_rows(s2)
    k = PEER_TOPK
    top = [v1[0] + v2[b] for b in range(k)]
    rest = [v1[a] + v2[b] for a in range(1, k) for b in range(k) if (a + 1) * (b + 1) <= k]
    pad = jnp.full(top[0].shape, -jnp.inf, F32)
    rest = rest + [pad] * (-len(rest) % k)
    for g in range(len(rest) // k):
        top = _merge_top(top, _sort_desc(rest[g * k:(g + 1) * k]))
    z = jnp.ones_like(top[0])
    for c in top[1:]:
        z = z + jnp.exp(c - top[0])
    tau = top[k - 1][0:1]
    cnt = jnp.zeros_like(s1)
    rank = jnp.zeros_like(s2)
    for b in range(k):
        best = v2[b][0:1]
        cnt = cnt + jnp.where(s1 + best >= tau, 1.0, 0.0)
        rank = rank + jnp.where(best > s2, 1.0, 0.0)
    cnt_ref[0] = cnt
    rank_ref[0] = rank.astype(rank_ref.dtype)
    e1_ref[0] = jnp.exp(s1 - v1[0][0:1]) / z[0:1]
    e2_ref[0] = jnp.exp(s2 - v2[0][0:1]).astype(e2_ref.dtype)


def _route(q, k1, k2, tm=1024):
    t = q.shape[0]
    wide = jax.ShapeDtypeStruct((PEER_HEADS, N_KEYS, t), F32)
    narrow = jax.ShapeDtypeStruct((PEER_HEADS, N_KEYS, t), BF16)
    spec = pl.BlockSpec((1, N_KEYS, tm), lambda i, h: (h, 0, i))
    blocks = _nbytes((tm, 2 * KEY_DIM), BF16) + 3 * _nbytes((N_KEYS, tm), F32)
    return pl.pallas_call(
        _route_kernel,
        out_shape=(wide, wide, narrow, narrow),
        grid=(t // tm, PEER_HEADS),
        in_specs=[pl.BlockSpec((tm, 2 * KEY_DIM), lambda i, h: (i, h)),
                  pl.BlockSpec((N_KEYS, KEY_DIM), lambda i, h: (0, 0)),
                  pl.BlockSpec((N_KEYS, KEY_DIM), lambda i, h: (0, 0))],
        out_specs=(spec, spec, spec, spec),
        compiler_params=_params(("parallel", "parallel"), blocks, 24 * _nbytes((N_KEYS, tm), F32)),
        name="peer_route",
    )(q, k1, k2)


def _peer_up_kernel(u_ref, h_ref, inv_ref, cnt_ref, e1_ref, rank_ref, e2_ref, o_ref, act_a, act_b):
    s = pl.program_id(0)

    @pl.when(s == 0)
    def _():
        act_a[...] = jnp.zeros(act_a.shape, F32)
        act_b[...] = jnp.zeros(act_b.shape, F32)

    def step(act_new, act_old):
        def chunk(r, carry):
            act_new[...] += _dot_nt(u_ref[r], h_ref[r])
            rows = pl.ds(pl.multiple_of(r * N_KEYS, N_KEYS), N_KEYS)
            cnt_rows = [cnt_ref[h, pl.ds(r, 1), :] for h in range(PEER_HEADS)]
            e1_rows = [e1_ref[h, pl.ds(r, 1), :] for h in range(PEER_HEADS)]
            packed = (N_KEYS // BF16_ROWS, BF16_ROWS, 128)
            for c in range(act_old.shape[1] // 128):
                cols = slice(c * 128, (c + 1) * 128)
                act = act_old[rows, cols]
                act_old[rows, cols] = jnp.zeros_like(act)
                act = act * inv_ref[:, cols]
                gate = None
                for h in range(PEER_HEADS):
                    cnt = jnp.broadcast_to(cnt_rows[h][:, cols], packed[1:]).astype(BF16)
                    e1 = jnp.broadcast_to(e1_rows[h][:, cols], packed[1:]).astype(BF16)
                    routed = rank_ref[h, :, cols].reshape(packed) < cnt
                    term = jnp.where(routed, e2_ref[h, :, cols].reshape(packed) * e1, jnp.zeros((), BF16))
                    gate = term if gate is None else gate + term
                w = jax.nn.gelu(act).astype(BF16).reshape(packed) * gate
                o_ref[rows, cols] = w.reshape(N_KEYS, 128)
            return carry

        lax.fori_loop(0, u_ref.shape[0], chunk, 0, unroll=PEER_UP_UNROLL)

    @pl.when(s % 2 == 0)
    def _():
        step(act_a, act_b)

    @pl.when(s % 2 == 1)
    def _():
        step(act_b, act_a)


def _peer_up(u, hn, inv_scale, cnt, e1, rank, e2, te=PEER_UP_TE, tm=512):
    chunks, n_exp, dc = u.shape
    t = hn.shape[1]
    rows = te // N_KEYS
    assert chunks == rows
    nj = n_exp // te
    steps = (t // tm) * nj

    def tile_of(step):
        return step // nj, step % nj

    def now(s):
        return tile_of(jnp.minimum(s, steps - 1))

    def lag(s):
        return tile_of(jnp.maximum(s - 1, 0))

    row_spec = pl.BlockSpec((PEER_HEADS, rows, tm), lambda s: (0, lag(s)[1], lag(s)[0]))
    full_spec = pl.BlockSpec((PEER_HEADS, N_KEYS, tm), lambda s: (0, 0, lag(s)[0]))
    blocks = (_nbytes((chunks, te, dc), FP8) + _nbytes((chunks, tm, dc), FP8)
              + 2 * _nbytes((PEER_HEADS, N_KEYS, tm), BF16)
              + 2 * _nbytes((PEER_HEADS, rows, tm), F32) + _nbytes((te, tm), BF16))
    return pl.pallas_call(
        _peer_up_kernel,
        out_shape=jax.ShapeDtypeStruct((n_exp, t), BF16),
        grid=(steps + 1,),
        in_specs=[pl.BlockSpec((chunks, te, dc), lambda s: (0, now(s)[1], 0)),
                  pl.BlockSpec((chunks, tm, dc), lambda s: (0, now(s)[0], 0)),
                  pl.BlockSpec((1, tm), lambda s: (0, lag(s)[0])),
                  row_spec, row_spec, full_spec, full_spec],
        out_specs=pl.BlockSpec((te, tm), lambda s: (lag(s)[1], lag(s)[0])),
        scratch_shapes=[pltpu.VMEM((te, tm), F32), pltpu.VMEM((te, tm), F32)],
        compiler_params=_params(("arbitrary",), blocks, 3 * _nbytes((te, tm), F32)),
        name="peer_up",
    )(u, hn, inv_scale, cnt, e1, rank, e2)


def _peer_down_kernel(inv_ref, vt_ref, w_ref, x_ref, o_ref, acc_ref):
    kk = pl.program_id(2)

    @pl.when(kk == 0)
    def _():
        acc_ref[...] = jnp.zeros(acc_ref.shape, F32)

    w = w_ref[...]
    amax = jnp.max(jnp.abs(w), axis=0, keepdims=True).astype(F32)
    shift = jnp.floor(jnp.log2(FP8_TARGET / jnp.maximum(amax, FP8_TINY)))
    w8 = (w * jnp.exp2(shift).astype(BF16)).astype(FP8)
    part = jnp.dot(vt_ref[...], w8, preferred_element_type=F32)
    acc_ref[...] += part * (jnp.exp2(-shift) * inv_ref[0, 0])

    @pl.when(kk == pl.num_programs(2) - 1)
    def _():
        o_ref[...] = x_ref[...] + acc_ref[...].T


def _peer_down(vt, inv_scale, wt, x, bd=1024, bt=1024, tk=2048):
    d, n_exp = vt.shape
    t = wt.shape[1]
    blocks = (_nbytes((bd, tk), FP8) + _nbytes((tk, bt), BF16) + 2 * _nbytes((bt, bd), F32))
    return pl.pallas_call(
        _peer_down_kernel,
        out_shape=jax.ShapeDtypeStruct((t, d), F32),
        grid=(d // bd, t // bt, n_exp // tk),
        in_specs=[pl.BlockSpec(memory_space=pltpu.SMEM),
                  pl.BlockSpec((bd, tk), lambda i, j, k: (i, k)),
                  pl.BlockSpec((tk, bt), lambda i, j, k: (k, j)),
                  pl.BlockSpec((bt, bd), lambda i, j, k: (j, i))],
        out_specs=pl.BlockSpec((bt, bd), lambda i, j, k: (j, i)),
        scratch_shapes=[pltpu.VMEM((bd, bt), F32)],
        compiler_params=_params(("parallel", "parallel", "arbitrary"), blocks,
                                3 * _nbytes((bd, bt), F32) + 2 * _nbytes((tk, bt), BF16)),
        name="peer_down",
    )(inv_scale, vt, wt, x)


def _ple_kernel(h_ref, wg_ref, p_ref, wp_ref, x_ref, o_ref):
    gate = jax.nn.sigmoid(jnp.dot(h_ref[...], wg_ref[...], preferred_element_type=F32))
    emb = jnp.dot(p_ref[...], wp_ref[...], preferred_element_type=F32)
    o_ref[...] = x_ref[...] + gate * emb


def _ple(hp, w_gate, p, w_proj, x, bm=512, bn=1024):
    t, d = hp.shape
    pd = p.shape[1]
    n = w_gate.shape[1]
    blocks = (_nbytes((bm, d), BF16) + _nbytes((d, bn), BF16) + _nbytes((bm, pd), BF16)
              + _nbytes((pd, bn), BF16) + 2 * _nbytes((bm, bn), F32))
    return pl.pallas_call(
        _ple_kernel,
        out_shape=jax.ShapeDtypeStruct((t, n), F32),
        grid=(t // bm, n // bn),
        in_specs=[pl.BlockSpec((bm, d), lambda i, j: (i, 0)),
                  pl.BlockSpec((d, bn), lambda i, j: (0, j)),
                  pl.BlockSpec((bm, pd), lambda i, j: (i, 0)),
                  pl.BlockSpec((pd, bn), lambda i, j: (0, j)),
                  pl.BlockSpec((bm, bn), lambda i, j: (i, j))],
        out_specs=pl.BlockSpec((bm, bn), lambda i, j: (i, j)),
        compiler_params=_params(("parallel", "parallel"), blocks, 2 * _nbytes((bm, bn), F32)),
        name="ple",
    )(hp, w_gate, p, w_proj, x)


def _qkv_column_scale():
    s = HEAD_DIM ** -0.5
    parts = [(DA_QK, s), (DA_QK, 1.0), (DA_V, 1.0), (SW_Q, s), (SW_KV, 1.0), (SW_KV, 1.0)]
    return jnp.concatenate([jnp.full((1, w), v, F32) for w, v in parts], axis=1)


@jax.jit
def kernel(x, p, positions, rel_bias, norm_mix, w_in, da_lambda, da_subln, sw_sinks, w_br_a, w_br_b, w_out,
           norm_ffn, peer_wq, peer_k1, peer_k2, peer_u, peer_v, norm_ple, ple_gate, ple_proj, norm_final):
    del positions
    batch, seq, d = x.shape
    t = batch * seq
    depth = w_in.shape[0]
    xf = x.reshape(t, d)
    da_bias = _bias_tiles(rel_bias[:, :DA_HEADS], DA_BLK, None, True)
    sw_bias = _bias_tiles(rel_bias[:, DA_HEADS:], WINDOW, WINDOW, False)
    col_scale = _qkv_column_scale()
    tile = pl.BlockSpec((1, 1024), lambda i, j: (0, j))
    for i in range(depth):
        lam_init = 0.8 - 0.6 * math.exp(-0.3 * i)
        h = _rmsnorm(xf, norm_mix[i], BF16)
        qkv = _matmul(_mm_scale_kernel, h, _narrow(w_in, i, 0, QKV_WIDTH, 1024), [col_scale], [tile],
                      BF16, 1024, 1024, "proj_qkv")
        gates = _matmul(_mm_sigmoid_kernel, h, _narrow(w_in, i, QKV_WIDTH, w_in.shape[2] - QKV_WIDTH, 1024),
                        [], [], F32, 1024, 1024, "proj_gates")
        o_a = _diff_attention(qkv, da_bias, da_lambda[i], da_subln[i], lam_init, batch, seq)
        o_b = _sliding_attention(qkv, sw_bias, sw_sinks[i], batch, seq)
        merged = _merge(o_a, o_b, _narrow(w_br_a, i), _narrow(w_br_b, i), gates)
        xf = _matmul(_mm_residual_kernel, merged, _narrow(w_out, i), [xf],
                     [pl.BlockSpec((512, 1024), lambda i, j: (i, j))], F32, 512, 1024, "proj_out")
        hn, hn8, hn_inv = _rmsnorm_fp8(xf, norm_ffn[i], PEER_UP_CHUNKS)
        q = _matmul(_mm_plain_kernel, hn, _narrow(peer_wq, i), [], [], BF16, 1024, 1024, "peer_query")
        cnt, e1, rank, e2 = _route(q, peer_k1[i].astype(BF16), peer_k2[i].astype(BF16))
        su, sv = _fp8_scale(peer_u, i), _fp8_scale(peer_v, i)
        u8 = _narrow(peer_u, i, layout="chunked", chunks=PEER_UP_CHUNKS, dtype=FP8, scale=su)
        wt = _peer_up(u8, hn8, hn_inv.reshape(1, t) / su, cnt, e1, rank, e2)
        xf = _peer_down(_narrow(peer_v, i, layout="transposed", dtype=FP8, scale=sv), 1.0 / sv, wt, xf)
        hp = _rmsnorm(xf, norm_ple[i], BF16)
        xf = _ple(hp, _narrow(ple_gate, i), p[i].reshape(t, -1).astype(BF16), ple_proj[i].astype(BF16), xf)
    return _rmsnorm(xf, norm_final, F32).reshape(batch, seq, d)
```

```python
import functools
import math

import jax
import jax.numpy as jnp
from jax import lax
from jax.experimental import pallas as pl
from jax.experimental.pallas import tpu as pltpu

F32 = jnp.float32
BF16 = jnp.bfloat16
FP8 = jnp.float8_e4m3fn
FP8_TARGET = 240.0
FP8_TINY = 1e-30

HEAD_DIM = 128
DA_HEADS = 8
DA_V_DIM = 2 * HEAD_DIM
SW_Q_HEADS = 16
SW_KV_HEADS = 4
SW_GROUP = SW_Q_HEADS // SW_KV_HEADS
WINDOW = 128
N_BUCKETS = 32
MAX_EXACT = N_BUCKETS // 2
MAX_DIST = 128
NEG = -1e30
DA_QK = DA_HEADS * 2 * HEAD_DIM
DA_V = DA_HEADS * DA_V_DIM
SW_Q = SW_Q_HEADS * HEAD_DIM
SW_KV = SW_KV_HEADS * HEAD_DIM
QKV_WIDTH = 3 * DA_QK + SW_Q + 2 * SW_KV
PEER_HEADS = 8
N_KEYS = 128
PEER_TOPK = 16
KEY_DIM = 128
EPS = 1e-6

V7X_VMEM_REQUEST_CAP = 60 * 1024 * 1024
BF16_ROWS = 16
DA_BLK = 512
PEER_UP_TE = 1024
PEER_UP_CHUNKS = 4
PEER_UP_UNROLL = 2


def _nbytes(shape, dtype):
    return math.prod(shape) * jnp.dtype(dtype).itemsize


def _params(semantics, block_bytes, scratch_bytes=0, flags=None):
    need = int(1.25 * (2 * block_bytes + scratch_bytes)) + (4 << 20)
    return pltpu.CompilerParams(dimension_semantics=semantics,
                                vmem_limit_bytes=min(need, V7X_VMEM_REQUEST_CAP), flags=flags)


def _pack_rows(x):
    return pltpu.bitcast(x, jnp.uint32)


def _unpack_rows(x):
    return pltpu.bitcast(x, BF16)


def _dot_nt(a, b):
    return lax.dot_general(a, b, (((1,), (1,)), ((), ())), preferred_element_type=F32)


def _rmsnorm_kernel(x_ref, g_ref, o_ref):
    x = x_ref[...]
    y = x * lax.rsqrt(jnp.mean(x * x, axis=-1, keepdims=True) + EPS)
    o_ref[...] = (y * g_ref[...]).astype(o_ref.dtype)


def _rmsnorm(x, g, out_dtype, rows=256):
    t, d = x.shape
    blocks = _nbytes((rows, d), F32) + _nbytes((rows, d), out_dtype)
    return pl.pallas_call(
        _rmsnorm_kernel,
        out_shape=jax.ShapeDtypeStruct((t, d), out_dtype),
        grid=(t // rows,),
        in_specs=[pl.BlockSpec((rows, d), lambda i: (i, 0)),
                  pl.BlockSpec((1, d), lambda i: (0, 0))],
        out_specs=pl.BlockSpec((rows, d), lambda i: (i, 0)),
        compiler_params=_params(("parallel",), blocks, _nbytes((rows, d), F32)),
        name="rmsnorm",
    )(x, g.reshape(1, d))


def _rmsnorm_fp8_kernel(x_ref, g_ref, o_ref, oc_ref, inv_ref):
    x = x_ref[...]
    y = x * lax.rsqrt(jnp.mean(x * x, axis=-1, keepdims=True) + EPS) * g_ref[...]
    o_ref[...] = y.astype(o_ref.dtype)
    amax = jnp.maximum(jnp.max(jnp.abs(y), axis=-1, keepdims=True), FP8_TINY)
    inv_ref[...] = amax * (1.0 / FP8_TARGET)
    y8 = y * (FP8_TARGET / amax)
    dc = oc_ref.shape[2]
    for k in range(oc_ref.shape[0]):
        oc_ref[k] = y8[:, k * dc:(k + 1) * dc].astype(oc_ref.dtype)


def _rmsnorm_fp8(x, g, chunks, rows=256):
    t, d = x.shape
    dc = d // chunks
    blocks = _nbytes((rows, d), F32) + _nbytes((rows, d), BF16) + _nbytes((rows, d), FP8) + _nbytes((rows, 128), F32)
    return pl.pallas_call(
        _rmsnorm_fp8_kernel,
        out_shape=(jax.ShapeDtypeStruct((t, d), BF16), jax.ShapeDtypeStruct((chunks, t, dc), FP8),
                   jax.ShapeDtypeStruct((t, 1), F32)),
        grid=(t // rows,),
        in_specs=[pl.BlockSpec((rows, d), lambda i: (i, 0)),
                  pl.BlockSpec((1, d), lambda i: (0, 0))],
        out_specs=(pl.BlockSpec((rows, d), lambda i: (i, 0)),
                   pl.BlockSpec((chunks, rows, dc), lambda i: (0, i, 0)),
                   pl.BlockSpec((rows, 1), lambda i: (i, 0))),
        compiler_params=_params(("parallel",), blocks, 2 * _nbytes((rows, d), F32)),
        name="rmsnorm_fp8",
    )(x, g.reshape(1, d))


CAST_BLOCK_BYTES = 8 << 20


def _cast_kernel(s_ref, x_ref, o_ref):
    o_ref[...] = (x_ref[...] * s_ref[0, 0]).astype(o_ref.dtype)


def _cast_chunked_kernel(s_ref, x_ref, o_ref):
    dc = o_ref.shape[2]
    for k in range(o_ref.shape[0]):
        o_ref[k] = (x_ref[:, k * dc:(k + 1) * dc] * s_ref[0, 0]).astype(o_ref.dtype)


def _cast_transposed_kernel(s_ref, x_ref, o_ref):
    o_ref[...] = (x_ref[...] * s_ref[0, 0]).T.astype(o_ref.dtype)


def _narrow(w, layer, col0=0, ncols=None, bc=None, layout="plain", chunks=None, dtype=BF16, scale=None):
    _, r, c = w.shape
    ncols = c if ncols is None else ncols
    bc = ncols if bc is None else bc
    br = min(r, CAST_BLOCK_BYTES // (bc * 4))
    grid = (r // br, ncols // bc)
    in_spec = pl.BlockSpec((None, br, bc), lambda i, j: (layer, i, col0 // bc + j))
    if layout == "plain":
        body, shape = _cast_kernel, (r, ncols)
        out_spec = pl.BlockSpec((br, bc), lambda i, j: (i, j))
    elif layout == "chunked":
        assert bc == ncols
        body, shape = _cast_chunked_kernel, (chunks, r, ncols // chunks)
        out_spec = pl.BlockSpec((chunks, br, ncols // chunks), lambda i, j: (0, i, 0))
    else:
        assert bc == ncols
        body, shape = _cast_transposed_kernel, (ncols, r)
        out_spec = pl.BlockSpec((ncols, br), lambda i, j: (0, i))
    scale = jnp.ones((1, 1), F32) if scale is None else scale
    return pl.pallas_call(
        body,
        out_shape=jax.ShapeDtypeStruct(shape, dtype),
        grid=grid,
        in_specs=[pl.BlockSpec(memory_space=pltpu.SMEM), in_spec],
        out_specs=out_spec,
        compiler_params=_params(("parallel", "parallel"), _nbytes((br, bc), F32) + _nbytes((br, bc), dtype),
                                2 * _nbytes((br, bc), F32)),
        name="narrow_" + layout,
    )(scale, w)


def _abs_max_kernel(x_ref, o_ref):
    @pl.when(pl.program_id(0) == 0)
    def _():
        o_ref[...] = jnp.zeros(o_ref.shape, F32)

    o_ref[...] = jnp.maximum(o_ref[...], jnp.max(jnp.abs(x_ref[...]), axis=0, keepdims=True))


def _fp8_scale(w, layer):
    _, r, c = w.shape
    br = min(r, CAST_BLOCK_BYTES // (c * 4))
    col_max = pl.pallas_call(
        _abs_max_kernel,
        out_shape=jax.ShapeDtypeStruct((1, c), F32),
        grid=(r // br,),
        in_specs=[pl.BlockSpec((None, br, c), lambda i: (layer, i, 0))],
        out_specs=pl.BlockSpec((1, c), lambda i: (0, 0)),
        compiler_params=_params(("arbitrary",), _nbytes((br, c), F32), _nbytes((br, c), F32)),
        name="abs_max",
    )(w)
    return (FP8_TARGET / jnp.maximum(jnp.max(col_max), FP8_TINY)).reshape(1, 1)


def _mm_scale_kernel(a_ref, b_ref, s_ref, o_ref):
    acc = jnp.dot(a_ref[...], b_ref[...], preferred_element_type=F32)
    o_ref[...] = (acc * s_ref[...]).astype(o_ref.dtype)


def _mm_sigmoid_kernel(a_ref, b_ref, o_ref):
    acc = jnp.dot(a_ref[...], b_ref[...], preferred_element_type=F32)
    o_ref[...] = jax.nn.sigmoid(acc).astype(o_ref.dtype)


def _mm_plain_kernel(a_ref, b_ref, o_ref):
    o_ref[...] = jnp.dot(a_ref[...], b_ref[...], preferred_element_type=F32).astype(o_ref.dtype)


def _mm_residual_kernel(a_ref, b_ref, x_ref, o_ref):
    o_ref[...] = x_ref[...] + jnp.dot(a_ref[...], b_ref[...], preferred_element_type=F32)


def _matmul(body, a, b, extra, extra_specs, out_dtype, bm, bn, name):
    m, k = a.shape
    n = b.shape[1]
    blocks = (_nbytes((bm, k), a.dtype) + _nbytes((k, bn), b.dtype) + _nbytes((bm, bn), out_dtype)
              + sum(_nbytes(s.block_shape, e.dtype) for s, e in zip(extra_specs, extra)))
    return pl.pallas_call(
        body,
        out_shape=jax.ShapeDtypeStruct((m, n), out_dtype),
        grid=(m // bm, n // bn),
        in_specs=[pl.BlockSpec((bm, k), lambda i, j: (i, 0)),
                  pl.BlockSpec((k, bn), lambda i, j: (0, j))] + list(extra_specs),
        out_specs=pl.BlockSpec((bm, bn), lambda i, j: (i, j)),
        compiler_params=_params(("parallel", "parallel"), blocks, _nbytes((bm, bn), F32)),
        name=name,
    )(a, b, *extra)


def _bias_kernel(tab_ref, o_ref, *, blk, window, rebase):
    h = pl.program_id(0)
    r = lax.broadcasted_iota(jnp.int32, (blk, blk), 0)
    c = lax.broadcasted_iota(jnp.int32, (blk, blk), 1)
    base = tab_ref[N_BUCKETS - 1, h] if rebase else 0.0
    for delta in (0, 1):
        rel = r - c + delta * blk
        n = jnp.maximum(rel, 0)
        nf = jnp.maximum(n, 1).astype(F32)
        large = MAX_EXACT + (jnp.log(nf / MAX_EXACT) / math.log(MAX_DIST / MAX_EXACT)
                             * (N_BUCKETS - MAX_EXACT)).astype(jnp.int32)
        large = jnp.minimum(large, N_BUCKETS - 1)
        bucket = jnp.where(n < MAX_EXACT, n, large)
        bias = jnp.zeros((blk, blk), F32)
        for b in range(N_BUCKETS):
            bias = jnp.where(bucket == b, tab_ref[b, h] - base, bias)
        mask = rel >= 0
        if window is not None:
            mask = mask & (rel < window)
        o_ref[0, delta] = jnp.where(mask, bias, NEG)


def _bias_tiles(tab, blk, window, rebase):
    heads = tab.shape[1]
    return pl.pallas_call(
        functools.partial(_bias_kernel, blk=blk, window=window, rebase=rebase),
        out_shape=jax.ShapeDtypeStruct((heads, 2, blk, blk), F32),
        grid=(heads,),
        in_specs=[pl.BlockSpec(memory_space=pltpu.SMEM)],
        out_specs=pl.BlockSpec((1, 2, blk, blk), lambda h: (h, 0, 0, 0)),
        compiler_params=_params(("parallel",), _nbytes((2, blk, blk), F32), 4 * _nbytes((blk, blk), F32)),
        name="bias_tiles",
    )(tab)


def _da_kernel(q_ref, k_ref, v_ref, bias_ref, lam_ref, g_ref, o_ref, *, lam_init):
    blk = q_ref.shape[0]
    qi = pl.program_id(2)
    lp = lam_ref[...]
    lam = (jnp.exp(jnp.sum(lp[0:1] * lp[1:2], axis=-1, keepdims=True))
           - jnp.exp(jnp.sum(lp[2:3] * lp[3:4], axis=-1, keepdims=True)) + lam_init)

    def softmax_pv(j, case):
        dims = slice(j * HEAD_DIM, (j + 1) * HEAD_DIM)
        q = q_ref[:, dims]
        spans = [(slice(case * blk, (case + 1) * blk), bias_ref[0, 0])]
        if case >= 1:
            spans.append((slice((case - 1) * blk, case * blk), bias_ref[0, 1]))
        if case >= 2:
            spans.append((slice(0, (case - 1) * blk), None))
        scores = []
        for rows, bias in spans:
            s = _dot_nt(q, k_ref[rows, dims])
            scores.append(s if bias is None else s + bias)
        m = functools.reduce(jnp.maximum, [jnp.max(s, axis=-1, keepdims=True) for s in scores])
        probs = [jnp.exp(s - m) for s in scores]
        norm = sum(jnp.sum(p, axis=-1, keepdims=True) for p in probs)
        out = sum(jnp.dot(p.astype(BF16), v_ref[rows, :], preferred_element_type=F32)
                  for p, (rows, _) in zip(probs, spans))
        return out / norm

    for case in range(k_ref.shape[0] // blk):
        @pl.when(qi == case)
        def _(case=case):
            o = softmax_pv(0, case) - lam * softmax_pv(1, case)
            y = o * lax.rsqrt(jnp.mean(o * o, axis=-1, keepdims=True) + EPS)
            o_ref[...] = ((y * g_ref[...]) * (1.0 - lam_init)).astype(o_ref.dtype)


def _diff_attention(qkv, bias, lam_p, subln_g, lam_init, batch, seq):
    blk = DA_BLK
    nq = seq // blk
    blocks = (2 * _nbytes((blk, DA_V_DIM), BF16) + 2 * _nbytes((seq, DA_V_DIM), BF16)
              + _nbytes((2, blk, blk), F32))
    scratch = 6 * _nbytes((blk, seq), F32)
    k_col0 = DA_QK // DA_V_DIM
    v_col0 = 2 * DA_QK // DA_V_DIM
    return pl.pallas_call(
        functools.partial(_da_kernel, lam_init=lam_init),
        out_shape=jax.ShapeDtypeStruct((batch * seq, DA_V), BF16),
        grid=(batch, DA_HEADS, nq),
        in_specs=[pl.BlockSpec((blk, DA_V_DIM), lambda b, h, i: (b * nq + i, h)),
                  pl.BlockSpec((seq, DA_V_DIM), lambda b, h, i: (b, k_col0 + h)),
                  pl.BlockSpec((seq, DA_V_DIM), lambda b, h, i: (b, v_col0 + h)),
                  pl.BlockSpec((1, 2, blk, blk), lambda b, h, i: (h, 0, 0, 0)),
                  pl.BlockSpec((4, HEAD_DIM), lambda b, h, i: (0, 0)),
                  pl.BlockSpec((1, DA_V_DIM), lambda b, h, i: (0, 0))],
        out_specs=pl.BlockSpec((blk, DA_V_DIM), lambda b, h, i: (b * nq + i, h)),
        compiler_params=_params(("parallel", "parallel", "parallel"), blocks, scratch),
        name="diff_attention",
    )(qkv, qkv, qkv, bias, lam_p, subln_g.reshape(1, DA_V_DIM))


def _swa_kernel(q_ref, kc_ref, kp_ref, vc_ref, vp_ref, bias_ref, sink_ref, o_ref):
    n = pl.program_id(1)
    is_prev = lax.broadcasted_iota(jnp.int32, (1, 2 * WINDOW), 1) < WINDOW
    no_prev = jnp.where(is_prev & (n == 0), NEG, 0.0).astype(F32)
    for hk in range(SW_KV_HEADS):
        cols = slice(hk * HEAD_DIM, (hk + 1) * HEAD_DIM)
        heads = [slice((hk * SW_GROUP + g) * HEAD_DIM, (hk * SW_GROUP + g + 1) * HEAD_DIM) for g in range(SW_GROUP)]
        q = jnp.concatenate([q_ref[:, hd] for hd in heads], axis=0)
        k = jnp.concatenate([kp_ref[:, cols], kc_ref[:, cols]], axis=0)
        v = jnp.concatenate([vp_ref[:, cols], vc_ref[:, cols]], axis=0)
        s = _dot_nt(q, k) + bias_ref[hk] + no_prev
        sink = sink_ref[hk]
        m = jnp.maximum(jnp.max(s, axis=-1, keepdims=True), sink)
        e = jnp.exp(s - m)
        den = jnp.sum(e, axis=-1, keepdims=True) + jnp.exp(sink - m)
        o = jnp.dot(e.astype(BF16), v, preferred_element_type=F32) / den
        for g, hd in enumerate(heads):
            o_ref[:, hd] = o[g * WINDOW:(g + 1) * WINDOW, :].astype(o_ref.dtype)


def _sliding_attention(qkv, bias_tiles, sinks, batch, seq):
    nb = seq // WINDOW
    q_col = 3 * DA_QK // SW_Q
    k_col = (3 * DA_QK + SW_Q) // SW_KV
    v_col = k_col + 1
    cur = lambda b, n: b * nb + n
    prev = lambda b, n: b * nb + jnp.maximum(n - 1, 0)
    rows = SW_GROUP * WINDOW
    bias = bias_tiles.reshape(SW_KV_HEADS, SW_GROUP, 2, WINDOW, WINDOW)[:, :, ::-1]
    bias = bias.transpose(0, 1, 3, 2, 4).reshape(SW_KV_HEADS, rows, 2 * WINDOW)
    sink_cols = jnp.repeat(sinks.astype(F32).reshape(SW_KV_HEADS, SW_GROUP), WINDOW, axis=1).reshape(
        SW_KV_HEADS, rows, 1)
    blocks = (2 * _nbytes((WINDOW, SW_Q), BF16) + 4 * _nbytes((WINDOW, SW_KV), BF16)
              + _nbytes((SW_KV_HEADS, rows, 2 * WINDOW), F32) + _nbytes((SW_KV_HEADS, rows, 128), F32))
    return pl.pallas_call(
        _swa_kernel,
        out_shape=jax.ShapeDtypeStruct((batch * seq, SW_Q), BF16),
        grid=(batch, nb),
        in_specs=[pl.BlockSpec((WINDOW, SW_Q), lambda b, n: (cur(b, n), q_col)),
                  pl.BlockSpec((WINDOW, SW_KV), lambda b, n: (cur(b, n), k_col)),
                  pl.BlockSpec((WINDOW, SW_KV), lambda b, n: (prev(b, n), k_col)),
                  pl.BlockSpec((WINDOW, SW_KV), lambda b, n: (cur(b, n), v_col)),
                  pl.BlockSpec((WINDOW, SW_KV), lambda b, n: (prev(b, n), v_col)),
                  pl.BlockSpec((SW_KV_HEADS, rows, 2 * WINDOW), lambda b, n: (0, 0, 0)),
                  pl.BlockSpec((SW_KV_HEADS, rows, 1), lambda b, n: (0, 0, 0))],
        out_specs=pl.BlockSpec((WINDOW, SW_Q), lambda b, n: (cur(b, n), 0)),
        compiler_params=_params(("parallel", "parallel"), blocks, 16 * _nbytes((rows, 2 * WINDOW), F32)),
        name="sliding_attention",
    )(qkv, qkv, qkv, qkv, qkv, bias, sink_cols)


def _merge_kernel(oa_ref, ob_ref, wa_ref, wb_ref, ga_ref, gb_ref, o_ref):
    a = jnp.dot(oa_ref[...], wa_ref[...], preferred_element_type=F32)
    b = jnp.dot(ob_ref[...], wb_ref[...], preferred_element_type=F32)
    o_ref[...] = (ga_ref[...].astype(F32) * a + gb_ref[...].astype(F32) * b).astype(o_ref.dtype)


def _merge(o_a, o_b, w_a, w_b, gates, bm=1024, bn=1024):
    t, ka = o_a.shape
    kb = o_b.shape[1]
    d = w_a.shape[1]
    nj = d // bn
    blocks = (_nbytes((bm, ka), BF16) + _nbytes((bm, kb), BF16) + _nbytes((ka, bn), BF16)
              + _nbytes((kb, bn), BF16) + 2 * _nbytes((bm, bn), gates.dtype) + _nbytes((bm, bn), BF16))
    return pl.pallas_call(
        _merge_kernel,
        out_shape=jax.ShapeDtypeStruct((t, d), BF16),
        grid=(t // bm, nj),
        in_specs=[pl.BlockSpec((bm, ka), lambda i, j: (i, 0)),
                  pl.BlockSpec((bm, kb), lambda i, j: (i, 0)),
                  pl.BlockSpec((ka, bn), lambda i, j: (0, j)),
                  pl.BlockSpec((kb, bn), lambda i, j: (0, j)),
                  pl.BlockSpec((bm, bn), lambda i, j: (i, j)),
                  pl.BlockSpec((bm, bn), lambda i, j: (i, nj + j))],
        out_specs=pl.BlockSpec((bm, bn), lambda i, j: (i, j)),
        compiler_params=_params(("parallel", "parallel"), blocks, 2 * _nbytes((bm, bn), F32)),
        name="merge",
    )(o_a, o_b, w_a, w_b, gates, gates)


def _sort_pairs(n):
    pairs = []

    def merge(lo, hi, r):
        step = r * 2
        if step < hi - lo:
            merge(lo, hi, step)
            merge(lo + r, hi, step)
            pairs.extend((i, i + r) for i in range(lo + r, hi - r, step))
        else:
            pairs.append((lo, lo + r))

    def sort(lo, hi):
        if hi - lo >= 1:
            mid = lo + (hi - lo) // 2
            sort(lo, mid)
            sort(mid + 1, hi)
            merge(lo, hi, 1)

    sort(0, n - 1)
    return pairs


_SORT16 = _sort_pairs(PEER_TOPK)


def _sort_desc(xs):
    xs = list(xs)
    for i, j in _SORT16:
        xs[i], xs[j] = jnp.maximum(xs[i], xs[j]), jnp.minimum(xs[i], xs[j])
    return xs


def _merge_top(a, b):
    k = PEER_TOPK
    xs = [jnp.maximum(a[i], b[k - 1 - i]) for i in range(k)]
    d = k // 2
    while d >= 1:
        for i in range(k):
            if not i & d:
                xs[i], xs[i + d] = jnp.maximum(xs[i], xs[i + d]), jnp.minimum(xs[i], xs[i + d])
        d //= 2
    return xs


def _top16_over_rows(s):
    groups = [s[a * 8:(a + 1) * 8, :] for a in range(s.shape[0] // 8)]
    xs = _sort_desc(groups)
    for shift in (4, 2, 1):
        xs = _merge_top(xs, [pltpu.roll(x, shift, 0) for x in xs])
    return xs


def _route_kernel(q_ref, k1_ref, k2_ref, cnt_ref, e1_ref, rank_ref, e2_ref):
    q = q_ref[...]
    s1 = _dot_nt(k1_ref[...], q[:, :KEY_DIM])
    s2 = _dot_nt(k2_ref[...], q[:, KEY_DIM:])
    v1 = _top16_over_rows(s1)
    v2 = _top16_over_rows(s2)
    k = PEER_TOPK
    top = [v1[0] + v2[b] for b in range(k)]
    rest = [v1[a] + v2[b] for a in range(1, k) for b in range(k) if (a + 1) * (b + 1) <= k]
    pad = jnp.full(top[0].shape, -jnp.inf, F32)
    rest = rest + [pad] * (-len(rest) % k)
    for g in range(len(rest) // k):
        top = _merge_top(top, _sort_desc(rest[g * k:(g + 1) * k]))
    z = jnp.ones_like(top[0])
    for c in top[1:]:
        z = z + jnp.exp(c - top[0])
    tau = top[k - 1][0:1]
    cnt = jnp.zeros_like(s1)
    rank = jnp.zeros_like(s2)
    for b in range(k):
        best = v2[b][0:1]
        cnt = cnt + jnp.where(s1 + best >= tau, 1.0, 0.0)
        rank = rank + jnp.where(best > s2, 1.0, 0.0)
    cnt_ref[0] = cnt
    rank_ref[0] = _pack_rows(rank.astype(BF16))
    e1_ref[0] = jnp.exp(s1 - v1[0][0:1]) / z[0:1]
    e2_ref[0] = _pack_rows(jnp.exp(s2 - v2[0][0:1]).astype(BF16))


def _route(q, k1, k2, tm=1024):
    t = q.shape[0]
    wide = jax.ShapeDtypeStruct((PEER_HEADS, N_KEYS, t), F32)
    narrow = jax.ShapeDtypeStruct((PEER_HEADS, N_KEYS // 2, t), jnp.uint32)
    spec = pl.BlockSpec((1, N_KEYS, tm), lambda i, h: (h, 0, i))
    narrow_spec = pl.BlockSpec((1, N_KEYS // 2, tm), lambda i, h: (h, 0, i))
    blocks = _nbytes((tm, 2 * KEY_DIM), BF16) + 3 * _nbytes((N_KEYS, tm), F32)
    return pl.pallas_call(
        _route_kernel,
        out_shape=(wide, wide, narrow, narrow),
        grid=(t // tm, PEER_HEADS),
        in_specs=[pl.BlockSpec((tm, 2 * KEY_DIM), lambda i, h: (i, h)),
                  pl.BlockSpec((N_KEYS, KEY_DIM), lambda i, h: (0, 0)),
                  pl.BlockSpec((N_KEYS, KEY_DIM), lambda i, h: (0, 0))],
        out_specs=(spec, spec, narrow_spec, narrow_spec),
        compiler_params=_params(("parallel", "parallel"), blocks, 24 * _nbytes((N_KEYS, tm), F32)),
        name="peer_route",
    )(q, k1, k2)


def _peer_up_kernel(u_ref, h_ref, inv_ref, cnt_ref, e1_ref, rank_ref, e2_ref, o_ref, act_a, act_b):
    s = pl.program_id(0)

    @pl.when(s == 0)
    def _():
        act_a[...] = jnp.zeros(act_a.shape, F32)
        act_b[...] = jnp.zeros(act_b.shape, F32)

    packed = (N_KEYS // BF16_ROWS, BF16_ROWS, 128)
    chunks = u_ref.shape[0]
    blocks_per_chunk = act_a.shape[0] // N_KEYS // chunks

    def epilogue_block(act_old, r):
        rows = pl.ds(pl.multiple_of(r * N_KEYS, N_KEYS), N_KEYS)
        out_rows = pl.ds(pl.multiple_of(r * (N_KEYS // 2), N_KEYS // 2), N_KEYS // 2)
        cnt_rows = [cnt_ref[h, pl.ds(r, 1), :] for h in range(PEER_HEADS)]
        e1_rows = [e1_ref[h, pl.ds(r, 1), :] for h in range(PEER_HEADS)]
        for c in range(act_old.shape[1] // 128):
            cols = slice(c * 128, (c + 1) * 128)
            act = act_old[rows, cols]
            act_old[rows, cols] = jnp.zeros_like(act)
            act = act * inv_ref[:, cols]
            gate = None
            for h in range(PEER_HEADS):
                cnt = jnp.tile(jnp.broadcast_to(cnt_rows[h][:, cols], packed[1:]).astype(BF16), (packed[0], 1))
                e1 = jnp.tile(jnp.broadcast_to(e1_rows[h][:, cols], packed[1:]).astype(BF16), (packed[0], 1))
                routed = _unpack_rows(rank_ref[h, :, cols]) < cnt
                term = jnp.where(routed, _unpack_rows(e2_ref[h, :, cols]) * e1, jnp.zeros((), BF16))
                gate = term if gate is None else gate + term
            o_ref[out_rows, cols] = _pack_rows(jax.nn.gelu(act).astype(BF16) * gate)

    def step(act_new, act_old):
        def chunk(k, carry):
            act_new[...] += _dot_nt(u_ref[k], h_ref[k])
            for b in range(blocks_per_chunk):
                epilogue_block(act_old, k * blocks_per_chunk + b)
            return carry

        lax.fori_loop(0, chunks, chunk, 0, unroll=PEER_UP_UNROLL)

    @pl.when(s % 2 == 0)
    def _():
        step(act_a, act_b)

    @pl.when(s % 2 == 1)
    def _():
        step(act_b, act_a)


def _peer_up(u, hn, inv_scale, cnt, e1, rank, e2, te=PEER_UP_TE, tm=512):
    chunks, n_exp, dc = u.shape
    t = hn.shape[1]
    rows = te // N_KEYS
    assert rows % chunks == 0
    nj = n_exp // te
    steps = (t // tm) * nj

    def tile_of(step):
        return step // nj, step % nj

    def now(s):
        return tile_of(jnp.minimum(s, steps - 1))

    def lag(s):
        return tile_of(jnp.maximum(s - 1, 0))

    row_spec = pl.BlockSpec((PEER_HEADS, rows, tm), lambda s: (0, lag(s)[1], lag(s)[0]))
    full_spec = pl.BlockSpec((PEER_HEADS, N_KEYS // 2, tm), lambda s: (0, 0, lag(s)[0]))
    blocks = (_nbytes((chunks, te, dc), FP8) + _nbytes((chunks, tm, dc), FP8)
              + 2 * _nbytes((PEER_HEADS, N_KEYS, tm), BF16)
              + 2 * _nbytes((PEER_HEADS, rows, tm), F32) + _nbytes((te, tm), BF16))
    return pl.pallas_call(
        _peer_up_kernel,
        out_shape=jax.ShapeDtypeStruct((n_exp // 2, t), jnp.uint32),
        grid=(steps + 1,),
        in_specs=[pl.BlockSpec((chunks, te, dc), lambda s: (0, now(s)[1], 0)),
                  pl.BlockSpec((chunks, tm, dc), lambda s: (0, now(s)[0], 0)),
                  pl.BlockSpec((1, tm), lambda s: (0, lag(s)[0])),
                  row_spec, row_spec, full_spec, full_spec],
        out_specs=pl.BlockSpec((te // 2, tm), lambda s: (lag(s)[1], lag(s)[0])),
        scratch_shapes=[pltpu.VMEM((te, tm), F32), pltpu.VMEM((te, tm), F32)],
        compiler_params=_params(("arbitrary",), blocks, 3 * _nbytes((te, tm), F32)),
        name="peer_up",
    )(u, hn, inv_scale, cnt, e1, rank, e2)


def _peer_down_kernel(inv_ref, vt_ref, w_ref, x_ref, o_ref, acc_ref):
    kk = pl.program_id(2)

    @pl.when(kk == 0)
    def _():
        acc_ref[...] = jnp.zeros(acc_ref.shape, F32)

    w = _unpack_rows(w_ref[...])
    amax = jnp.max(jnp.abs(w), axis=0, keepdims=True).astype(F32)
    shift = jnp.floor(jnp.log2(FP8_TARGET / jnp.maximum(amax, FP8_TINY)))
    w8 = (w * jnp.exp2(shift).astype(BF16)).astype(FP8)
    part = jnp.dot(vt_ref[...], w8, preferred_element_type=F32)
    acc_ref[...] += part * (jnp.exp2(-shift) * inv_ref[0, 0])

    @pl.when(kk == pl.num_programs(2) - 1)
    def _():
        o_ref[...] = x_ref[...] + acc_ref[...].T


def _peer_down(vt, inv_scale, wt, x, bd=1024, bt=1024, tk=2048):
    d, n_exp = vt.shape
    t = wt.shape[1]
    blocks = (_nbytes((bd, tk), FP8) + _nbytes((tk, bt), BF16) + 2 * _nbytes((bt, bd), F32))
    return pl.pallas_call(
        _peer_down_kernel,
        out_shape=jax.ShapeDtypeStruct((t, d), F32),
        grid=(d // bd, t // bt, n_exp // tk),
        in_specs=[pl.BlockSpec(memory_space=pltpu.SMEM),
                  pl.BlockSpec((bd, tk), lambda i, j, k: (i, k)),
                  pl.BlockSpec((tk // 2, bt), lambda i, j, k: (k, j)),
                  pl.BlockSpec((bt, bd), lambda i, j, k: (j, i))],
        out_specs=pl.BlockSpec((bt, bd), lambda i, j, k: (j, i)),
        scratch_shapes=[pltpu.VMEM((bd, bt), F32)],
        compiler_params=_params(("parallel", "parallel", "arbitrary"), blocks,
                                3 * _nbytes((bd, bt), F32) + 2 * _nbytes((tk, bt), BF16)),
        name="peer_down",
    )(inv_scale, vt, wt, x)


def _ple_kernel(h_ref, wg_ref, p_ref, wp_ref, x_ref, o_ref):
    gate = jax.nn.sigmoid(jnp.dot(h_ref[...], wg_ref[...], preferred_element_type=F32))
    emb = jnp.dot(p_ref[...], wp_ref[...], preferred_element_type=F32)
    o_ref[...] = x_ref[...] + gate * emb


def _ple(hp, w_gate, p, w_proj, x, bm=512, bn=1024):
    t, d = hp.shape
    pd = p.shape[1]
    n = w_gate.shape[1]
    blocks = (_nbytes((bm, d), BF16) + _nbytes((d, bn), BF16) + _nbytes((bm, pd), BF16)
              + _nbytes((pd, bn), BF16) + 2 * _nbytes((bm, bn), F32))
    return pl.pallas_call(
        _ple_kernel,
        out_shape=jax.ShapeDtypeStruct((t, n), F32),
        grid=(t // bm, n // bn),
        in_specs=[pl.BlockSpec((bm, d), lambda i, j: (i, 0)),
                  pl.BlockSpec((d, bn), lambda i, j: (0, j)),
                  pl.BlockSpec((bm, pd), lambda i, j: (i, 0)),
                  pl.BlockSpec((pd, bn), lambda i, j: (0, j)),
                  pl.BlockSpec((bm, bn), lambda i, j: (i, j))],
        out_specs=pl.BlockSpec((bm, bn), lambda i, j: (i, j)),
        compiler_params=_params(("parallel", "parallel"), blocks, 2 * _nbytes((bm, bn), F32)),
        name="ple",
    )(hp, w_gate, p, w_proj, x)


def _qkv_column_scale():
    s = HEAD_DIM ** -0.5
    parts = [(DA_QK, s), (DA_QK, 1.0), (DA_V, 1.0), (SW_Q, s), (SW_KV, 1.0), (SW_KV, 1.0)]
    return jnp.concatenate([jnp.full((1, w), v, F32) for w, v in parts], axis=1)


@jax.jit
def kernel(x, p, positions, rel_bias, norm_mix, w_in, da_lambda, da_subln, sw_sinks, w_br_a, w_br_b, w_out,
           norm_ffn, peer_wq, peer_k1, peer_k2, peer_u, peer_v, norm_ple, ple_gate, ple_proj, norm_final):
    del positions
    batch, seq, d = x.shape
    t = batch * seq
    depth = w_in.shape[0]
    xf = x.reshape(t, d)
    da_bias = _bias_tiles(rel_bias[:, :DA_HEADS], DA_BLK, None, True)
    sw_bias = _bias_tiles(rel_bias[:, DA_HEADS:], WINDOW, WINDOW, False)
    col_scale = _qkv_column_scale()
    tile = pl.BlockSpec((1, 1024), lambda i, j: (0, j))
    for i in range(depth):
        lam_init = 0.8 - 0.6 * math.exp(-0.3 * i)
        h = _rmsnorm(xf, norm_mix[i], BF16)
        qkv = _matmul(_mm_scale_kernel, h, _narrow(w_in, i, 0, QKV_WIDTH, 1024), [col_scale], [tile],
                      BF16, 1024, 1024, "proj_qkv")
        gates = _matmul(_mm_sigmoid_kernel, h, _narrow(w_in, i, QKV_WIDTH, w_in.shape[2] - QKV_WIDTH, 1024),
                        [], [], BF16, 1024, 1024, "proj_gates")
        o_a = _diff_attention(qkv, da_bias, da_lambda[i], da_subln[i], lam_init, batch, seq)
        o_b = _sliding_attention(qkv, sw_bias, sw_sinks[i], batch, seq)
        merged = _merge(o_a, o_b, _narrow(w_br_a, i), _narrow(w_br_b, i), gates)
        xf = _matmul(_mm_residual_kernel, merged, _narrow(w_out, i), [xf],
                     [pl.BlockSpec((512, 1024), lambda i, j: (i, j))], F32, 512, 1024, "proj_out")
        hn, hn8, hn_inv = _rmsnorm_fp8(xf, norm_ffn[i], PEER_UP_CHUNKS)
        q = _matmul(_mm_plain_kernel, hn, _narrow(peer_wq, i), [], [], BF16, 1024, 1024, "peer_query")
        cnt, e1, rank, e2 = _route(q, peer_k1[i].astype(BF16), peer_k2[i].astype(BF16))
        su, sv = _fp8_scale(peer_u, i), _fp8_scale(peer_v, i)
        u8 = _narrow(peer_u, i, layout="chunked", chunks=PEER_UP_CHUNKS, dtype=FP8, scale=su)
        wt = _peer_up(u8, hn8, hn_inv.reshape(1, t) / su, cnt, e1, rank, e2)
        xf = _peer_down(_narrow(peer_v, i, layout="transposed", dtype=FP8, scale=sv), 1.0 / sv, wt, xf)
        hp = _rmsnorm(xf, norm_ple[i], BF16)
        xf = _ple(hp, _narrow(ple_gate, i), p[i].reshape(t, -1).astype(BF16), ple_proj[i].astype(BF16), xf)
    return _rmsnorm(xf, norm_final, F32).reshape(batch, seq, d)
```

```python
import functools
import math

import jax
import jax.numpy as jnp
from jax import lax
from jax.experimental import pallas as pl
from jax.experimental.pallas import tpu as pltpu

F32 = jnp.float32
BF16 = jnp.bfloat16
FP8 = jnp.float8_e4m3fn
FP8_TARGET = 240.0
FP8_TINY = 1e-30

HEAD_DIM = 128
DA_HEADS = 8
DA_V_DIM = 2 * HEAD_DIM
SW_Q_HEADS = 16
SW_KV_HEADS = 4
SW_GROUP = SW_Q_HEADS // SW_KV_HEADS
WINDOW = 128
N_BUCKETS = 32
MAX_EXACT = N_BUCKETS // 2
MAX_DIST = 128
NEG = -1e30
DA_QK = DA_HEADS * 2 * HEAD_DIM
DA_V = DA_HEADS * DA_V_DIM
SW_Q = SW_Q_HEADS * HEAD_DIM
SW_KV = SW_KV_HEADS * HEAD_DIM
QKV_WIDTH = 3 * DA_QK + SW_Q + 2 * SW_KV
PEER_HEADS = 8
N_KEYS = 128
PEER_TOPK = 16
KEY_DIM = 128
EPS = 1e-6

V7X_VMEM_REQUEST_CAP = 60 * 1024 * 1024
BF16_ROWS = 16
DA_BLK = 512
PEER_UP_TE = 1024
FP8_ROW_BLOCK = 512
PEER_UP_CHUNKS = 4
PEER_UP_UNROLL = 2


def _nbytes(shape, dtype):
    return math.prod(shape) * jnp.dtype(dtype).itemsize


def _params(semantics, block_bytes, scratch_bytes=0, flags=None):
    need = int(1.25 * (2 * block_bytes + scratch_bytes)) + (4 << 20)
    return pltpu.CompilerParams(dimension_semantics=semantics,
                                vmem_limit_bytes=min(need, V7X_VMEM_REQUEST_CAP), flags=flags)


def _pack_rows(x):
    return pltpu.bitcast(x, jnp.uint32)


def _unpack_rows(x):
    return pltpu.bitcast(x, BF16)


def _dot_nt(a, b):
    return lax.dot_general(a, b, (((1,), (1,)), ((), ())), preferred_element_type=F32)


def _rmsnorm_kernel(x_ref, g_ref, o_ref):
    x = x_ref[...]
    y = x * lax.rsqrt(jnp.mean(x * x, axis=-1, keepdims=True) + EPS)
    o_ref[...] = (y * g_ref[...]).astype(o_ref.dtype)


def _rmsnorm(x, g, out_dtype, rows=256):
    t, d = x.shape
    blocks = _nbytes((rows, d), F32) + _nbytes((rows, d), out_dtype)
    return pl.pallas_call(
        _rmsnorm_kernel,
        out_shape=jax.ShapeDtypeStruct((t, d), out_dtype),
        grid=(t // rows,),
        in_specs=[pl.BlockSpec((rows, d), lambda i: (i, 0)),
                  pl.BlockSpec((1, d), lambda i: (0, 0))],
        out_specs=pl.BlockSpec((rows, d), lambda i: (i, 0)),
        compiler_params=_params(("parallel",), blocks, _nbytes((rows, d), F32)),
        name="rmsnorm",
    )(x, g.reshape(1, d))


def _rmsnorm_fp8_kernel(x_ref, g_ref, o_ref, oc_ref, inv_ref):
    x = x_ref[...]
    y = x * lax.rsqrt(jnp.mean(x * x, axis=-1, keepdims=True) + EPS) * g_ref[...]
    o_ref[...] = y.astype(o_ref.dtype)
    amax = jnp.maximum(jnp.max(jnp.abs(y), axis=-1, keepdims=True), FP8_TINY)
    inv_ref[...] = amax * (1.0 / FP8_TARGET)
    y8 = y * (FP8_TARGET / amax)
    dc = oc_ref.shape[2]
    for k in range(oc_ref.shape[0]):
        oc_ref[k] = y8[:, k * dc:(k + 1) * dc].astype(oc_ref.dtype)


def _rmsnorm_fp8(x, g, chunks, rows=256):
    t, d = x.shape
    dc = d // chunks
    blocks = _nbytes((rows, d), F32) + _nbytes((rows, d), BF16) + _nbytes((rows, d), FP8) + _nbytes((rows, 128), F32)
    return pl.pallas_call(
        _rmsnorm_fp8_kernel,
        out_shape=(jax.ShapeDtypeStruct((t, d), BF16), jax.ShapeDtypeStruct((chunks, t, dc), FP8),
                   jax.ShapeDtypeStruct((t, 1), F32)),
        grid=(t // rows,),
        in_specs=[pl.BlockSpec((rows, d), lambda i: (i, 0)),
                  pl.BlockSpec((1, d), lambda i: (0, 0))],
        out_specs=(pl.BlockSpec((rows, d), lambda i: (i, 0)),
                   pl.BlockSpec((chunks, rows, dc), lambda i: (0, i, 0)),
                   pl.BlockSpec((rows, 1), lambda i: (i, 0))),
        compiler_params=_params(("parallel",), blocks, 2 * _nbytes((rows, d), F32)),
        name="rmsnorm_fp8",
    )(x, g.reshape(1, d))


CAST_BLOCK_BYTES = 8 << 20


def _cast_kernel(x_ref, o_ref):
    o_ref[...] = x_ref[...].astype(o_ref.dtype)


def _narrow(w, layer, col0=0, ncols=None, bc=None):
    _, r, c = w.shape
    ncols = c if ncols is None else ncols
    bc = ncols if bc is None else bc
    br = min(r, CAST_BLOCK_BYTES // (bc * 4))
    return pl.pallas_call(
        _cast_kernel,
        out_shape=jax.ShapeDtypeStruct((r, ncols), BF16),
        grid=(r // br, ncols // bc),
        in_specs=[pl.BlockSpec((None, br, bc), lambda i, j: (layer, i, col0 // bc + j))],
        out_specs=pl.BlockSpec((br, bc), lambda i, j: (i, j)),
        compiler_params=_params(("parallel", "parallel"), _nbytes((br, bc), F32) + _nbytes((br, bc), BF16),
                                _nbytes((br, bc), F32)),
        name="narrow",
    )(w)


def _pow2_scale(amax):
    shift = jnp.floor(jnp.log2(FP8_TARGET / jnp.maximum(amax, FP8_TINY)))
    return jnp.exp2(shift), jnp.exp2(-shift)


def _fp8_rows_kernel(x_ref, o_ref, inv_ref, *, layout):
    x = x_ref[...]
    amax = jnp.max(jnp.max(jnp.abs(x), axis=0, keepdims=True), axis=1, keepdims=True)
    scale, inv = _pow2_scale(amax)
    inv_ref[...] = jnp.broadcast_to(inv, inv_ref.shape)
    y = x * scale
    if layout == "chunked":
        dc = o_ref.shape[2]
        for k in range(o_ref.shape[0]):
            o_ref[k] = y[:, k * dc:(k + 1) * dc].astype(o_ref.dtype)
    else:
        o_ref[...] = y.T.astype(o_ref.dtype)


def _fp8_rows(w, layer, layout, chunks=None, br=FP8_ROW_BLOCK):
    _, r, c = w.shape
    if layout == "chunked":
        shape = (chunks, r, c // chunks)
        out_spec = pl.BlockSpec((chunks, br, c // chunks), lambda i: (0, i, 0))
    else:
        shape = (c, r)
        out_spec = pl.BlockSpec((c, br), lambda i: (0, i))
    out, inv = pl.pallas_call(
        functools.partial(_fp8_rows_kernel, layout=layout),
        out_shape=(jax.ShapeDtypeStruct(shape, FP8), jax.ShapeDtypeStruct((r // br, 1, 128), F32)),
        grid=(r // br,),
        in_specs=[pl.BlockSpec((None, br, c), lambda i: (layer, i, 0))],
        out_specs=(out_spec, pl.BlockSpec((1, 1, 128), lambda i: (i, 0, 0))),
        compiler_params=_params(("parallel",), _nbytes((br, c), F32) + _nbytes((br, c), FP8), 2 * _nbytes((br, c), F32)),
        name="fp8_rows_" + layout,
    )(w)
    return out, inv[:, 0, 0]


def _mm_scale_kernel(a_ref, b_ref, s_ref, o_ref):
    acc = jnp.dot(a_ref[...], b_ref[...], preferred_element_type=F32)
    o_ref[...] = (acc * s_ref[...]).astype(o_ref.dtype)


def _mm_sigmoid_kernel(a_ref, b_ref, o_ref):
    acc = jnp.dot(a_ref[...], b_ref[...], preferred_element_type=F32)
    o_ref[...] = jax.nn.sigmoid(acc).astype(o_ref.dtype)


def _mm_plain_kernel(a_ref, b_ref, o_ref):
    o_ref[...] = jnp.dot(a_ref[...], b_ref[...], preferred_element_type=F32).astype(o_ref.dtype)


def _mm_residual_kernel(a_ref, b_ref, x_ref, o_ref):
    o_ref[...] = x_ref[...] + jnp.dot(a_ref[...], b_ref[...], preferred_element_type=F32)


def _matmul(body, a, b, extra, extra_specs, out_dtype, bm, bn, name):
    m, k = a.shape
    n = b.shape[1]
    blocks = (_nbytes((bm, k), a.dtype) + _nbytes((k, bn), b.dtype) + _nbytes((bm, bn), out_dtype)
              + sum(_nbytes(s.block_shape, e.dtype) for s, e in zip(extra_specs, extra)))
    return pl.pallas_call(
        body,
        out_shape=jax.ShapeDtypeStruct((m, n), out_dtype),
        grid=(m // bm, n // bn),
        in_specs=[pl.BlockSpec((bm, k), lambda i, j: (i, 0)),
                  pl.BlockSpec((k, bn), lambda i, j: (0, j))] + list(extra_specs),
        out_specs=pl.BlockSpec((bm, bn), lambda i, j: (i, j)),
        compiler_params=_params(("parallel", "parallel"), blocks, _nbytes((bm, bn), F32)),
        name=name,
    )(a, b, *extra)


def _bias_kernel(tab_ref, o_ref, *, blk, window, rebase):
    h = pl.program_id(0)
    r = lax.broadcasted_iota(jnp.int32, (blk, blk), 0)
    c = lax.broadcasted_iota(jnp.int32, (blk, blk), 1)
    base = tab_ref[N_BUCKETS - 1, h] if rebase else 0.0
    for delta in (0, 1):
        rel = r - c + delta * blk
        n = jnp.maximum(rel, 0)
        nf = jnp.maximum(n, 1).astype(F32)
        large = MAX_EXACT + (jnp.log(nf / MAX_EXACT) / math.log(MAX_DIST / MAX_EXACT)
                             * (N_BUCKETS - MAX_EXACT)).astype(jnp.int32)
        large = jnp.minimum(large, N_BUCKETS - 1)
        bucket = jnp.where(n < MAX_EXACT, n, large)
        bias = jnp.zeros((blk, blk), F32)
        for b in range(N_BUCKETS):
            bias = jnp.where(bucket == b, tab_ref[b, h] - base, bias)
        mask = rel >= 0
        if window is not None:
            mask = mask & (rel < window)
        o_ref[0, delta] = jnp.where(mask, bias, NEG)


def _bias_tiles(tab, blk, window, rebase):
    heads = tab.shape[1]
    return pl.pallas_call(
        functools.partial(_bias_kernel, blk=blk, window=window, rebase=rebase),
        out_shape=jax.ShapeDtypeStruct((heads, 2, blk, blk), F32),
        grid=(heads,),
        in_specs=[pl.BlockSpec(memory_space=pltpu.SMEM)],
        out_specs=pl.BlockSpec((1, 2, blk, blk), lambda h: (h, 0, 0, 0)),
        compiler_params=_params(("parallel",), _nbytes((2, blk, blk), F32), 4 * _nbytes((blk, blk), F32)),
        name="bias_tiles",
    )(tab)


def _da_kernel(q_ref, k_ref, v_ref, bias_ref, lam_ref, g_ref, o_ref, *, lam_init):
    blk = q_ref.shape[0]
    qi = pl.program_id(2)
    lp = lam_ref[...]
    lam = (jnp.exp(jnp.sum(lp[0:1] * lp[1:2], axis=-1, keepdims=True))
           - jnp.exp(jnp.sum(lp[2:3] * lp[3:4], axis=-1, keepdims=True)) + lam_init)

    def softmax_pv(j, case):
        dims = slice(j * HEAD_DIM, (j + 1) * HEAD_DIM)
        q = q_ref[:, dims]
        spans = [(slice(case * blk, (case + 1) * blk), bias_ref[0, 0])]
        if case >= 1:
            spans.append((slice((case - 1) * blk, case * blk), bias_ref[0, 1]))
        if case >= 2:
            spans.append((slice(0, (case - 1) * blk), None))
        scores = []
        for rows, bias in spans:
            s = _dot_nt(q, k_ref[rows, dims])
            scores.append(s if bias is None else s + bias)
        m = functools.reduce(jnp.maximum, [jnp.max(s, axis=-1, keepdims=True) for s in scores])
        probs = [jnp.exp(s - m) for s in scores]
        norm = sum(jnp.sum(p, axis=-1, keepdims=True) for p in probs)
        out = sum(jnp.dot(p.astype(BF16), v_ref[rows, :], preferred_element_type=F32)
                  for p, (rows, _) in zip(probs, spans))
        return out / norm

    for case in range(k_ref.shape[0] // blk):
        @pl.when(qi == case)
        def _(case=case):
            o = softmax_pv(0, case) - lam * softmax_pv(1, case)
            y = o * lax.rsqrt(jnp.mean(o * o, axis=-1, keepdims=True) + EPS)
            o_ref[...] = ((y * g_ref[...]) * (1.0 - lam_init)).astype(o_ref.dtype)


def _diff_attention(qkv, bias, lam_p, subln_g, lam_init, batch, seq):
    blk = DA_BLK
    nq = seq // blk
    blocks = (2 * _nbytes((blk, DA_V_DIM), BF16) + 2 * _nbytes((seq, DA_V_DIM), BF16)
              + _nbytes((2, blk, blk), F32))
    scratch = 6 * _nbytes((blk, seq), F32)
    k_col0 = DA_QK // DA_V_DIM
    v_col0 = 2 * DA_QK // DA_V_DIM
    return pl.pallas_call(
        functools.partial(_da_kernel, lam_init=lam_init),
        out_shape=jax.ShapeDtypeStruct((batch * seq, DA_V), BF16),
        grid=(batch, DA_HEADS, nq),
        in_specs=[pl.BlockSpec((blk, DA_V_DIM), lambda b, h, i: (b * nq + i, h)),
                  pl.BlockSpec((seq, DA_V_DIM), lambda b, h, i: (b, k_col0 + h)),
                  pl.BlockSpec((seq, DA_V_DIM), lambda b, h, i: (b, v_col0 + h)),
                  pl.BlockSpec((1, 2, blk, blk), lambda b, h, i: (h, 0, 0, 0)),
                  pl.BlockSpec((4, HEAD_DIM), lambda b, h, i: (0, 0)),
                  pl.BlockSpec((1, DA_V_DIM), lambda b, h, i: (0, 0))],
        out_specs=pl.BlockSpec((blk, DA_V_DIM), lambda b, h, i: (b * nq + i, h)),
        compiler_params=_params(("parallel", "parallel", "parallel"), blocks, scratch),
        name="diff_attention",
    )(qkv, qkv, qkv, bias, lam_p, subln_g.reshape(1, DA_V_DIM))


def _swa_kernel(q_ref, kc_ref, kp_ref, vc_ref, vp_ref, bias_ref, sink_ref, o_ref):
    n = pl.program_id(1)
    is_prev = lax.broadcasted_iota(jnp.int32, (1, 2 * WINDOW), 1) < WINDOW
    no_prev = jnp.where(is_prev & (n == 0), NEG, 0.0).astype(F32)
    for hk in range(SW_KV_HEADS):
        cols = slice(hk * HEAD_DIM, (hk + 1) * HEAD_DIM)
        heads = [slice((hk * SW_GROUP + g) * HEAD_DIM, (hk * SW_GROUP + g + 1) * HEAD_DIM) for g in range(SW_GROUP)]
        q = jnp.concatenate([q_ref[:, hd] for hd in heads], axis=0)
        k = jnp.concatenate([kp_ref[:, cols], kc_ref[:, cols]], axis=0)
        v = jnp.concatenate([vp_ref[:, cols], vc_ref[:, cols]], axis=0)
        s = _dot_nt(q, k) + bias_ref[hk] + no_prev
        sink = sink_ref[hk]
        m = jnp.maximum(jnp.max(s, axis=-1, keepdims=True), sink)
        e = jnp.exp(s - m)
        den = jnp.sum(e, axis=-1, keepdims=True) + jnp.exp(sink - m)
        o = jnp.dot(e.astype(BF16), v, preferred_element_type=F32) / den
        for g, hd in enumerate(heads):
            o_ref[:, hd] = o[g * WINDOW:(g + 1) * WINDOW, :].astype(o_ref.dtype)


def _sliding_attention(qkv, bias_tiles, sinks, batch, seq):
    nb = seq // WINDOW
    q_col = 3 * DA_QK // SW_Q
    k_col = (3 * DA_QK + SW_Q) // SW_KV
    v_col = k_col + 1
    cur = lambda b, n: b * nb + n
    prev = lambda b, n: b * nb + jnp.maximum(n - 1, 0)
    rows = SW_GROUP * WINDOW
    bias = bias_tiles.reshape(SW_KV_HEADS, SW_GROUP, 2, WINDOW, WINDOW)[:, :, ::-1]
    bias = bias.transpose(0, 1, 3, 2, 4).reshape(SW_KV_HEADS, rows, 2 * WINDOW)
    sink_cols = jnp.repeat(sinks.astype(F32).reshape(SW_KV_HEADS, SW_GROUP), WINDOW, axis=1).reshape(
        SW_KV_HEADS, rows, 1)
    blocks = (2 * _nbytes((WINDOW, SW_Q), BF16) + 4 * _nbytes((WINDOW, SW_KV), BF16)
              + _nbytes((SW_KV_HEADS, rows, 2 * WINDOW), F32) + _nbytes((SW_KV_HEADS, rows, 128), F32))
    return pl.pallas_call(
        _swa_kernel,
        out_shape=jax.ShapeDtypeStruct((batch * seq, SW_Q), BF16),
        grid=(batch, nb),
        in_specs=[pl.BlockSpec((WINDOW, SW_Q), lambda b, n: (cur(b, n), q_col)),
                  pl.BlockSpec((WINDOW, SW_KV), lambda b, n: (cur(b, n), k_col)),
                  pl.BlockSpec((WINDOW, SW_KV), lambda b, n: (prev(b, n), k_col)),
                  pl.BlockSpec((WINDOW, SW_KV), lambda b, n: (cur(b, n), v_col)),
                  pl.BlockSpec((WINDOW, SW_KV), lambda b, n: (prev(b, n), v_col)),
                  pl.BlockSpec((SW_KV_HEADS, rows, 2 * WINDOW), lambda b, n: (0, 0, 0)),
                  pl.BlockSpec((SW_KV_HEADS, rows, 1), lambda b, n: (0, 0, 0))],
        out_specs=pl.BlockSpec((WINDOW, SW_Q), lambda b, n: (cur(b, n), 0)),
        compiler_params=_params(("parallel", "parallel"), blocks, 16 * _nbytes((rows, 2 * WINDOW), F32)),
        name="sliding_attention",
    )(qkv, qkv, qkv, qkv, qkv, bias, sink_cols)


def _merge_kernel(oa_ref, ob_ref, wa_ref, wb_ref, ga_ref, gb_ref, o_ref):
    a = jnp.dot(oa_ref[...], wa_ref[...], preferred_element_type=F32)
    b = jnp.dot(ob_ref[...], wb_ref[...], preferred_element_type=F32)
    o_ref[...] = (ga_ref[...].astype(F32) * a + gb_ref[...].astype(F32) * b).astype(o_ref.dtype)


def _merge(o_a, o_b, w_a, w_b, gates, bm=1024, bn=1024):
    t, ka = o_a.shape
    kb = o_b.shape[1]
    d = w_a.shape[1]
    nj = d // bn
    blocks = (_nbytes((bm, ka), BF16) + _nbytes((bm, kb), BF16) + _nbytes((ka, bn), BF16)
              + _nbytes((kb, bn), BF16) + 2 * _nbytes((bm, bn), gates.dtype) + _nbytes((bm, bn), BF16))
    return pl.pallas_call(
        _merge_kernel,
        out_shape=jax.ShapeDtypeStruct((t, d), BF16),
        grid=(t // bm, nj),
        in_specs=[pl.BlockSpec((bm, ka), lambda i, j: (i, 0)),
                  pl.BlockSpec((bm, kb), lambda i, j: (i, 0)),
                  pl.BlockSpec((ka, bn), lambda i, j: (0, j)),
                  pl.BlockSpec((kb, bn), lambda i, j: (0, j)),
                  pl.BlockSpec((bm, bn), lambda i, j: (i, j)),
                  pl.BlockSpec((bm, bn), lambda i, j: (i, nj + j))],
        out_specs=pl.BlockSpec((bm, bn), lambda i, j: (i, j)),
        compiler_params=_params(("parallel", "parallel"), blocks, 2 * _nbytes((bm, bn), F32)),
        name="merge",
    )(o_a, o_b, w_a, w_b, gates, gates)


def _sort_pairs(n):
    pairs = []

    def merge(lo, hi, r):
        step = r * 2
        if step < hi - lo:
            merge(lo, hi, step)
            merge(lo + r, hi, step)
            pairs.extend((i, i + r) for i in range(lo + r, hi - r, step))
        else:
            pairs.append((lo, lo + r))

    def sort(lo, hi):
        if hi - lo >= 1:
            mid = lo + (hi - lo) // 2
            sort(lo, mid)
            sort(mid + 1, hi)
            merge(lo, hi, 1)

    sort(0, n - 1)
    return pairs


_SORT16 = _sort_pairs(PEER_TOPK)


def _sort_desc(xs):
    xs = list(xs)
    for i, j in _SORT16:
        xs[i], xs[j] = jnp.maximum(xs[i], xs[j]), jnp.minimum(xs[i], xs[j])
    return xs


def _merge_top(a, b):
    k = PEER_TOPK
    xs = [jnp.maximum(a[i], b[k - 1 - i]) for i in range(k)]
    d = k // 2
    while d >= 1:
        for i in range(k):
            if not i & d:
                xs[i], xs[i + d] = jnp.maximum(xs[i], xs[i + d]), jnp.minimum(xs[i], xs[i + d])
        d //= 2
    return xs


def _top16_over_rows(s):
    groups = [s[a * 8:(a + 1) * 8, :] for a in range(s.shape[0] // 8)]
    xs = _sort_desc(groups)
    for shift in (4, 2, 1):
        xs = _merge_top(xs, [pltpu.roll(x, shift, 0) for x in xs])
    return xs


def _route_kernel(q_ref, k1_ref, k2_ref, cnt_ref, e1_ref, rank_ref, e2_ref):
    q = q_ref[...]
    s1 = _dot_nt(k1_ref[...], q[:, :KEY_DIM])
    s2 = _dot_nt(k2_ref[...], q[:, KEY_DIM:])
    v1 = _top16_over_rows(s1)
    v2 = _top16_over_rows(s2)
    k = PEER_TOPK
    top = [v1[0] + v2[b] for b in range(k)]
    rest = [v1[a] + v2[b] for a in range(1, k) for b in range(k) if (a + 1) * (b + 1) <= k]
    pad = jnp.full(top[0].shape, -jnp.inf, F32)
    rest = rest + [pad] * (-len(rest) % k)
    for g in range(len(rest) // k):
        top = _merge_top(top, _sort_desc(rest[g * k:(g + 1) * k]))
    z = jnp.ones_like(top[0])
    for c in top[1:]:
        z = z + jnp.exp(c - top[0])
    tau = top[k - 1][0:1]
    cnt = jnp.zeros_like(s1)
    rank = jnp.zeros_like(s2)
    for b in range(k):
        best = v2[b][0:1]
        cnt = cnt + jnp.where(s1 + best >= tau, 1.0, 0.0)
        rank = rank + jnp.where(best > s2, 1.0, 0.0)
    cnt_ref[0] = cnt
    rank_ref[0] = _pack_rows(rank.astype(BF16))
    e1_ref[0] = jnp.exp(s1 - v1[0][0:1]) / z[0:1]
    e2_ref[0] = _pack_rows(jnp.exp(s2 - v2[0][0:1]).astype(BF16))


def _route(q, k1, k2, tm=1024):
    t = q.shape[0]
    wide = jax.ShapeDtypeStruct((PEER_HEADS, N_KEYS, t), F32)
    narrow = jax.ShapeDtypeStruct((PEER_HEADS, N_KEYS // 2, t), jnp.uint32)
    spec = pl.BlockSpec((1, N_KEYS, tm), lambda i, h: (h, 0, i))
    narrow_spec = pl.BlockSpec((1, N_KEYS // 2, tm), lambda i, h: (h, 0, i))
    blocks = _nbytes((tm, 2 * KEY_DIM), BF16) + 3 * _nbytes((N_KEYS, tm), F32)
    return pl.pallas_call(
        _route_kernel,
        out_shape=(wide, wide, narrow, narrow),
        grid=(t // tm, PEER_HEADS),
        in_specs=[pl.BlockSpec((tm, 2 * KEY_DIM), lambda i, h: (i, h)),
                  pl.BlockSpec((N_KEYS, KEY_DIM), lambda i, h: (0, 0)),
                  pl.BlockSpec((N_KEYS, KEY_DIM), lambda i, h: (0, 0))],
        out_specs=(spec, spec, narrow_spec, narrow_spec),
        compiler_params=_params(("parallel", "parallel"), blocks, 24 * _nbytes((N_KEYS, tm), F32)),
        name="peer_route",
    )(q, k1, k2)


def _peer_up_kernel(inv_u_ref, inv_v_ref, u_ref, h_ref, inv_ref, cnt_ref, e1_ref, rank_ref, e2_ref,
                    o_ref, amax_ref, act_a, act_b, *, expert_tiles):
    s = pl.program_id(0)
    first_scale_block = (jnp.maximum(s - 1, 0) % expert_tiles) * (act_a.shape[0] // FP8_ROW_BLOCK)

    @pl.when(s == 0)
    def _():
        act_a[...] = jnp.zeros(act_a.shape, F32)
        act_b[...] = jnp.zeros(act_b.shape, F32)

    packed = (N_KEYS // BF16_ROWS, BF16_ROWS, 128)
    chunks = u_ref.shape[0]
    blocks_per_chunk = act_a.shape[0] // N_KEYS // chunks

    def epilogue_block(act_old, r):
        rows = pl.ds(pl.multiple_of(r * N_KEYS, N_KEYS), N_KEYS)
        out_rows = pl.ds(pl.multiple_of(r * (N_KEYS // 2), N_KEYS // 2), N_KEYS // 2)
        scale_block = first_scale_block + r // (FP8_ROW_BLOCK // N_KEYS)
        inv_u, inv_v = inv_u_ref[scale_block], inv_v_ref[scale_block]
        cnt_rows = [cnt_ref[h, pl.ds(r, 1), :] for h in range(PEER_HEADS)]
        e1_rows = [e1_ref[h, pl.ds(r, 1), :] * inv_v for h in range(PEER_HEADS)]
        for c in range(act_old.shape[1] // 128):
            cols = slice(c * 128, (c + 1) * 128)
            act = act_old[rows, cols]
            act_old[rows, cols] = jnp.zeros_like(act)
            act = act * (inv_ref[:, cols] * inv_u)
            gate = None
            for h in range(PEER_HEADS):
                cnt = jnp.tile(jnp.broadcast_to(cnt_rows[h][:, cols], packed[1:]).astype(BF16), (packed[0], 1))
                e1 = jnp.tile(jnp.broadcast_to(e1_rows[h][:, cols], packed[1:]).astype(BF16), (packed[0], 1))
                routed = _unpack_rows(rank_ref[h, :, cols]) < cnt
                term = jnp.where(routed, _unpack_rows(e2_ref[h, :, cols]) * e1, jnp.zeros((), BF16))
                gate = term if gate is None else gate + term
            w = jax.nn.gelu(act).astype(BF16) * gate
            o_ref[out_rows, cols] = _pack_rows(w)
            peak = jnp.max(jnp.abs(w).reshape(packed), axis=0).astype(F32)
            amax_ref[0, :, cols] = jnp.where(r == 0, peak, jnp.maximum(amax_ref[0, :, cols], peak))

    def step(act_new, act_old):
        def chunk(k, carry):
            act_new[...] += _dot_nt(u_ref[k], h_ref[k])
            for b in range(blocks_per_chunk):
                epilogue_block(act_old, k * blocks_per_chunk + b)
            return carry

        lax.fori_loop(0, chunks, chunk, 0, unroll=PEER_UP_UNROLL)

    @pl.when(s % 2 == 0)
    def _():
        step(act_a, act_b)

    @pl.when(s % 2 == 1)
    def _():
        step(act_b, act_a)


def _peer_up(u, inv_u, inv_v, hn, inv_h, cnt, e1, rank, e2, te=PEER_UP_TE, tm=512):
    chunks, n_exp, dc = u.shape
    t = hn.shape[1]
    rows = te // N_KEYS
    assert rows % chunks == 0
    nj = n_exp // te
    steps = (t // tm) * nj

    def tile_of(step):
        return step // nj, step % nj

    def now(s):
        return tile_of(jnp.minimum(s, steps - 1))

    def lag(s):
        return tile_of(jnp.maximum(s - 1, 0))

    row_spec = pl.BlockSpec((PEER_HEADS, rows, tm), lambda s: (0, lag(s)[1], lag(s)[0]))
    full_spec = pl.BlockSpec((PEER_HEADS, N_KEYS // 2, tm), lambda s: (0, 0, lag(s)[0]))
    blocks = (_nbytes((chunks, te, dc), FP8) + _nbytes((chunks, tm, dc), FP8)
              + 2 * _nbytes((PEER_HEADS, N_KEYS, tm), BF16)
              + 2 * _nbytes((PEER_HEADS, rows, tm), F32) + _nbytes((te, tm), BF16))
    smem = pl.BlockSpec(memory_space=pltpu.SMEM)
    return pl.pallas_call(
        functools.partial(_peer_up_kernel, expert_tiles=nj),
        out_shape=(jax.ShapeDtypeStruct((n_exp // 2, t), jnp.uint32),
                   jax.ShapeDtypeStruct((nj, BF16_ROWS, t), F32)),
        grid=(steps + 1,),
        in_specs=[smem, smem,
                  pl.BlockSpec((chunks, te, dc), lambda s: (0, now(s)[1], 0)),
                  pl.BlockSpec((chunks, tm, dc), lambda s: (0, now(s)[0], 0)),
                  pl.BlockSpec((1, tm), lambda s: (0, lag(s)[0])),
                  row_spec, row_spec, full_spec, full_spec],
        out_specs=(pl.BlockSpec((te // 2, tm), lambda s: (lag(s)[1], lag(s)[0])),
                   pl.BlockSpec((1, BF16_ROWS, tm), lambda s: (lag(s)[1], 0, lag(s)[0]))),
        scratch_shapes=[pltpu.VMEM((te, tm), F32), pltpu.VMEM((te, tm), F32)],
        compiler_params=_params(("arbitrary",), blocks, 3 * _nbytes((te, tm), F32)),
        name="peer_up",
    )(inv_u, inv_v, u, hn, inv_h, cnt, e1, rank, e2)


def _peer_down_kernel(vt_ref, w_ref, amax_ref, x_ref, o_ref, acc_ref):
    kk = pl.program_id(2)

    @pl.when(kk == 0)
    def _():
        acc_ref[...] = jnp.zeros(acc_ref.shape, F32)

    tiles = amax_ref.shape[0]
    rows = vt_ref.shape[1] // tiles
    part = None
    for i in range(tiles):
        scale, inv = _pow2_scale(jnp.max(amax_ref[i], axis=0, keepdims=True))
        w8 = (_unpack_rows(w_ref[i * rows // 2:(i + 1) * rows // 2, :]) * scale.astype(BF16)).astype(FP8)
        term = jnp.dot(vt_ref[:, i * rows:(i + 1) * rows], w8, preferred_element_type=F32) * inv
        part = term if part is None else part + term
    acc_ref[...] += part

    @pl.when(kk == pl.num_programs(2) - 1)
    def _():
        o_ref[...] = x_ref[...] + acc_ref[...].T


def _peer_down(vt, wt, w_amax, x, bd=1024, bt=1024, tk=2048):
    d, n_exp = vt.shape
    t = wt.shape[1]
    tiles = tk // (n_exp // w_amax.shape[0])
    blocks = (_nbytes((bd, tk), FP8) + _nbytes((tk, bt), BF16) + 2 * _nbytes((bt, bd), F32)
              + _nbytes((tiles, BF16_ROWS, bt), F32))
    return pl.pallas_call(
        _peer_down_kernel,
        out_shape=jax.ShapeDtypeStruct((t, d), F32),
        grid=(d // bd, t // bt, n_exp // tk),
        in_specs=[pl.BlockSpec((bd, tk), lambda i, j, k: (i, k)),
                  pl.BlockSpec((tk // 2, bt), lambda i, j, k: (k, j)),
                  pl.BlockSpec((tiles, BF16_ROWS, bt), lambda i, j, k: (k, 0, j)),
                  pl.BlockSpec((bt, bd), lambda i, j, k: (j, i))],
        out_specs=pl.BlockSpec((bt, bd), lambda i, j, k: (j, i)),
        scratch_shapes=[pltpu.VMEM((bd, bt), F32)],
        compiler_params=_params(("parallel", "parallel", "arbitrary"), blocks,
                                3 * _nbytes((bd, bt), F32) + 2 * _nbytes((tk, bt), BF16)),
        name="peer_down",
    )(vt, wt, w_amax, x)


def _ple_kernel(h_ref, wg_ref, p_ref, wp_ref, x_ref, o_ref):
    gate = jax.nn.sigmoid(jnp.dot(h_ref[...], wg_ref[...], preferred_element_type=F32))
    emb = jnp.dot(p_ref[...], wp_ref[...], preferred_element_type=F32)
    o_ref[...] = x_ref[...] + gate * emb


def _ple(hp, w_gate, p, w_proj, x, bm=1024, bn=512):
    t, d = hp.shape
    pd = p.shape[1]
    n = w_gate.shape[1]
    blocks = (_nbytes((bm, d), BF16) + _nbytes((d, bn), BF16) + _nbytes((bm, pd), BF16)
              + _nbytes((pd, bn), BF16) + 2 * _nbytes((bm, bn), F32))
    return pl.pallas_call(
        _ple_kernel,
        out_shape=jax.ShapeDtypeStruct((t, n), F32),
        grid=(t // bm, n // bn),
        in_specs=[pl.BlockSpec((bm, d), lambda i, j: (i, 0)),
                  pl.BlockSpec((d, bn), lambda i, j: (0, j)),
                  pl.BlockSpec((bm, pd), lambda i, j: (i, 0)),
                  pl.BlockSpec((pd, bn), lambda i, j: (0, j)),
                  pl.BlockSpec((bm, bn), lambda i, j: (i, j))],
        out_specs=pl.BlockSpec((bm, bn), lambda i, j: (i, j)),
        compiler_params=_params(("parallel", "parallel"), blocks, 2 * _nbytes((bm, bn), F32)),
        name="ple",
    )(hp, w_gate, p, w_proj, x)


def _qkv_column_scale():
    s = HEAD_DIM ** -0.5
    parts = [(DA_QK, s), (DA_QK, 1.0), (DA_V, 1.0), (SW_Q, s), (SW_KV, 1.0), (SW_KV, 1.0)]
    return jnp.concatenate([jnp.full((1, w), v, F32) for w, v in parts], axis=1)


@jax.jit
def kernel(x, p, positions, rel_bias, norm_mix, w_in, da_lambda, da_subln, sw_sinks, w_br_a, w_br_b, w_out,
           norm_ffn, peer_wq, peer_k1, peer_k2, peer_u, peer_v, norm_ple, ple_gate, ple_proj, norm_final):
    del positions
    batch, seq, d = x.shape
    t = batch * seq
    depth = w_in.shape[0]
    xf = x.reshape(t, d)
    da_bias = _bias_tiles(rel_bias[:, :DA_HEADS], DA_BLK, None, True)
    sw_bias = _bias_tiles(rel_bias[:, DA_HEADS:], WINDOW, WINDOW, False)
    col_scale = _qkv_column_scale()
    tile = pl.BlockSpec((1, 1024), lambda i, j: (0, j))
    for i in range(depth):
        lam_init = 0.8 - 0.6 * math.exp(-0.3 * i)
        h = _rmsnorm(xf, norm_mix[i], BF16)
        qkv = _matmul(_mm_scale_kernel, h, _narrow(w_in, i, 0, QKV_WIDTH, 1024), [col_scale], [tile],
                      BF16, 1024, 1024, "proj_qkv")
        gates = _matmul(_mm_sigmoid_kernel, h, _narrow(w_in, i, QKV_WIDTH, w_in.shape[2] - QKV_WIDTH, 1024),
                        [], [], BF16, 1024, 1024, "proj_gates")
        o_a = _diff_attention(qkv, da_bias, da_lambda[i], da_subln[i], lam_init, batch, seq)
        o_b = _sliding_attention(qkv, sw_bias, sw_sinks[i], batch, seq)
        merged = _merge(o_a, o_b, _narrow(w_br_a, i), _narrow(w_br_b, i), gates)
        xf = _matmul(_mm_residual_kernel, merged, _narrow(w_out, i), [xf],
                     [pl.BlockSpec((1024, 512), lambda i, j: (i, j))], F32, 1024, 512, "proj_out")
        hn, hn8, hn_inv = _rmsnorm_fp8(xf, norm_ffn[i], PEER_UP_CHUNKS)
        q = _matmul(_mm_plain_kernel, hn, _narrow(peer_wq, i), [], [], BF16, 1024, 1024, "peer_query")
        cnt, e1, rank, e2 = _route(q, peer_k1[i].astype(BF16), peer_k2[i].astype(BF16))
        u8, inv_u = _fp8_rows(peer_u, i, "chunked", PEER_UP_CHUNKS)
        vt8, inv_v = _fp8_rows(peer_v, i, "transposed")
        wt, w_amax = _peer_up(u8, inv_u, inv_v, hn8, hn_inv.reshape(1, t), cnt, e1, rank, e2)
        xf = _peer_down(vt8, wt, w_amax, xf)
        hp = _rmsnorm(xf, norm_ple[i], BF16)
        xf = _ple(hp, _narrow(ple_gate, i), p[i].reshape(t, -1).astype(BF16), ple_proj[i].astype(BF16), xf)
    return _rmsnorm(xf, norm_final, F32).reshape(batch, seq, d)
```

```python
import functools
import math

import jax
import jax.numpy as jnp
from jax import lax
from jax.experimental import pallas as pl
from jax.experimental.pallas import tpu as pltpu

F32 = jnp.float32
BF16 = jnp.bfloat16
FP8 = jnp.float8_e4m3fn
FP8_TARGET = 240.0
FP8_TINY = 1e-30

HEAD_DIM = 128
DA_HEADS = 8
DA_V_DIM = 2 * HEAD_DIM
SW_Q_HEADS = 16
SW_KV_HEADS = 4
SW_GROUP = SW_Q_HEADS // SW_KV_HEADS
WINDOW = 128
N_BUCKETS = 32
MAX_EXACT = N_BUCKETS // 2
MAX_DIST = 128
NEG = -1e30
DA_QK = DA_HEADS * 2 * HEAD_DIM
DA_V = DA_HEADS * DA_V_DIM
SW_Q = SW_Q_HEADS * HEAD_DIM
SW_KV = SW_KV_HEADS * HEAD_DIM
QKV_WIDTH = 3 * DA_QK + SW_Q + 2 * SW_KV
PEER_HEADS = 8
N_KEYS = 128
PEER_TOPK = 16
KEY_DIM = 128
EPS = 1e-6

V7X_VMEM_REQUEST_CAP = 60 * 1024 * 1024
BF16_ROWS = 16
DA_BLK = 512
PEER_UP_TE = 1024
FP8_ROW_BLOCK = 512
PEER_UP_CHUNKS = 4
PEER_UP_UNROLL = 2


def _nbytes(shape, dtype):
    return math.prod(shape) * jnp.dtype(dtype).itemsize


def _params(semantics, block_bytes, scratch_bytes=0, flags=None):
    need = int(1.25 * (2 * block_bytes + scratch_bytes)) + (4 << 20)
    return pltpu.CompilerParams(dimension_semantics=semantics,
                                vmem_limit_bytes=min(need, V7X_VMEM_REQUEST_CAP), flags=flags)


def _pack_rows(x):
    return pltpu.bitcast(x, jnp.uint32)


def _unpack_rows(x):
    return pltpu.bitcast(x, BF16)


def _dot_nt(a, b):
    return lax.dot_general(a, b, (((1,), (1,)), ((), ())), preferred_element_type=F32)


def _rmsnorm_kernel(x_ref, g_ref, o_ref):
    x = x_ref[...]
    y = x * lax.rsqrt(jnp.mean(x * x, axis=-1, keepdims=True) + EPS)
    o_ref[...] = (y * g_ref[...]).astype(o_ref.dtype)


def _rmsnorm(x, g, out_dtype, rows=256):
    t, d = x.shape
    blocks = _nbytes((rows, d), F32) + _nbytes((rows, d), out_dtype)
    return pl.pallas_call(
        _rmsnorm_kernel,
        out_shape=jax.ShapeDtypeStruct((t, d), out_dtype),
        grid=(t // rows,),
        in_specs=[pl.BlockSpec((rows, d), lambda i: (i, 0)),
                  pl.BlockSpec((1, d), lambda i: (0, 0))],
        out_specs=pl.BlockSpec((rows, d), lambda i: (i, 0)),
        compiler_params=_params(("parallel",), blocks, _nbytes((rows, d), F32)),
        name="rmsnorm",
    )(x, g.reshape(1, d))


def _rmsnorm_fp8_kernel(x_ref, g_ref, o_ref, oc_ref, inv_ref):
    x = x_ref[...]
    y = x * lax.rsqrt(jnp.mean(x * x, axis=-1, keepdims=True) + EPS) * g_ref[...]
    o_ref[...] = y.astype(o_ref.dtype)
    amax = jnp.maximum(jnp.max(jnp.abs(y), axis=-1, keepdims=True), FP8_TINY)
    inv_ref[...] = amax * (1.0 / FP8_TARGET)
    y8 = y * (FP8_TARGET / amax)
    dc = oc_ref.shape[2]
    for k in range(oc_ref.shape[0]):
        oc_ref[k] = y8[:, k * dc:(k + 1) * dc].astype(oc_ref.dtype)


def _rmsnorm_fp8(x, g, chunks, rows=256):
    t, d = x.shape
    dc = d // chunks
    blocks = _nbytes((rows, d), F32) + _nbytes((rows, d), BF16) + _nbytes((rows, d), FP8) + _nbytes((rows, 128), F32)
    return pl.pallas_call(
        _rmsnorm_fp8_kernel,
        out_shape=(jax.ShapeDtypeStruct((t, d), BF16), jax.ShapeDtypeStruct((chunks, t, dc), FP8),
                   jax.ShapeDtypeStruct((t, 1), F32)),
        grid=(t // rows,),
        in_specs=[pl.BlockSpec((rows, d), lambda i: (i, 0)),
                  pl.BlockSpec((1, d), lambda i: (0, 0))],
        out_specs=(pl.BlockSpec((rows, d), lambda i: (i, 0)),
                   pl.BlockSpec((chunks, rows, dc), lambda i: (0, i, 0)),
                   pl.BlockSpec((rows, 1), lambda i: (i, 0))),
        compiler_params=_params(("parallel",), blocks, 2 * _nbytes((rows, d), F32)),
        name="rmsnorm_fp8",
    )(x, g.reshape(1, d))


CAST_BLOCK_BYTES = 8 << 20


def _cast_kernel(x_ref, o_ref):
    o_ref[...] = x_ref[...].astype(o_ref.dtype)


def _narrow(w, layer, col0=0, ncols=None, bc=None):
    _, r, c = w.shape
    ncols = c if ncols is None else ncols
    bc = ncols if bc is None else bc
    br = min(r, CAST_BLOCK_BYTES // (bc * 4))
    return pl.pallas_call(
        _cast_kernel,
        out_shape=jax.ShapeDtypeStruct((r, ncols), BF16),
        grid=(r // br, ncols // bc),
        in_specs=[pl.BlockSpec((None, br, bc), lambda i, j: (layer, i, col0 // bc + j))],
        out_specs=pl.BlockSpec((br, bc), lambda i, j: (i, j)),
        compiler_params=_params(("parallel", "parallel"), _nbytes((br, bc), F32) + _nbytes((br, bc), BF16),
                                _nbytes((br, bc), F32)),
        name="narrow",
    )(w)


def _pow2_scale(amax):
    shift = jnp.floor(jnp.log2(FP8_TARGET / jnp.maximum(amax, FP8_TINY)))
    return jnp.exp2(shift), jnp.exp2(-shift)


def _fp8_rows_kernel(x_ref, o_ref, inv_ref, *, layout):
    x = x_ref[...]
    amax = jnp.max(jnp.max(jnp.abs(x), axis=0, keepdims=True), axis=1, keepdims=True)
    scale, inv = _pow2_scale(amax)
    inv_ref[...] = jnp.broadcast_to(inv, inv_ref.shape)
    y = x * scale
    if layout == "chunked":
        dc = o_ref.shape[2]
        for k in range(o_ref.shape[0]):
            o_ref[k] = y[:, k * dc:(k + 1) * dc].astype(o_ref.dtype)
    else:
        o_ref[...] = y.T.astype(o_ref.dtype)


def _fp8_rows(w, layer, layout, chunks=None, br=FP8_ROW_BLOCK):
    _, r, c = w.shape
    if layout == "chunked":
        shape = (chunks, r, c // chunks)
        out_spec = pl.BlockSpec((chunks, br, c // chunks), lambda i: (0, i, 0))
    else:
        shape = (c, r)
        out_spec = pl.BlockSpec((c, br), lambda i: (0, i))
    out, inv = pl.pallas_call(
        functools.partial(_fp8_rows_kernel, layout=layout),
        out_shape=(jax.ShapeDtypeStruct(shape, FP8), jax.ShapeDtypeStruct((r // br, 1, 128), F32)),
        grid=(r // br,),
        in_specs=[pl.BlockSpec((None, br, c), lambda i: (layer, i, 0))],
        out_specs=(out_spec, pl.BlockSpec((1, 1, 128), lambda i: (i, 0, 0))),
        compiler_params=_params(("parallel",), _nbytes((br, c), F32) + _nbytes((br, c), FP8), 2 * _nbytes((br, c), F32)),
        name="fp8_rows_" + layout,
    )(w)
    return out, inv[:, 0, 0]


def _mm_scale_kernel(a_ref, b_ref, s_ref, o_ref):
    acc = jnp.dot(a_ref[...], b_ref[...], preferred_element_type=F32)
    o_ref[...] = (acc * s_ref[...]).astype(o_ref.dtype)


def _mm_sigmoid_kernel(a_ref, b_ref, o_ref):
    acc = jnp.dot(a_ref[...], b_ref[...], preferred_element_type=F32)
    o_ref[...] = jax.nn.sigmoid(acc).astype(o_ref.dtype)


def _mm_plain_kernel(a_ref, b_ref, o_ref):
    o_ref[...] = jnp.dot(a_ref[...], b_ref[...], preferred_element_type=F32).astype(o_ref.dtype)


def _mm_residual_kernel(a_ref, b_ref, x_ref, o_ref):
    o_ref[...] = x_ref[...] + jnp.dot(a_ref[...], b_ref[...], preferred_element_type=F32)


def _matmul(body, a, b, extra, extra_specs, out_dtype, bm, bn, name):
    m, k = a.shape
    n = b.shape[1]
    blocks = (_nbytes((bm, k), a.dtype) + _nbytes((k, bn), b.dtype) + _nbytes((bm, bn), out_dtype)
              + sum(_nbytes(s.block_shape, e.dtype) for s, e in zip(extra_specs, extra)))
    return pl.pallas_call(
        body,
        out_shape=jax.ShapeDtypeStruct((m, n), out_dtype),
        grid=(m // bm, n // bn),
        in_specs=[pl.BlockSpec((bm, k), lambda i, j: (i, 0)),
                  pl.BlockSpec((k, bn), lambda i, j: (0, j))] + list(extra_specs),
        out_specs=pl.BlockSpec((bm, bn), lambda i, j: (i, j)),
        compiler_params=_params(("parallel", "parallel"), blocks, _nbytes((bm, bn), F32)),
        name=name,
    )(a, b, *extra)


def _bias_kernel(tab_ref, o_ref, *, blk, window, rebase):
    h = pl.program_id(0)
    r = lax.broadcasted_iota(jnp.int32, (blk, blk), 0)
    c = lax.broadcasted_iota(jnp.int32, (blk, blk), 1)
    base = tab_ref[N_BUCKETS - 1, h] if rebase else 0.0
    for delta in (0, 1):
        rel = r - c + delta * blk
        n = jnp.maximum(rel, 0)
        nf = jnp.maximum(n, 1).astype(F32)
        large = MAX_EXACT + (jnp.log(nf / MAX_EXACT) / math.log(MAX_DIST / MAX_EXACT)
                             * (N_BUCKETS - MAX_EXACT)).astype(jnp.int32)
        large = jnp.minimum(large, N_BUCKETS - 1)
        bucket = jnp.where(n < MAX_EXACT, n, large)
        bias = jnp.zeros((blk, blk), F32)
        for b in range(N_BUCKETS):
            bias = jnp.where(bucket == b, tab_ref[b, h] - base, bias)
        mask = rel >= 0
        if window is not None:
            mask = mask & (rel < window)
        o_ref[0, delta] = jnp.where(mask, bias, NEG)


def _bias_tiles(tab, blk, window, rebase):
    heads = tab.shape[1]
    return pl.pallas_call(
        functools.partial(_bias_kernel, blk=blk, window=window, rebase=rebase),
        out_shape=jax.ShapeDtypeStruct((heads, 2, blk, blk), F32),
        grid=(heads,),
        in_specs=[pl.BlockSpec(memory_space=pltpu.SMEM)],
        out_specs=pl.BlockSpec((1, 2, blk, blk), lambda h: (h, 0, 0, 0)),
        compiler_params=_params(("parallel",), _nbytes((2, blk, blk), F32), 4 * _nbytes((blk, blk), F32)),
        name="bias_tiles",
    )(tab)


def _da_kernel(q_ref, k_ref, v_ref, bias_ref, lam_ref, g_ref, o_ref, *, lam_init):
    blk = q_ref.shape[0]
    qi = pl.program_id(2)
    lp = lam_ref[...]
    lam = (jnp.exp(jnp.sum(lp[0:1] * lp[1:2], axis=-1, keepdims=True))
           - jnp.exp(jnp.sum(lp[2:3] * lp[3:4], axis=-1, keepdims=True)) + lam_init)

    def softmax_pv(j, case):
        dims = slice(j * HEAD_DIM, (j + 1) * HEAD_DIM)
        q = q_ref[:, dims]
        spans = [(slice(case * blk, (case + 1) * blk), bias_ref[0, 0])]
        if case >= 1:
            spans.append((slice((case - 1) * blk, case * blk), bias_ref[0, 1]))
        if case >= 2:
            spans.append((slice(0, (case - 1) * blk), None))
        scores = []
        for rows, bias in spans:
            s = _dot_nt(q, k_ref[rows, dims])
            scores.append(s if bias is None else s + bias)
        m = functools.reduce(jnp.maximum, [jnp.max(s, axis=-1, keepdims=True) for s in scores])
        probs = [jnp.exp(s - m) for s in scores]
        norm = sum(jnp.sum(p, axis=-1, keepdims=True) for p in probs)
        out = sum(jnp.dot(p.astype(BF16), v_ref[rows, :], preferred_element_type=F32)
                  for p, (rows, _) in zip(probs, spans))
        return out / norm

    for case in range(k_ref.shape[0] // blk):
        @pl.when(qi == case)
        def _(case=case):
            o = softmax_pv(0, case) - lam * softmax_pv(1, case)
            y = o * lax.rsqrt(jnp.mean(o * o, axis=-1, keepdims=True) + EPS)
            o_ref[...] = ((y * g_ref[...]) * (1.0 - lam_init)).astype(o_ref.dtype)


def _diff_attention(qkv, bias, lam_p, subln_g, lam_init, batch, seq):
    blk = DA_BLK
    nq = seq // blk
    blocks = (2 * _nbytes((blk, DA_V_DIM), BF16) + 2 * _nbytes((seq, DA_V_DIM), BF16)
              + _nbytes((2, blk, blk), F32))
    scratch = 6 * _nbytes((blk, seq), F32)
    k_col0 = DA_QK // DA_V_DIM
    v_col0 = 2 * DA_QK // DA_V_DIM
    return pl.pallas_call(
        functools.partial(_da_kernel, lam_init=lam_init),
        out_shape=jax.ShapeDtypeStruct((batch * seq, DA_V), BF16),
        grid=(batch, DA_HEADS, nq),
        in_specs=[pl.BlockSpec((blk, DA_V_DIM), lambda b, h, i: (b * nq + i, h)),
                  pl.BlockSpec((seq, DA_V_DIM), lambda b, h, i: (b, k_col0 + h)),
                  pl.BlockSpec((seq, DA_V_DIM), lambda b, h, i: (b, v_col0 + h)),
                  pl.BlockSpec((1, 2, blk, blk), lambda b, h, i: (h, 0, 0, 0)),
                  pl.BlockSpec((4, HEAD_DIM), lambda b, h, i: (0, 0)),
                  pl.BlockSpec((1, DA_V_DIM), lambda b, h, i: (0, 0))],
        out_specs=pl.BlockSpec((blk, DA_V_DIM), lambda b, h, i: (b * nq + i, h)),
        compiler_params=_params(("parallel", "parallel", "parallel"), blocks, scratch),
        name="diff_attention",
    )(qkv, qkv, qkv, bias, lam_p, subln_g.reshape(1, DA_V_DIM))


def _swa_kernel(q_ref, kc_ref, kp_ref, vc_ref, vp_ref, bias_ref, sink_ref, o_ref):
    n = pl.program_id(1)
    is_prev = lax.broadcasted_iota(jnp.int32, (1, 2 * WINDOW), 1) < WINDOW
    no_prev = jnp.where(is_prev & (n == 0), NEG, 0.0).astype(F32)
    for hk in range(SW_KV_HEADS):
        cols = slice(hk * HEAD_DIM, (hk + 1) * HEAD_DIM)
        heads = [slice((hk * SW_GROUP + g) * HEAD_DIM, (hk * SW_GROUP + g + 1) * HEAD_DIM) for g in range(SW_GROUP)]
        q = jnp.concatenate([q_ref[:, hd] for hd in heads], axis=0)
        k = jnp.concatenate([kp_ref[:, cols], kc_ref[:, cols]], axis=0)
        v = jnp.concatenate([vp_ref[:, cols], vc_ref[:, cols]], axis=0)
        s = _dot_nt(q, k) + bias_ref[hk] + no_prev
        sink = sink_ref[hk]
        m = jnp.maximum(jnp.max(s, axis=-1, keepdims=True), sink)
        e = jnp.exp(s - m)
        den = jnp.sum(e, axis=-1, keepdims=True) + jnp.exp(sink - m)
        o = jnp.dot(e.astype(BF16), v, preferred_element_type=F32) / den
        for g, hd in enumerate(heads):
            o_ref[:, hd] = o[g * WINDOW:(g + 1) * WINDOW, :].astype(o_ref.dtype)


def _sliding_attention(qkv, bias_tiles, sinks, batch, seq):
    nb = seq // WINDOW
    q_col = 3 * DA_QK // SW_Q
    k_col = (3 * DA_QK + SW_Q) // SW_KV
    v_col = k_col + 1
    cur = lambda b, n: b * nb + n
    prev = lambda b, n: b * nb + jnp.maximum(n - 1, 0)
    rows = SW_GROUP * WINDOW
    bias = bias_tiles.reshape(SW_KV_HEADS, SW_GROUP, 2, WINDOW, WINDOW)[:, :, ::-1]
    bias = bias.transpose(0, 1, 3, 2, 4).reshape(SW_KV_HEADS, rows, 2 * WINDOW)
    sink_cols = jnp.repeat(sinks.astype(F32).reshape(SW_KV_HEADS, SW_GROUP), WINDOW, axis=1).reshape(
        SW_KV_HEADS, rows, 1)
    blocks = (2 * _nbytes((WINDOW, SW_Q), BF16) + 4 * _nbytes((WINDOW, SW_KV), BF16)
              + _nbytes((SW_KV_HEADS, rows, 2 * WINDOW), F32) + _nbytes((SW_KV_HEADS, rows, 128), F32))
    return pl.pallas_call(
        _swa_kernel,
        out_shape=jax.ShapeDtypeStruct((batch * seq, SW_Q), BF16),
        grid=(batch, nb),
        in_specs=[pl.BlockSpec((WINDOW, SW_Q), lambda b, n: (cur(b, n), q_col)),
                  pl.BlockSpec((WINDOW, SW_KV), lambda b, n: (cur(b, n), k_col)),
                  pl.BlockSpec((WINDOW, SW_KV), lambda b, n: (prev(b, n), k_col)),
                  pl.BlockSpec((WINDOW, SW_KV), lambda b, n: (cur(b, n), v_col)),
                  pl.BlockSpec((WINDOW, SW_KV), lambda b, n: (prev(b, n), v_col)),
                  pl.BlockSpec((SW_KV_HEADS, rows, 2 * WINDOW), lambda b, n: (0, 0, 0)),
                  pl.BlockSpec((SW_KV_HEADS, rows, 1), lambda b, n: (0, 0, 0))],
        out_specs=pl.BlockSpec((WINDOW, SW_Q), lambda b, n: (cur(b, n), 0)),
        compiler_params=_params(("parallel", "parallel"), blocks, 16 * _nbytes((rows, 2 * WINDOW), F32)),
        name="sliding_attention",
    )(qkv, qkv, qkv, qkv, qkv, bias, sink_cols)


def _merge_kernel(oa_ref, ob_ref, wa_ref, wb_ref, ga_ref, gb_ref, o_ref):
    a = jnp.dot(oa_ref[...], wa_ref[...], preferred_element_type=F32)
    b = jnp.dot(ob_ref[...], wb_ref[...], preferred_element_type=F32)
    o_ref[...] = (ga_ref[...].astype(F32) * a + gb_ref[...].astype(F32) * b).astype(o_ref.dtype)


def _merge(o_a, o_b, w_a, w_b, gates, bm=1024, bn=1024):
    t, ka = o_a.shape
    kb = o_b.shape[1]
    d = w_a.shape[1]
    nj = d // bn
    blocks = (_nbytes((bm, ka), BF16) + _nbytes((bm, kb), BF16) + _nbytes((ka, bn), BF16)
              + _nbytes((kb, bn), BF16) + 2 * _nbytes((bm, bn), gates.dtype) + _nbytes((bm, bn), BF16))
    return pl.pallas_call(
        _merge_kernel,
        out_shape=jax.ShapeDtypeStruct((t, d), BF16),
        grid=(t // bm, nj),
        in_specs=[pl.BlockSpec((bm, ka), lambda i, j: (i, 0)),
                  pl.BlockSpec((bm, kb), lambda i, j: (i, 0)),
                  pl.BlockSpec((ka, bn), lambda i, j: (0, j)),
                  pl.BlockSpec((kb, bn), lambda i, j: (0, j)),
                  pl.BlockSpec((bm, bn), lambda i, j: (i, j)),
                  pl.BlockSpec((bm, bn), lambda i, j: (i, nj + j))],
        out_specs=pl.BlockSpec((bm, bn), lambda i, j: (i, j)),
        compiler_params=_params(("parallel", "parallel"), blocks, 2 * _nbytes((bm, bn), F32)),
        name="merge",
    )(o_a, o_b, w_a, w_b, gates, gates)


def _sort_pairs(n):
    pairs = []

    def merge(lo, hi, r):
        step = r * 2
        if step < hi - lo:
            merge(lo, hi, step)
            merge(lo + r, hi, step)
            pairs.extend((i, i + r) for i in range(lo + r, hi - r, step))
        else:
            pairs.append((lo, lo + r))

    def sort(lo, hi):
        if hi - lo >= 1:
            mid = lo + (hi - lo) // 2
            sort(lo, mid)
            sort(mid + 1, hi)
            merge(lo, hi, 1)

    sort(0, n - 1)
    return pairs


_SORT16 = _sort_pairs(PEER_TOPK)


def _sort_desc(xs):
    xs = list(xs)
    for i, j in _SORT16:
        xs[i], xs[j] = jnp.maximum(xs[i], xs[j]), jnp.minimum(xs[i], xs[j])
    return xs


def _merge_top(a, b):
    k = PEER_TOPK
    xs = [jnp.maximum(a[i], b[k - 1 - i]) for i in range(k)]
    d = k // 2
    while d >= 1:
        for i in range(k):
            if not i & d:
                xs[i], xs[i + d] = jnp.maximum(xs[i], xs[i + d]), jnp.minimum(xs[i], xs[i + d])
        d //= 2
    return xs


def _top16_over_rows(s):
    groups = [s[a * 8:(a + 1) * 8, :] for a in range(s.shape[0] // 8)]
    xs = _sort_desc(groups)
    for shift in (4, 2, 1):
        xs = _merge_top(xs, [pltpu.roll(x, shift, 0) for x in xs])
    return xs


def _count_leading(pred, values):
    n = len(values)

    def pick(lo, hi, taken):
        if not taken:
            return values[(lo + hi) // 2 - 1]
        mid = (lo + hi) // 2
        return jnp.where(taken[0], pick(mid, hi, taken[1:]), pick(lo, mid, taken[1:]))

    taken = []
    step = n // 2
    while step >= 1:
        taken.append(pred(pick(0, n, taken)))
        step //= 2
    total = None
    for i, t in enumerate(taken):
        part = jnp.where(t, float(n >> (i + 1)), 0.0)
        total = part if total is None else total + part
    return total + jnp.where(pred(values[n - 1]), 1.0, 0.0)


def _route_kernel(q_ref, k1_ref, k2_ref, cnt_ref, e1_ref, rank_ref, e2_ref):
    q = q_ref[...]
    s1 = _dot_nt(k1_ref[...], q[:, :KEY_DIM])
    s2 = _dot_nt(k2_ref[...], q[:, KEY_DIM:])
    v1 = _top16_over_rows(s1)
    v2 = _top16_over_rows(s2)
    k = PEER_TOPK
    top = [v1[0] + v2[b] for b in range(k)]
    rest = [v1[a] + v2[b] for a in range(1, k) for b in range(k) if (a + 1) * (b + 1) <= k]
    pad = jnp.full(top[0].shape, -jnp.inf, F32)
    rest = rest + [pad] * (-len(rest) % k)
    for g in range(len(rest) // k):
        top = _merge_top(top, _sort_desc(rest[g * k:(g + 1) * k]))
    z = jnp.ones_like(top[0])
    for c in top[1:]:
        z = z + jnp.exp(c - top[0])
    tau = top[k - 1][0:1]
    best = [v[0:1] for v in v2]
    cnt = _count_leading(lambda b: s1 + b >= tau, best)
    rank = _count_leading(lambda b: b > s2, best)
    cnt_ref[0] = cnt
    rank_ref[0] = _pack_rows(rank.astype(BF16))
    e1_ref[0] = jnp.exp(s1 - v1[0][0:1]) / z[0:1]
    e2_ref[0] = _pack_rows(jnp.exp(s2 - v2[0][0:1]).astype(BF16))


def _route(q, k1, k2, tm=1024):
    t = q.shape[0]
    wide = jax.ShapeDtypeStruct((PEER_HEADS, N_KEYS, t), F32)
    narrow = jax.ShapeDtypeStruct((PEER_HEADS, N_KEYS // 2, t), jnp.uint32)
    spec = pl.BlockSpec((1, N_KEYS, tm), lambda i, h: (h, 0, i))
    narrow_spec = pl.BlockSpec((1, N_KEYS // 2, tm), lambda i, h: (h, 0, i))
    blocks = _nbytes((tm, 2 * KEY_DIM), BF16) + 3 * _nbytes((N_KEYS, tm), F32)
    return pl.pallas_call(
        _route_kernel,
        out_shape=(wide, wide, narrow, narrow),
        grid=(t // tm, PEER_HEADS),
        in_specs=[pl.BlockSpec((tm, 2 * KEY_DIM), lambda i, h: (i, h)),
                  pl.BlockSpec((N_KEYS, KEY_DIM), lambda i, h: (0, 0)),
                  pl.BlockSpec((N_KEYS, KEY_DIM), lambda i, h: (0, 0))],
        out_specs=(spec, spec, narrow_spec, narrow_spec),
        compiler_params=_params(("parallel", "parallel"), blocks, 24 * _nbytes((N_KEYS, tm), F32)),
        name="peer_route",
    )(q, k1, k2)


def _peer_up_kernel(inv_u_ref, inv_v_ref, u_ref, h_ref, inv_ref, cnt_ref, e1_ref, rank_ref, e2_ref,
                    o_ref, amax_ref, act_a, act_b, *, expert_tiles):
    s = pl.program_id(0)
    first_scale_block = (jnp.maximum(s - 1, 0) % expert_tiles) * (act_a.shape[0] // FP8_ROW_BLOCK)

    @pl.when(s == 0)
    def _():
        act_a[...] = jnp.zeros(act_a.shape, F32)
        act_b[...] = jnp.zeros(act_b.shape, F32)

    packed = (N_KEYS // BF16_ROWS, BF16_ROWS, 128)
    chunks = u_ref.shape[0]
    blocks_per_chunk = act_a.shape[0] // N_KEYS // chunks

    def epilogue_block(act_old, r):
        rows = pl.ds(pl.multiple_of(r * N_KEYS, N_KEYS), N_KEYS)
        out_rows = pl.ds(pl.multiple_of(r * (N_KEYS // 2), N_KEYS // 2), N_KEYS // 2)
        scale_block = first_scale_block + r // (FP8_ROW_BLOCK // N_KEYS)
        inv_u, inv_v = inv_u_ref[scale_block], inv_v_ref[scale_block]
        cnt_rows = [cnt_ref[h, pl.ds(r, 1), :] for h in range(PEER_HEADS)]
        e1_rows = [e1_ref[h, pl.ds(r, 1), :] * inv_v for h in range(PEER_HEADS)]
        for c in range(act_old.shape[1] // 128):
            cols = slice(c * 128, (c + 1) * 128)
            act = act_old[rows, cols]
            act_old[rows, cols] = jnp.zeros_like(act)
            act = act * (inv_ref[:, cols] * inv_u)
            gate = None
            for h in range(PEER_HEADS):
                cnt = jnp.tile(jnp.broadcast_to(cnt_rows[h][:, cols], packed[1:]).astype(BF16), (packed[0], 1))
                e1 = jnp.tile(jnp.broadcast_to(e1_rows[h][:, cols], packed[1:]).astype(BF16), (packed[0], 1))
                routed = _unpack_rows(rank_ref[h, :, cols]) < cnt
                term = jnp.where(routed, _unpack_rows(e2_ref[h, :, cols]) * e1, jnp.zeros((), BF16))
                gate = term if gate is None else gate + term
            w = jax.nn.gelu(act.astype(BF16)) * gate
            o_ref[out_rows, cols] = _pack_rows(w)
            peak = jnp.max(jnp.abs(w).reshape(packed), axis=0).astype(F32)
            amax_ref[0, :, cols] = jnp.where(r == 0, peak, jnp.maximum(amax_ref[0, :, cols], peak))

    def step(act_new, act_old):
        def chunk(k, carry):
            act_new[...] += _dot_nt(u_ref[k], h_ref[k])
            for b in range(blocks_per_chunk):
                epilogue_block(act_old, k * blocks_per_chunk + b)
            return carry

        lax.fori_loop(0, chunks, chunk, 0, unroll=PEER_UP_UNROLL)

    @pl.when(s % 2 == 0)
    def _():
        step(act_a, act_b)

    @pl.when(s % 2 == 1)
    def _():
        step(act_b, act_a)


def _peer_up(u, inv_u, inv_v, hn, inv_h, cnt, e1, rank, e2, te=PEER_UP_TE, tm=512):
    chunks, n_exp, dc = u.shape
    t = hn.shape[1]
    rows = te // N_KEYS
    assert rows % chunks == 0
    nj = n_exp // te
    steps = (t // tm) * nj

    def tile_of(step):
        return step // nj, step % nj

    def now(s):
        return tile_of(jnp.minimum(s, steps - 1))

    def lag(s):
        return tile_of(jnp.maximum(s - 1, 0))

    row_spec = pl.BlockSpec((PEER_HEADS, rows, tm), lambda s: (0, lag(s)[1], lag(s)[0]))
    full_spec = pl.BlockSpec((PEER_HEADS, N_KEYS // 2, tm), lambda s: (0, 0, lag(s)[0]))
    blocks = (_nbytes((chunks, te, dc), FP8) + _nbytes((chunks, tm, dc), FP8)
              + 2 * _nbytes((PEER_HEADS, N_KEYS, tm), BF16)
              + 2 * _nbytes((PEER_HEADS, rows, tm), F32) + _nbytes((te, tm), BF16))
    smem = pl.BlockSpec(memory_space=pltpu.SMEM)
    return pl.pallas_call(
        functools.partial(_peer_up_kernel, expert_tiles=nj),
        out_shape=(jax.ShapeDtypeStruct((n_exp // 2, t), jnp.uint32),
                   jax.ShapeDtypeStruct((nj, BF16_ROWS, t), F32)),
        grid=(steps + 1,),
        in_specs=[smem, smem,
                  pl.BlockSpec((chunks, te, dc), lambda s: (0, now(s)[1], 0)),
                  pl.BlockSpec((chunks, tm, dc), lambda s: (0, now(s)[0], 0)),
                  pl.BlockSpec((1, tm), lambda s: (0, lag(s)[0])),
                  row_spec, row_spec, full_spec, full_spec],
        out_specs=(pl.BlockSpec((te // 2, tm), lambda s: (lag(s)[1], lag(s)[0])),
                   pl.BlockSpec((1, BF16_ROWS, tm), lambda s: (lag(s)[1], 0, lag(s)[0]))),
        scratch_shapes=[pltpu.VMEM((te, tm), F32), pltpu.VMEM((te, tm), F32)],
        compiler_params=_params(("arbitrary",), blocks, 3 * _nbytes((te, tm), F32)),
        name="peer_up",
    )(inv_u, inv_v, u, hn, inv_h, cnt, e1, rank, e2)


def _peer_down_kernel(vt_ref, w_ref, amax_ref, x_ref, o_ref, acc_ref):
    kk = pl.program_id(2)

    @pl.when(kk == 0)
    def _():
        acc_ref[...] = jnp.zeros(acc_ref.shape, F32)

    amax = functools.reduce(jnp.maximum, [amax_ref[i] for i in range(amax_ref.shape[0])])
    scale, inv = _pow2_scale(jnp.max(amax, axis=0, keepdims=True))
    w8 = (_unpack_rows(w_ref[...]) * scale.astype(BF16)).astype(FP8)
    acc_ref[...] += jnp.dot(vt_ref[...], w8, preferred_element_type=F32) * inv

    @pl.when(kk == pl.num_programs(2) - 1)
    def _():
        o_ref[...] = x_ref[...] + acc_ref[...].T


def _peer_down(vt, wt, w_amax, x, bd=1024, bt=1024, tk=4096):
    d, n_exp = vt.shape
    t = wt.shape[1]
    tiles = tk // (n_exp // w_amax.shape[0])
    blocks = (_nbytes((bd, tk), FP8) + _nbytes((tk, bt), BF16) + 2 * _nbytes((bt, bd), F32)
              + _nbytes((tiles, BF16_ROWS, bt), F32))
    return pl.pallas_call(
        _peer_down_kernel,
        out_shape=jax.ShapeDtypeStruct((t, d), F32),
        grid=(d // bd, t // bt, n_exp // tk),
        in_specs=[pl.BlockSpec((bd, tk), lambda i, j, k: (i, k)),
                  pl.BlockSpec((tk // 2, bt), lambda i, j, k: (k, j)),
                  pl.BlockSpec((tiles, BF16_ROWS, bt), lambda i, j, k: (k, 0, j)),
                  pl.BlockSpec((bt, bd), lambda i, j, k: (j, i))],
        out_specs=pl.BlockSpec((bt, bd), lambda i, j, k: (j, i)),
        scratch_shapes=[pltpu.VMEM((bd, bt), F32)],
        compiler_params=_params(("parallel", "parallel", "arbitrary"), blocks,
                                3 * _nbytes((bd, bt), F32) + 2 * _nbytes((tk, bt), BF16)),
        name="peer_down",
    )(vt, wt, w_amax, x)


def _ple_kernel(h_ref, wg_ref, p_ref, wp_ref, x_ref, o_ref):
    gate = jax.nn.sigmoid(jnp.dot(h_ref[...], wg_ref[...], preferred_element_type=F32))
    emb = jnp.dot(p_ref[...], wp_ref[...], preferred_element_type=F32)
    o_ref[...] = x_ref[...] + gate * emb


def _ple(hp, w_gate, p, w_proj, x, bm=1024, bn=512):
    t, d = hp.shape
    pd = p.shape[1]
    n = w_gate.shape[1]
    blocks = (_nbytes((bm, d), BF16) + _nbytes((d, bn), BF16) + _nbytes((bm, pd), BF16)
              + _nbytes((pd, bn), BF16) + 2 * _nbytes((bm, bn), F32))
    return pl.pallas_call(
        _ple_kernel,
        out_shape=jax.ShapeDtypeStruct((t, n), F32),
        grid=(t // bm, n // bn),
        in_specs=[pl.BlockSpec((bm, d), lambda i, j: (i, 0)),
                  pl.BlockSpec((d, bn), lambda i, j: (0, j)),
                  pl.BlockSpec((bm, pd), lambda i, j: (i, 0)),
                  pl.BlockSpec((pd, bn), lambda i, j: (0, j)),
                  pl.BlockSpec((bm, bn), lambda i, j: (i, j))],
        out_specs=pl.BlockSpec((bm, bn), lambda i, j: (i, j)),
        compiler_params=_params(("parallel", "parallel"), blocks, 2 * _nbytes((bm, bn), F32)),
        name="ple",
    )(hp, w_gate, p, w_proj, x)


def _qkv_column_scale():
    s = HEAD_DIM ** -0.5
    parts = [(DA_QK, s), (DA_QK, 1.0), (DA_V, 1.0), (SW_Q, s), (SW_KV, 1.0), (SW_KV, 1.0)]
    return jnp.concatenate([jnp.full((1, w), v, F32) for w, v in parts], axis=1)


@jax.jit
def kernel(x, p, positions, rel_bias, norm_mix, w_in, da_lambda, da_subln, sw_sinks, w_br_a, w_br_b, w_out,
           norm_ffn, peer_wq, peer_k1, peer_k2, peer_u, peer_v, norm_ple, ple_gate, ple_proj, norm_final):
    del positions
    batch, seq, d = x.shape
    t = batch * seq
    depth = w_in.shape[0]
    xf = x.reshape(t, d)
    da_bias = _bias_tiles(rel_bias[:, :DA_HEADS], DA_BLK, None, True)
    sw_bias = _bias_tiles(rel_bias[:, DA_HEADS:], WINDOW, WINDOW, False)
    col_scale = _qkv_column_scale()
    tile = pl.BlockSpec((1, 1024), lambda i, j: (0, j))
    for i in range(depth):
        lam_init = 0.8 - 0.6 * math.exp(-0.3 * i)
        h = _rmsnorm(xf, norm_mix[i], BF16)
        qkv = _matmul(_mm_scale_kernel, h, _narrow(w_in, i, 0, QKV_WIDTH, 1024), [col_scale], [tile],
                      BF16, 1024, 1024, "proj_qkv")
        gates = _matmul(_mm_sigmoid_kernel, h, _narrow(w_in, i, QKV_WIDTH, w_in.shape[2] - QKV_WIDTH, 1024),
                        [], [], BF16, 1024, 1024, "proj_gates")
        o_a = _diff_attention(qkv, da_bias, da_lambda[i], da_subln[i], lam_init, batch, seq)
        o_b = _sliding_attention(qkv, sw_bias, sw_sinks[i], batch, seq)
        merged = _merge(o_a, o_b, _narrow(w_br_a, i), _narrow(w_br_b, i), gates)
        xf = _matmul(_mm_residual_kernel, merged, _narrow(w_out, i), [xf],
                     [pl.BlockSpec((1024, 512), lambda i, j: (i, j))], F32, 1024, 512, "proj_out")
        hn, hn8, hn_inv = _rmsnorm_fp8(xf, norm_ffn[i], PEER_UP_CHUNKS)
        q = _matmul(_mm_plain_kernel, hn, _narrow(peer_wq, i), [], [], BF16, 1024, 1024, "peer_query")
        cnt, e1, rank, e2 = _route(q, peer_k1[i].astype(BF16), peer_k2[i].astype(BF16))
        u8, inv_u = _fp8_rows(peer_u, i, "chunked", PEER_UP_CHUNKS)
        vt8, inv_v = _fp8_rows(peer_v, i, "transposed")
        wt, w_amax = _peer_up(u8, inv_u, inv_v, hn8, hn_inv.reshape(1, t), cnt, e1, rank, e2)
        xf = _peer_down(vt8, wt, w_amax, xf)
        hp = _rmsnorm(xf, norm_ple[i], BF16)
        xf = _ple(hp, _narrow(ple_gate, i), p[i].reshape(t, -1).astype(BF16), ple_proj[i].astype(BF16), xf)
    return _rmsnorm(xf, norm_final, F32).reshape(batch, seq, d)
```

```python
import functools
import math

import jax
import jax.numpy as jnp
from jax import lax
from jax.experimental import pallas as pl
from jax.experimental.pallas import tpu as pltpu

F32 = jnp.float32
BF16 = jnp.bfloat16
FP8 = jnp.float8_e4m3fn
FP8_TARGET = 240.0
FP8_TINY = 1e-30

HEAD_DIM = 128
DA_HEADS = 8
DA_V_DIM = 2 * HEAD_DIM
SW_Q_HEADS = 16
SW_KV_HEADS = 4
SW_GROUP = SW_Q_HEADS // SW_KV_HEADS
WINDOW = 128
N_BUCKETS = 32
MAX_EXACT = N_BUCKETS // 2
MAX_DIST = 128
NEG = -1e30
DA_QK = DA_HEADS * 2 * HEAD_DIM
DA_V = DA_HEADS * DA_V_DIM
SW_Q = SW_Q_HEADS * HEAD_DIM
SW_KV = SW_KV_HEADS * HEAD_DIM
QKV_WIDTH = 3 * DA_QK + SW_Q + 2 * SW_KV
PEER_HEADS = 8
N_KEYS = 128
PEER_TOPK = 16
KEY_DIM = 128
EPS = 1e-6

V7X_VMEM_REQUEST_CAP = 60 * 1024 * 1024
BF16_ROWS = 16
DA_BLK = 512
PEER_UP_TE = 1024
FP8_ROW_BLOCK = 512
PEER_UP_CHUNKS = 4
PEER_UP_UNROLL = 2


def _nbytes(shape, dtype):
    return math.prod(shape) * jnp.dtype(dtype).itemsize


def _params(semantics, block_bytes, scratch_bytes=0, flags=None):
    need = int(1.25 * (2 * block_bytes + scratch_bytes)) + (4 << 20)
    return pltpu.CompilerParams(dimension_semantics=semantics,
                                vmem_limit_bytes=min(need, V7X_VMEM_REQUEST_CAP), flags=flags)


def _pack_rows(x):
    return pltpu.bitcast(x, jnp.uint32)


def _unpack_rows(x, dtype=BF16):
    return pltpu.bitcast(x, dtype)


def _dot_nt(a, b):
    return lax.dot_general(a, b, (((1,), (1,)), ((), ())), preferred_element_type=F32)


def _rmsnorm_kernel(x_ref, g_ref, o_ref):
    x = x_ref[...]
    y = x * lax.rsqrt(jnp.mean(x * x, axis=-1, keepdims=True) + EPS)
    o_ref[...] = (y * g_ref[...]).astype(o_ref.dtype)


def _rmsnorm(x, g, out_dtype, rows=256):
    t, d = x.shape
    blocks = _nbytes((rows, d), F32) + _nbytes((rows, d), out_dtype)
    return pl.pallas_call(
        _rmsnorm_kernel,
        out_shape=jax.ShapeDtypeStruct((t, d), out_dtype),
        grid=(t // rows,),
        in_specs=[pl.BlockSpec((rows, d), lambda i: (i, 0)),
                  pl.BlockSpec((1, d), lambda i: (0, 0))],
        out_specs=pl.BlockSpec((rows, d), lambda i: (i, 0)),
        compiler_params=_params(("parallel",), blocks, _nbytes((rows, d), F32)),
        name="rmsnorm",
    )(x, g.reshape(1, d))


def _rmsnorm_fp8_kernel(x_ref, g_ref, o_ref, oc_ref, inv_ref, norm_ref):
    x = x_ref[...]
    y = x * lax.rsqrt(jnp.mean(x * x, axis=-1, keepdims=True) + EPS) * g_ref[...]
    o_ref[...] = y.astype(o_ref.dtype)
    norm_ref[...] = jnp.sqrt(jnp.sum(y * y, axis=-1, keepdims=True))
    amax = jnp.maximum(jnp.max(jnp.abs(y), axis=-1, keepdims=True), FP8_TINY)
    inv_ref[...] = amax * (1.0 / FP8_TARGET)
    y8 = y * (FP8_TARGET / amax)
    dc = oc_ref.shape[2]
    for k in range(oc_ref.shape[0]):
        oc_ref[k] = y8[:, k * dc:(k + 1) * dc].astype(oc_ref.dtype)


def _rmsnorm_fp8(x, g, chunks, rows=256):
    t, d = x.shape
    dc = d // chunks
    blocks = _nbytes((rows, d), F32) + _nbytes((rows, d), BF16) + _nbytes((rows, d), FP8) + _nbytes((rows, 128), F32)
    return pl.pallas_call(
        _rmsnorm_fp8_kernel,
        out_shape=(jax.ShapeDtypeStruct((t, d), BF16), jax.ShapeDtypeStruct((chunks, t, dc), FP8),
                   jax.ShapeDtypeStruct((t, 1), F32), jax.ShapeDtypeStruct((t, 1), F32)),
        grid=(t // rows,),
        in_specs=[pl.BlockSpec((rows, d), lambda i: (i, 0)),
                  pl.BlockSpec((1, d), lambda i: (0, 0))],
        out_specs=(pl.BlockSpec((rows, d), lambda i: (i, 0)),
                   pl.BlockSpec((chunks, rows, dc), lambda i: (0, i, 0)),
                   pl.BlockSpec((rows, 1), lambda i: (i, 0)),
                   pl.BlockSpec((rows, 1), lambda i: (i, 0))),
        compiler_params=_params(("parallel",), blocks, 2 * _nbytes((rows, d), F32)),
        name="rmsnorm_fp8",
    )(x, g.reshape(1, d))


CAST_BLOCK_BYTES = 8 << 20


def _cast_kernel(x_ref, o_ref):
    o_ref[...] = x_ref[...].astype(o_ref.dtype)


def _narrow(w, layer, col0=0, ncols=None, bc=None):
    _, r, c = w.shape
    ncols = c if ncols is None else ncols
    bc = ncols if bc is None else bc
    br = min(r, CAST_BLOCK_BYTES // (bc * 4))
    return pl.pallas_call(
        _cast_kernel,
        out_shape=jax.ShapeDtypeStruct((r, ncols), BF16),
        grid=(r // br, ncols // bc),
        in_specs=[pl.BlockSpec((None, br, bc), lambda i, j: (layer, i, col0 // bc + j))],
        out_specs=pl.BlockSpec((br, bc), lambda i, j: (i, j)),
        compiler_params=_params(("parallel", "parallel"), _nbytes((br, bc), F32) + _nbytes((br, bc), BF16),
                                _nbytes((br, bc), F32)),
        name="narrow",
    )(w)


def _pow2_scale(amax):
    shift = jnp.floor(jnp.log2(FP8_TARGET / jnp.maximum(amax, FP8_TINY)))
    return jnp.exp2(shift), jnp.exp2(-shift)


def _fp8_rows_kernel(x_ref, o_ref, inv_ref, norm_ref, *, layout):
    x = x_ref[...]
    norm = jnp.sqrt(jnp.max(jnp.sum(x * x, axis=1, keepdims=True), axis=0, keepdims=True))
    norm_ref[...] = jnp.broadcast_to(norm, norm_ref.shape)
    amax = jnp.max(jnp.max(jnp.abs(x), axis=0, keepdims=True), axis=1, keepdims=True)
    scale, inv = _pow2_scale(amax)
    inv_ref[...] = jnp.broadcast_to(inv, inv_ref.shape)
    y = x * scale
    if layout == "chunked":
        dc = o_ref.shape[2]
        for k in range(o_ref.shape[0]):
            o_ref[k] = y[:, k * dc:(k + 1) * dc].astype(o_ref.dtype)
    else:
        o_ref[...] = y.T.astype(o_ref.dtype)


def _fp8_rows(w, layer, layout, chunks=None, br=FP8_ROW_BLOCK):
    _, r, c = w.shape
    if layout == "chunked":
        shape = (chunks, r, c // chunks)
        out_spec = pl.BlockSpec((chunks, br, c // chunks), lambda i: (0, i, 0))
    else:
        shape = (c, r)
        out_spec = pl.BlockSpec((c, br), lambda i: (0, i))
    small = jax.ShapeDtypeStruct((r // br, 1, 128), F32)
    small_spec = pl.BlockSpec((1, 1, 128), lambda i: (i, 0, 0))
    out, inv, norm = pl.pallas_call(
        functools.partial(_fp8_rows_kernel, layout=layout),
        out_shape=(jax.ShapeDtypeStruct(shape, FP8), small, small),
        grid=(r // br,),
        in_specs=[pl.BlockSpec((None, br, c), lambda i: (layer, i, 0))],
        out_specs=(out_spec, small_spec, small_spec),
        compiler_params=_params(("parallel",), _nbytes((br, c), F32) + _nbytes((br, c), FP8), 2 * _nbytes((br, c), F32)),
        name="fp8_rows_" + layout,
    )(w)
    return out, inv[:, 0, 0], jnp.max(norm)


def _mm_scale_kernel(a_ref, b_ref, s_ref, o_ref):
    acc = jnp.dot(a_ref[...], b_ref[...], preferred_element_type=F32)
    o_ref[...] = (acc * s_ref[...]).astype(o_ref.dtype)


def _mm_sigmoid_kernel(a_ref, b_ref, o_ref):
    acc = jnp.dot(a_ref[...], b_ref[...], preferred_element_type=F32)
    o_ref[...] = jax.nn.sigmoid(acc).astype(o_ref.dtype)


def _mm_plain_kernel(a_ref, b_ref, o_ref):
    o_ref[...] = jnp.dot(a_ref[...], b_ref[...], preferred_element_type=F32).astype(o_ref.dtype)


def _mm_residual_kernel(a_ref, b_ref, x_ref, o_ref):
    o_ref[...] = x_ref[...] + jnp.dot(a_ref[...], b_ref[...], preferred_element_type=F32)


def _matmul(body, a, b, extra, extra_specs, out_dtype, bm, bn, name):
    m, k = a.shape
    n = b.shape[1]
    blocks = (_nbytes((bm, k), a.dtype) + _nbytes((k, bn), b.dtype) + _nbytes((bm, bn), out_dtype)
              + sum(_nbytes(s.block_shape, e.dtype) for s, e in zip(extra_specs, extra)))
    return pl.pallas_call(
        body,
        out_shape=jax.ShapeDtypeStruct((m, n), out_dtype),
        grid=(m // bm, n // bn),
        in_specs=[pl.BlockSpec((bm, k), lambda i, j: (i, 0)),
                  pl.BlockSpec((k, bn), lambda i, j: (0, j))] + list(extra_specs),
        out_specs=pl.BlockSpec((bm, bn), lambda i, j: (i, j)),
        compiler_params=_params(("parallel", "parallel"), blocks, _nbytes((bm, bn), F32)),
        name=name,
    )(a, b, *extra)


def _bias_kernel(tab_ref, o_ref, *, blk, window, rebase):
    h = pl.program_id(0)
    r = lax.broadcasted_iota(jnp.int32, (blk, blk), 0)
    c = lax.broadcasted_iota(jnp.int32, (blk, blk), 1)
    base = tab_ref[N_BUCKETS - 1, h] if rebase else 0.0
    for delta in (0, 1):
        rel = r - c + delta * blk
        n = jnp.maximum(rel, 0)
        nf = jnp.maximum(n, 1).astype(F32)
        large = MAX_EXACT + (jnp.log(nf / MAX_EXACT) / math.log(MAX_DIST / MAX_EXACT)
                             * (N_BUCKETS - MAX_EXACT)).astype(jnp.int32)
        large = jnp.minimum(large, N_BUCKETS - 1)
        bucket = jnp.where(n < MAX_EXACT, n, large)
        bias = jnp.zeros((blk, blk), F32)
        for b in range(N_BUCKETS):
            bias = jnp.where(bucket == b, tab_ref[b, h] - base, bias)
        mask = rel >= 0
        if window is not None:
            mask = mask & (rel < window)
        o_ref[0, delta] = jnp.where(mask, bias, NEG)


def _bias_tiles(tab, blk, window, rebase):
    heads = tab.shape[1]
    return pl.pallas_call(
        functools.partial(_bias_kernel, blk=blk, window=window, rebase=rebase),
        out_shape=jax.ShapeDtypeStruct((heads, 2, blk, blk), F32),
        grid=(heads,),
        in_specs=[pl.BlockSpec(memory_space=pltpu.SMEM)],
        out_specs=pl.BlockSpec((1, 2, blk, blk), lambda h: (h, 0, 0, 0)),
        compiler_params=_params(("parallel",), _nbytes((2, blk, blk), F32), 4 * _nbytes((blk, blk), F32)),
        name="bias_tiles",
    )(tab)


def _da_kernel(q_ref, k_ref, v_ref, bias_ref, lam_ref, g_ref, o_ref, *, lam_init):
    blk = q_ref.shape[0]
    qi = pl.program_id(2)
    lp = lam_ref[...]
    lam = (jnp.exp(jnp.sum(lp[0:1] * lp[1:2], axis=-1, keepdims=True))
           - jnp.exp(jnp.sum(lp[2:3] * lp[3:4], axis=-1, keepdims=True)) + lam_init)

    def softmax_pv(j, case):
        dims = slice(j * HEAD_DIM, (j + 1) * HEAD_DIM)
        q = q_ref[:, dims]
        spans = [(slice(case * blk, (case + 1) * blk), bias_ref[0, 0])]
        if case >= 1:
            spans.append((slice((case - 1) * blk, case * blk), bias_ref[0, 1]))
        if case >= 2:
            spans.append((slice(0, (case - 1) * blk), None))
        scores = []
        for rows, bias in spans:
            s = _dot_nt(q, k_ref[rows, dims])
            scores.append(s if bias is None else s + bias)
        m = functools.reduce(jnp.maximum, [jnp.max(s, axis=-1, keepdims=True) for s in scores])
        probs = [jnp.exp(s - m) for s in scores]
        norm = sum(jnp.sum(p, axis=-1, keepdims=True) for p in probs)
        out = sum(jnp.dot(p.astype(BF16), v_ref[rows, :], preferred_element_type=F32)
                  for p, (rows, _) in zip(probs, spans))
        return out / norm

    for case in range(k_ref.shape[0] // blk):
        @pl.when(qi == case)
        def _(case=case):
            o = softmax_pv(0, case) - lam * softmax_pv(1, case)
            y = o * lax.rsqrt(jnp.mean(o * o, axis=-1, keepdims=True) + EPS)
            o_ref[...] = ((y * g_ref[...]) * (1.0 - lam_init)).astype(o_ref.dtype)


def _diff_attention(qkv, bias, lam_p, subln_g, lam_init, batch, seq):
    blk = DA_BLK
    nq = seq // blk
    blocks = (2 * _nbytes((blk, DA_V_DIM), BF16) + 2 * _nbytes((seq, DA_V_DIM), BF16)
              + _nbytes((2, blk, blk), F32))
    scratch = 6 * _nbytes((blk, seq), F32)
    k_col0 = DA_QK // DA_V_DIM
    v_col0 = 2 * DA_QK // DA_V_DIM
    return pl.pallas_call(
        functools.partial(_da_kernel, lam_init=lam_init),
        out_shape=jax.ShapeDtypeStruct((batch * seq, DA_V), BF16),
        grid=(batch, DA_HEADS, nq),
        in_specs=[pl.BlockSpec((blk, DA_V_DIM), lambda b, h, i: (b * nq + i, h)),
                  pl.BlockSpec((seq, DA_V_DIM), lambda b, h, i: (b, k_col0 + h)),
                  pl.BlockSpec((seq, DA_V_DIM), lambda b, h, i: (b, v_col0 + h)),
                  pl.BlockSpec((1, 2, blk, blk), lambda b, h, i: (h, 0, 0, 0)),
                  pl.BlockSpec((4, HEAD_DIM), lambda b, h, i: (0, 0)),
                  pl.BlockSpec((1, DA_V_DIM), lambda b, h, i: (0, 0))],
        out_specs=pl.BlockSpec((blk, DA_V_DIM), lambda b, h, i: (b * nq + i, h)),
        compiler_params=_params(("parallel", "parallel", "parallel"), blocks, scratch),
        name="diff_attention",
    )(qkv, qkv, qkv, bias, lam_p, subln_g.reshape(1, DA_V_DIM))


def _swa_kernel(q_ref, kc_ref, kp_ref, vc_ref, vp_ref, bias_ref, sink_ref, o_ref):
    n = pl.program_id(1)
    is_prev = lax.broadcasted_iota(jnp.int32, (1, 2 * WINDOW), 1) < WINDOW
    no_prev = jnp.where(is_prev & (n == 0), NEG, 0.0).astype(F32)
    for hk in range(SW_KV_HEADS):
        cols = slice(hk * HEAD_DIM, (hk + 1) * HEAD_DIM)
        heads = [slice((hk * SW_GROUP + g) * HEAD_DIM, (hk * SW_GROUP + g + 1) * HEAD_DIM) for g in range(SW_GROUP)]
        q = jnp.concatenate([q_ref[:, hd] for hd in heads], axis=0)
        k = jnp.concatenate([kp_ref[:, cols], kc_ref[:, cols]], axis=0)
        v = jnp.concatenate([vp_ref[:, cols], vc_ref[:, cols]], axis=0)
        s = _dot_nt(q, k) + bias_ref[hk] + no_prev
        sink = sink_ref[hk]
        m = jnp.maximum(jnp.max(s, axis=-1, keepdims=True), sink)
        e = jnp.exp(s - m)
        den = jnp.sum(e, axis=-1, keepdims=True) + jnp.exp(sink - m)
        o = jnp.dot(e.astype(BF16), v, preferred_element_type=F32) / den
        for g, hd in enumerate(heads):
            o_ref[:, hd] = o[g * WINDOW:(g + 1) * WINDOW, :].astype(o_ref.dtype)


def _sliding_attention(qkv, bias_tiles, sinks, batch, seq):
    nb = seq // WINDOW
    q_col = 3 * DA_QK // SW_Q
    k_col = (3 * DA_QK + SW_Q) // SW_KV
    v_col = k_col + 1
    cur = lambda b, n: b * nb + n
    prev = lambda b, n: b * nb + jnp.maximum(n - 1, 0)
    rows = SW_GROUP * WINDOW
    bias = bias_tiles.reshape(SW_KV_HEADS, SW_GROUP, 2, WINDOW, WINDOW)[:, :, ::-1]
    bias = bias.transpose(0, 1, 3, 2, 4).reshape(SW_KV_HEADS, rows, 2 * WINDOW)
    sink_cols = jnp.repeat(sinks.astype(F32).reshape(SW_KV_HEADS, SW_GROUP), WINDOW, axis=1).reshape(
        SW_KV_HEADS, rows, 1)
    blocks = (2 * _nbytes((WINDOW, SW_Q), BF16) + 4 * _nbytes((WINDOW, SW_KV), BF16)
              + _nbytes((SW_KV_HEADS, rows, 2 * WINDOW), F32) + _nbytes((SW_KV_HEADS, rows, 128), F32))
    return pl.pallas_call(
        _swa_kernel,
        out_shape=jax.ShapeDtypeStruct((batch * seq, SW_Q), BF16),
        grid=(batch, nb),
        in_specs=[pl.BlockSpec((WINDOW, SW_Q), lambda b, n: (cur(b, n), q_col)),
                  pl.BlockSpec((WINDOW, SW_KV), lambda b, n: (cur(b, n), k_col)),
                  pl.BlockSpec((WINDOW, SW_KV), lambda b, n: (prev(b, n), k_col)),
                  pl.BlockSpec((WINDOW, SW_KV), lambda b, n: (cur(b, n), v_col)),
                  pl.BlockSpec((WINDOW, SW_KV), lambda b, n: (prev(b, n), v_col)),
                  pl.BlockSpec((SW_KV_HEADS, rows, 2 * WINDOW), lambda b, n: (0, 0, 0)),
                  pl.BlockSpec((SW_KV_HEADS, rows, 1), lambda b, n: (0, 0, 0))],
        out_specs=pl.BlockSpec((WINDOW, SW_Q), lambda b, n: (cur(b, n), 0)),
        compiler_params=_params(("parallel", "parallel"), blocks, 16 * _nbytes((rows, 2 * WINDOW), F32)),
        name="sliding_attention",
    )(qkv, qkv, qkv, qkv, qkv, bias, sink_cols)


def _merge_kernel(oa_ref, ob_ref, wa_ref, wb_ref, ga_ref, gb_ref, o_ref):
    a = jnp.dot(oa_ref[...], wa_ref[...], preferred_element_type=F32)
    b = jnp.dot(ob_ref[...], wb_ref[...], preferred_element_type=F32)
    o_ref[...] = (ga_ref[...].astype(F32) * a + gb_ref[...].astype(F32) * b).astype(o_ref.dtype)


def _merge(o_a, o_b, w_a, w_b, gates, bm=1024, bn=1024):
    t, ka = o_a.shape
    kb = o_b.shape[1]
    d = w_a.shape[1]
    nj = d // bn
    blocks = (_nbytes((bm, ka), BF16) + _nbytes((bm, kb), BF16) + _nbytes((ka, bn), BF16)
              + _nbytes((kb, bn), BF16) + 2 * _nbytes((bm, bn), gates.dtype) + _nbytes((bm, bn), BF16))
    return pl.pallas_call(
        _merge_kernel,
        out_shape=jax.ShapeDtypeStruct((t, d), BF16),
        grid=(t // bm, nj),
        in_specs=[pl.BlockSpec((bm, ka), lambda i, j: (i, 0)),
                  pl.BlockSpec((bm, kb), lambda i, j: (i, 0)),
                  pl.BlockSpec((ka, bn), lambda i, j: (0, j)),
                  pl.BlockSpec((kb, bn), lambda i, j: (0, j)),
                  pl.BlockSpec((bm, bn), lambda i, j: (i, j)),
                  pl.BlockSpec((bm, bn), lambda i, j: (i, nj + j))],
        out_specs=pl.BlockSpec((bm, bn), lambda i, j: (i, j)),
        compiler_params=_params(("parallel", "parallel"), blocks, 2 * _nbytes((bm, bn), F32)),
        name="merge",
    )(o_a, o_b, w_a, w_b, gates, gates)


def _sort_pairs(n):
    pairs = []

    def merge(lo, hi, r):
        step = r * 2
        if step < hi - lo:
            merge(lo, hi, step)
            merge(lo + r, hi, step)
            pairs.extend((i, i + r) for i in range(lo + r, hi - r, step))
        else:
            pairs.append((lo, lo + r))

    def sort(lo, hi):
        if hi - lo >= 1:
            mid = lo + (hi - lo) // 2
            sort(lo, mid)
            sort(mid + 1, hi)
            merge(lo, hi, 1)

    sort(0, n - 1)
    return pairs


_SORT16 = _sort_pairs(PEER_TOPK)


def _sort_desc(xs):
    xs = list(xs)
    for i, j in _SORT16:
        xs[i], xs[j] = jnp.maximum(xs[i], xs[j]), jnp.minimum(xs[i], xs[j])
    return xs


def _merge_top(a, b):
    k = PEER_TOPK
    xs = [jnp.maximum(a[i], b[k - 1 - i]) for i in range(k)]
    d = k // 2
    while d >= 1:
        for i in range(k):
            if not i & d:
                xs[i], xs[i + d] = jnp.maximum(xs[i], xs[i + d]), jnp.minimum(xs[i], xs[i + d])
        d //= 2
    return xs


def _top16_over_rows(s):
    groups = [s[a * 8:(a + 1) * 8, :] for a in range(s.shape[0] // 8)]
    xs = _sort_desc(groups)
    for shift in (4, 2, 1):
        xs = _merge_top(xs, [pltpu.roll(x, shift, 0) for x in xs])
    return xs


def _count_leading(pred, values):
    n = len(values)

    def pick(lo, hi, taken):
        if not taken:
            return values[(lo + hi) // 2 - 1]
        mid = (lo + hi) // 2
        return jnp.where(taken[0], pick(mid, hi, taken[1:]), pick(lo, mid, taken[1:]))

    taken = []
    step = n // 2
    while step >= 1:
        taken.append(pred(pick(0, n, taken)))
        step //= 2
    total = None
    for i, t in enumerate(taken):
        part = jnp.where(t, float(n >> (i + 1)), 0.0)
        total = part if total is None else total + part
    return total + jnp.where(pred(values[n - 1]), 1.0, 0.0)


def _route_kernel(q_ref, k1_ref, k2_ref, cnt_ref, e1_ref, rank_ref, e2_ref, top_ref):
    q = q_ref[...]
    s1 = _dot_nt(k1_ref[...], q[:, :KEY_DIM])
    s2 = _dot_nt(k2_ref[...], q[:, KEY_DIM:])
    v1 = _top16_over_rows(s1)
    v2 = _top16_over_rows(s2)
    k = PEER_TOPK
    top = [v1[0] + v2[b] for b in range(k)]
    rest = [v1[a] + v2[b] for a in range(1, k) for b in range(k) if (a + 1) * (b + 1) <= k]
    pad = jnp.full(top[0].shape, -jnp.inf, F32)
    rest = rest + [pad] * (-len(rest) % k)
    for g in range(len(rest) // k):
        top = _merge_top(top, _sort_desc(rest[g * k:(g + 1) * k]))
    z = jnp.ones_like(top[0])
    for c in top[1:]:
        z = z + jnp.exp(c - top[0])
    tau = top[k - 1][0:1]
    best = [v[0:1] for v in v2]
    cnt = _count_leading(lambda b: s1 + b >= tau, best)
    rank = _count_leading(lambda b: b > s2, best)
    cnt_ref[0] = cnt
    rank_ref[0] = _pack_rows(rank.astype(BF16))
    e1_ref[0] = jnp.exp(s1 - v1[0][0:1]) / z[0:1]
    e2_ref[0] = _pack_rows(jnp.exp(s2 - v2[0][0:1]).astype(BF16))
    top_ref[0] = 1.0 / z[0:1]


def _route(q, k1, k2, tm=1024):
    t = q.shape[0]
    wide = jax.ShapeDtypeStruct((PEER_HEADS, N_KEYS, t), F32)
    narrow = jax.ShapeDtypeStruct((PEER_HEADS, N_KEYS // 2, t), jnp.uint32)
    spec = pl.BlockSpec((1, N_KEYS, tm), lambda i, h: (h, 0, i))
    narrow_spec = pl.BlockSpec((1, N_KEYS // 2, tm), lambda i, h: (h, 0, i))
    blocks = _nbytes((tm, 2 * KEY_DIM), BF16) + 3 * _nbytes((N_KEYS, tm), F32)
    return pl.pallas_call(
        _route_kernel,
        out_shape=(wide, wide, narrow, narrow, jax.ShapeDtypeStruct((PEER_HEADS, 1, t), F32)),
        grid=(t // tm, PEER_HEADS),
        in_specs=[pl.BlockSpec((tm, 2 * KEY_DIM), lambda i, h: (i, h)),
                  pl.BlockSpec((N_KEYS, KEY_DIM), lambda i, h: (0, 0)),
                  pl.BlockSpec((N_KEYS, KEY_DIM), lambda i, h: (0, 0))],
        out_specs=(spec, spec, narrow_spec, narrow_spec, pl.BlockSpec((1, 1, tm), lambda i, h: (h, 0, i))),
        compiler_params=_params(("parallel", "parallel"), blocks, 24 * _nbytes((N_KEYS, tm), F32)),
        name="peer_route",
    )(q, k1, k2)


def _peer_up_kernel(inv_u_ref, inv_v_ref, u_ref, h_ref, inv_ref, wscale_ref, cnt_ref, e1_ref, rank_ref, e2_ref,
                    o_ref, act_a, act_b, *, expert_tiles):
    s = pl.program_id(0)
    first_scale_block = (jnp.maximum(s - 1, 0) % expert_tiles) * (act_a.shape[0] // FP8_ROW_BLOCK)

    @pl.when(s == 0)
    def _():
        act_a[...] = jnp.zeros(act_a.shape, F32)
        act_b[...] = jnp.zeros(act_b.shape, F32)

    packed = (N_KEYS // BF16_ROWS, BF16_ROWS, 128)
    chunks = u_ref.shape[0]
    blocks_per_chunk = act_a.shape[0] // N_KEYS // chunks

    def epilogue_block(act_old, r):
        rows = pl.ds(pl.multiple_of(r * N_KEYS, N_KEYS), N_KEYS)
        out_rows = pl.ds(pl.multiple_of(r * (N_KEYS // 4), N_KEYS // 4), N_KEYS // 4)
        scale_block = first_scale_block + r // (FP8_ROW_BLOCK // N_KEYS)
        inv_u, inv_v = inv_u_ref[scale_block], inv_v_ref[scale_block]
        cnt_rows = [cnt_ref[h, pl.ds(r, 1), :] for h in range(PEER_HEADS)]
        e1_rows = [e1_ref[h, pl.ds(r, 1), :] * (wscale_ref[...] * inv_v) for h in range(PEER_HEADS)]
        for c in range(act_old.shape[1] // 128):
            cols = slice(c * 128, (c + 1) * 128)
            act = act_old[rows, cols]
            act_old[rows, cols] = jnp.zeros_like(act)
            act = act * (inv_ref[:, cols] * inv_u)
            gate = None
            for h in range(PEER_HEADS):
                cnt = jnp.tile(jnp.broadcast_to(cnt_rows[h][:, cols], packed[1:]).astype(BF16), (packed[0], 1))
                e1 = jnp.tile(jnp.broadcast_to(e1_rows[h][:, cols], packed[1:]).astype(BF16), (packed[0], 1))
                routed = _unpack_rows(rank_ref[h, :, cols]) < cnt
                term = jnp.where(routed, _unpack_rows(e2_ref[h, :, cols]) * e1, jnp.zeros((), BF16))
                gate = term if gate is None else gate + term
            w = jax.nn.gelu(act).astype(BF16) * gate
            o_ref[out_rows, cols] = _pack_rows(w.astype(FP8))

    def step(act_new, act_old):
        def chunk(k, carry):
            act_new[...] += _dot_nt(u_ref[k], h_ref[k])
            for b in range(blocks_per_chunk):
                epilogue_block(act_old, k * blocks_per_chunk + b)
            return carry

        lax.fori_loop(0, chunks, chunk, 0, unroll=PEER_UP_UNROLL)

    @pl.when(s % 2 == 0)
    def _():
        step(act_a, act_b)

    @pl.when(s % 2 == 1)
    def _():
        step(act_b, act_a)


def _peer_up(u, inv_u, inv_v, hn, inv_h, w_scale, cnt, e1, rank, e2, te=PEER_UP_TE, tm=512):
    chunks, n_exp, dc = u.shape
    t = hn.shape[1]
    rows = te // N_KEYS
    assert rows % chunks == 0
    nj = n_exp // te
    steps = (t // tm) * nj

    def tile_of(step):
        return step // nj, step % nj

    def now(s):
        return tile_of(jnp.minimum(s, steps - 1))

    def lag(s):
        return tile_of(jnp.maximum(s - 1, 0))

    row_spec = pl.BlockSpec((PEER_HEADS, rows, tm), lambda s: (0, lag(s)[1], lag(s)[0]))
    full_spec = pl.BlockSpec((PEER_HEADS, N_KEYS // 2, tm), lambda s: (0, 0, lag(s)[0]))
    blocks = (_nbytes((chunks, te, dc), FP8) + _nbytes((chunks, tm, dc), FP8)
              + 2 * _nbytes((PEER_HEADS, N_KEYS, tm), BF16)
              + 2 * _nbytes((PEER_HEADS, rows, tm), F32) + _nbytes((te, tm), BF16))
    smem = pl.BlockSpec(memory_space=pltpu.SMEM)
    return pl.pallas_call(
        functools.partial(_peer_up_kernel, expert_tiles=nj),
        out_shape=jax.ShapeDtypeStruct((n_exp // 4, t), jnp.uint32),
        grid=(steps + 1,),
        in_specs=[smem, smem,
                  pl.BlockSpec((chunks, te, dc), lambda s: (0, now(s)[1], 0)),
                  pl.BlockSpec((chunks, tm, dc), lambda s: (0, now(s)[0], 0)),
                  pl.BlockSpec((1, tm), lambda s: (0, lag(s)[0])),
                  pl.BlockSpec((1, tm), lambda s: (0, lag(s)[0])),
                  row_spec, row_spec, full_spec, full_spec],
        out_specs=pl.BlockSpec((te // 4, tm), lambda s: (lag(s)[1], lag(s)[0])),
        scratch_shapes=[pltpu.VMEM((te, tm), F32), pltpu.VMEM((te, tm), F32)],
        compiler_params=_params(("arbitrary",), blocks, 3 * _nbytes((te, tm), F32)),
        name="peer_up",
    )(inv_u, inv_v, u, hn, inv_h, w_scale, cnt, e1, rank, e2)


def _peer_down_kernel(vt_ref, w_ref, inv_ref, x_ref, o_ref, acc_ref):
    kk = pl.program_id(2)

    @pl.when(kk == 0)
    def _():
        acc_ref[...] = jnp.zeros(acc_ref.shape, F32)

    acc_ref[...] += jnp.dot(vt_ref[...], _unpack_rows(w_ref[...], FP8), preferred_element_type=F32)

    @pl.when(kk == pl.num_programs(2) - 1)
    def _():
        o_ref[...] = x_ref[...] + (acc_ref[...] * inv_ref[...]).T


def _peer_down(vt, wt, inv_w_scale, x, bd=1024, bt=1024, tk=4096):
    d, n_exp = vt.shape
    t = wt.shape[1]
    blocks = (_nbytes((bd, tk), FP8) + _nbytes((tk, bt), FP8) + 2 * _nbytes((bt, bd), F32))
    return pl.pallas_call(
        _peer_down_kernel,
        out_shape=jax.ShapeDtypeStruct((t, d), F32),
        grid=(d // bd, t // bt, n_exp // tk),
        in_specs=[pl.BlockSpec((bd, tk), lambda i, j, k: (i, k)),
                  pl.BlockSpec((tk // 4, bt), lambda i, j, k: (k, j)),
                  pl.BlockSpec((1, bt), lambda i, j, k: (0, j)),
                  pl.BlockSpec((bt, bd), lambda i, j, k: (j, i))],
        out_specs=pl.BlockSpec((bt, bd), lambda i, j, k: (j, i)),
        scratch_shapes=[pltpu.VMEM((bd, bt), F32)],
        compiler_params=_params(("parallel", "parallel", "arbitrary"), blocks, 3 * _nbytes((bd, bt), F32)),
        name="peer_down",
    )(vt, wt, inv_w_scale, x)


def _w_scale(h_norm, u_norm, top_weight, inv_v):
    bound = h_norm.reshape(1, -1) * u_norm * jnp.sum(top_weight, axis=0) * jnp.max(inv_v)
    return _pow2_scale(bound)


def _ple_kernel(h_ref, wg_ref, p_ref, wp_ref, x_ref, o_ref):
    gate = jax.nn.sigmoid(jnp.dot(h_ref[...], wg_ref[...], preferred_element_type=F32))
    emb = jnp.dot(p_ref[...], wp_ref[...], preferred_element_type=F32)
    o_ref[...] = x_ref[...] + gate * emb


def _ple(hp, w_gate, p, w_proj, x, bm=1024, bn=512):
    t, d = hp.shape
    pd = p.shape[1]
    n = w_gate.shape[1]
    blocks = (_nbytes((bm, d), BF16) + _nbytes((d, bn), BF16) + _nbytes((bm, pd), BF16)
              + _nbytes((pd, bn), BF16) + 2 * _nbytes((bm, bn), F32))
    return pl.pallas_call(
        _ple_kernel,
        out_shape=jax.ShapeDtypeStruct((t, n), F32),
        grid=(t // bm, n // bn),
        in_specs=[pl.BlockSpec((bm, d), lambda i, j: (i, 0)),
                  pl.BlockSpec((d, bn), lambda i, j: (0, j)),
                  pl.BlockSpec((bm, pd), lambda i, j: (i, 0)),
                  pl.BlockSpec((pd, bn), lambda i, j: (0, j)),
                  pl.BlockSpec((bm, bn), lambda i, j: (i, j))],
        out_specs=pl.BlockSpec((bm, bn), lambda i, j: (i, j)),
        compiler_params=_params(("parallel", "parallel"), blocks, 2 * _nbytes((bm, bn), F32)),
        name="ple",
    )(hp, w_gate, p, w_proj, x)


def _qkv_column_scale():
    s = HEAD_DIM ** -0.5
    parts = [(DA_QK, s), (DA_QK, 1.0), (DA_V, 1.0), (SW_Q, s), (SW_KV, 1.0), (SW_KV, 1.0)]
    return jnp.concatenate([jnp.full((1, w), v, F32) for w, v in parts], axis=1)


@jax.jit
def kernel(x, p, positions, rel_bias, norm_mix, w_in, da_lambda, da_subln, sw_sinks, w_br_a, w_br_b, w_out,
           norm_ffn, peer_wq, peer_k1, peer_k2, peer_u, peer_v, norm_ple, ple_gate, ple_proj, norm_final):
    del positions
    batch, seq, d = x.shape
    t = batch * seq
    depth = w_in.shape[0]
    xf = x.reshape(t, d)
    da_bias = _bias_tiles(rel_bias[:, :DA_HEADS], DA_BLK, None, True)
    sw_bias = _bias_tiles(rel_bias[:, DA_HEADS:], WINDOW, WINDOW, False)
    col_scale = _qkv_column_scale()
    tile = pl.BlockSpec((1, 1024), lambda i, j: (0, j))
    for i in range(depth):
        lam_init = 0.8 - 0.6 * math.exp(-0.3 * i)
        h = _rmsnorm(xf, norm_mix[i], BF16)
        qkv = _matmul(_mm_scale_kernel, h, _narrow(w_in, i, 0, QKV_WIDTH, 1024), [col_scale], [tile],
                      BF16, 1024, 1024, "proj_qkv")
        gates = _matmul(_mm_sigmoid_kernel, h, _narrow(w_in, i, QKV_WIDTH, w_in.shape[2] - QKV_WIDTH, 1024),
                        [], [], BF16, 1024, 1024, "proj_gates")
        o_a = _diff_attention(qkv, da_bias, da_lambda[i], da_subln[i], lam_init, batch, seq)
        o_b = _sliding_attention(qkv, sw_bias, sw_sinks[i], batch, seq)
        merged = _merge(o_a, o_b, _narrow(w_br_a, i), _narrow(w_br_b, i), gates)
        xf = _matmul(_mm_residual_kernel, merged, _narrow(w_out, i), [xf],
                     [pl.BlockSpec((1024, 512), lambda i, j: (i, j))], F32, 1024, 512, "proj_out")
        hn, hn8, hn_inv, hn_norm = _rmsnorm_fp8(xf, norm_ffn[i], PEER_UP_CHUNKS)
        q = _matmul(_mm_plain_kernel, hn, _narrow(peer_wq, i), [], [], BF16, 1024, 1024, "peer_query")
        cnt, e1, rank, e2, top = _route(q, peer_k1[i].astype(BF16), peer_k2[i].astype(BF16))
        u8, inv_u, u_norm = _fp8_rows(peer_u, i, "chunked", PEER_UP_CHUNKS)
        vt8, inv_v, _ = _fp8_rows(peer_v, i, "transposed")
        w_scale, inv_w_scale = _w_scale(hn_norm, u_norm, top, inv_v)
        wt = _peer_up(u8, inv_u, inv_v, hn8, hn_inv.reshape(1, t), w_scale, cnt, e1, rank, e2)
        xf = _peer_down(vt8, wt, inv_w_scale, xf)
        hp = _rmsnorm(xf, norm_ple[i], BF16)
        xf = _ple(hp, _narrow(ple_gate, i), p[i].reshape(t, -1).astype(BF16), ple_proj[i].astype(BF16), xf)
    return _rmsnorm(xf, norm_final, F32).reshape(batch, seq, d)
```

```python
import functools
import math

import jax
import jax.numpy as jnp
from jax import lax
from jax.experimental import pallas as pl
from jax.experimental.pallas import tpu as pltpu

F32 = jnp.float32
BF16 = jnp.bfloat16
FP8 = jnp.float8_e4m3fn
FP8_TARGET = 240.0
FP8_TINY = 1e-30

HEAD_DIM = 128
DA_HEADS = 8
DA_V_DIM = 2 * HEAD_DIM
SW_Q_HEADS = 16
SW_KV_HEADS = 4
SW_GROUP = SW_Q_HEADS // SW_KV_HEADS
WINDOW = 128
N_BUCKETS = 32
MAX_EXACT = N_BUCKETS // 2
MAX_DIST = 128
NEG = -1e30
DA_QK = DA_HEADS * 2 * HEAD_DIM
DA_V = DA_HEADS * DA_V_DIM
SW_Q = SW_Q_HEADS * HEAD_DIM
SW_KV = SW_KV_HEADS * HEAD_DIM
QKV_WIDTH = 3 * DA_QK + SW_Q + 2 * SW_KV
PEER_HEADS = 8
N_KEYS = 128
PEER_TOPK = 16
KEY_DIM = 128
EPS = 1e-6

V7X_VMEM_REQUEST_CAP = 60 * 1024 * 1024
BF16_ROWS = 16
DA_BLK = 512
SW_BLOCKS_PER_STEP = 4
PEER_UP_TE = 1024
FP8_ROW_BLOCK = 512
PEER_UP_CHUNKS = 4
PEER_UP_UNROLL = 2


def _nbytes(shape, dtype):
    return math.prod(shape) * jnp.dtype(dtype).itemsize


def _params(semantics, block_bytes, scratch_bytes=0, flags=None):
    need = int(1.25 * (2 * block_bytes + scratch_bytes)) + (4 << 20)
    return pltpu.CompilerParams(dimension_semantics=semantics,
                                vmem_limit_bytes=min(need, V7X_VMEM_REQUEST_CAP), flags=flags)


def _pack_rows(x):
    return pltpu.bitcast(x, jnp.uint32)


def _unpack_rows(x, dtype=BF16):
    return pltpu.bitcast(x, dtype)


def _dot_nt(a, b):
    return lax.dot_general(a, b, (((1,), (1,)), ((), ())), preferred_element_type=F32)


def _rmsnorm_kernel(x_ref, g_ref, o_ref):
    x = x_ref[...]
    y = x * lax.rsqrt(jnp.mean(x * x, axis=-1, keepdims=True) + EPS)
    o_ref[...] = (y * g_ref[...]).astype(o_ref.dtype)


def _rmsnorm(x, g, out_dtype, rows=256):
    t, d = x.shape
    blocks = _nbytes((rows, d), F32) + _nbytes((rows, d), out_dtype)
    return pl.pallas_call(
        _rmsnorm_kernel,
        out_shape=jax.ShapeDtypeStruct((t, d), out_dtype),
        grid=(t // rows,),
        in_specs=[pl.BlockSpec((rows, d), lambda i: (i, 0)),
                  pl.BlockSpec((1, d), lambda i: (0, 0))],
        out_specs=pl.BlockSpec((rows, d), lambda i: (i, 0)),
        compiler_params=_params(("parallel",), blocks, _nbytes((rows, d), F32)),
        name="rmsnorm",
    )(x, g.reshape(1, d))


def _rmsnorm_fp8_kernel(x_ref, g_ref, o_ref, oc_ref, inv_ref, norm_ref):
    x = x_ref[...]
    y = x * lax.rsqrt(jnp.mean(x * x, axis=-1, keepdims=True) + EPS) * g_ref[...]
    o_ref[...] = y.astype(o_ref.dtype)
    norm_ref[...] = jnp.sqrt(jnp.sum(y * y, axis=-1, keepdims=True))
    amax = jnp.maximum(jnp.max(jnp.abs(y), axis=-1, keepdims=True), FP8_TINY)
    inv_ref[...] = amax * (1.0 / FP8_TARGET)
    y8 = y * (FP8_TARGET / amax)
    dc = oc_ref.shape[2]
    for k in range(oc_ref.shape[0]):
        oc_ref[k] = y8[:, k * dc:(k + 1) * dc].astype(oc_ref.dtype)


def _rmsnorm_fp8(x, g, chunks, rows=256):
    t, d = x.shape
    dc = d // chunks
    blocks = _nbytes((rows, d), F32) + _nbytes((rows, d), BF16) + _nbytes((rows, d), FP8) + _nbytes((rows, 128), F32)
    return pl.pallas_call(
        _rmsnorm_fp8_kernel,
        out_shape=(jax.ShapeDtypeStruct((t, d), BF16), jax.ShapeDtypeStruct((chunks, t, dc), FP8),
                   jax.ShapeDtypeStruct((t, 1), F32), jax.ShapeDtypeStruct((t, 1), F32)),
        grid=(t // rows,),
        in_specs=[pl.BlockSpec((rows, d), lambda i: (i, 0)),
                  pl.BlockSpec((1, d), lambda i: (0, 0))],
        out_specs=(pl.BlockSpec((rows, d), lambda i: (i, 0)),
                   pl.BlockSpec((chunks, rows, dc), lambda i: (0, i, 0)),
                   pl.BlockSpec((rows, 1), lambda i: (i, 0)),
                   pl.BlockSpec((rows, 1), lambda i: (i, 0))),
        compiler_params=_params(("parallel",), blocks, 2 * _nbytes((rows, d), F32)),
        name="rmsnorm_fp8",
    )(x, g.reshape(1, d))


CAST_BLOCK_BYTES = 8 << 20


def _cast_kernel(x_ref, o_ref):
    o_ref[...] = x_ref[...].astype(o_ref.dtype)


def _narrow(w, layer, col0=0, ncols=None, bc=None):
    _, r, c = w.shape
    ncols = c if ncols is None else ncols
    bc = ncols if bc is None else bc
    br = min(r, CAST_BLOCK_BYTES // (bc * 4))
    return pl.pallas_call(
        _cast_kernel,
        out_shape=jax.ShapeDtypeStruct((r, ncols), BF16),
        grid=(r // br, ncols // bc),
        in_specs=[pl.BlockSpec((None, br, bc), lambda i, j: (layer, i, col0 // bc + j))],
        out_specs=pl.BlockSpec((br, bc), lambda i, j: (i, j)),
        compiler_params=_params(("parallel", "parallel"), _nbytes((br, bc), F32) + _nbytes((br, bc), BF16),
                                _nbytes((br, bc), F32)),
        name="narrow",
    )(w)


def _pow2_scale(amax):
    shift = jnp.floor(jnp.log2(FP8_TARGET / jnp.maximum(amax, FP8_TINY)))
    return jnp.exp2(shift), jnp.exp2(-shift)


def _fp8_rows_kernel(x_ref, o_ref, inv_ref, norm_ref, *, layout):
    x = x_ref[...]
    norm = jnp.sqrt(jnp.max(jnp.sum(x * x, axis=1, keepdims=True), axis=0, keepdims=True))
    norm_ref[...] = jnp.broadcast_to(norm, norm_ref.shape)
    amax = jnp.max(jnp.max(jnp.abs(x), axis=0, keepdims=True), axis=1, keepdims=True)
    scale, inv = _pow2_scale(amax)
    inv_ref[...] = jnp.broadcast_to(inv, inv_ref.shape)
    y = x * scale
    if layout == "chunked":
        dc = o_ref.shape[2]
        for k in range(o_ref.shape[0]):
            o_ref[k] = y[:, k * dc:(k + 1) * dc].astype(o_ref.dtype)
    else:
        o_ref[...] = y.T.astype(o_ref.dtype)


def _fp8_rows(w, layer, layout, chunks=None, br=FP8_ROW_BLOCK):
    _, r, c = w.shape
    if layout == "chunked":
        shape = (chunks, r, c // chunks)
        out_spec = pl.BlockSpec((chunks, br, c // chunks), lambda i: (0, i, 0))
    else:
        shape = (c, r)
        out_spec = pl.BlockSpec((c, br), lambda i: (0, i))
    small = jax.ShapeDtypeStruct((r // br, 1, 128), F32)
    small_spec = pl.BlockSpec((1, 1, 128), lambda i: (i, 0, 0))
    out, inv, norm = pl.pallas_call(
        functools.partial(_fp8_rows_kernel, layout=layout),
        out_shape=(jax.ShapeDtypeStruct(shape, FP8), small, small),
        grid=(r // br,),
        in_specs=[pl.BlockSpec((None, br, c), lambda i: (layer, i, 0))],
        out_specs=(out_spec, small_spec, small_spec),
        compiler_params=_params(("parallel",), _nbytes((br, c), F32) + _nbytes((br, c), FP8), 2 * _nbytes((br, c), F32)),
        name="fp8_rows_" + layout,
    )(w)
    return out, inv[:, 0, 0], jnp.max(norm)


def _mm_scale_kernel(a_ref, b_ref, s_ref, o_ref):
    acc = jnp.dot(a_ref[...], b_ref[...], preferred_element_type=F32)
    o_ref[...] = (acc * s_ref[...]).astype(o_ref.dtype)


def _mm_sigmoid_kernel(a_ref, b_ref, o_ref):
    acc = jnp.dot(a_ref[...], b_ref[...], preferred_element_type=F32)
    o_ref[...] = jax.nn.sigmoid(acc).astype(o_ref.dtype)


def _mm_plain_kernel(a_ref, b_ref, o_ref):
    o_ref[...] = jnp.dot(a_ref[...], b_ref[...], preferred_element_type=F32).astype(o_ref.dtype)


def _mm_residual_kernel(a_ref, b_ref, x_ref, o_ref):
    o_ref[...] = x_ref[...] + jnp.dot(a_ref[...], b_ref[...], preferred_element_type=F32)


def _matmul(body, a, b, extra, extra_specs, out_dtype, bm, bn, name):
    m, k = a.shape
    n = b.shape[1]
    blocks = (_nbytes((bm, k), a.dtype) + _nbytes((k, bn), b.dtype) + _nbytes((bm, bn), out_dtype)
              + sum(_nbytes(s.block_shape, e.dtype) for s, e in zip(extra_specs, extra)))
    return pl.pallas_call(
        body,
        out_shape=jax.ShapeDtypeStruct((m, n), out_dtype),
        grid=(m // bm, n // bn),
        in_specs=[pl.BlockSpec((bm, k), lambda i, j: (i, 0)),
                  pl.BlockSpec((k, bn), lambda i, j: (0, j))] + list(extra_specs),
        out_specs=pl.BlockSpec((bm, bn), lambda i, j: (i, j)),
        compiler_params=_params(("parallel", "parallel"), blocks, _nbytes((bm, bn), F32)),
        name=name,
    )(a, b, *extra)


def _bias_kernel(tab_ref, o_ref, *, blk, window, rebase):
    h = pl.program_id(0)
    r = lax.broadcasted_iota(jnp.int32, (blk, blk), 0)
    c = lax.broadcasted_iota(jnp.int32, (blk, blk), 1)
    base = tab_ref[N_BUCKETS - 1, h] if rebase else 0.0
    for delta in (0, 1):
        rel = r - c + delta * blk
        n = jnp.maximum(rel, 0)
        nf = jnp.maximum(n, 1).astype(F32)
        large = MAX_EXACT + (jnp.log(nf / MAX_EXACT) / math.log(MAX_DIST / MAX_EXACT)
                             * (N_BUCKETS - MAX_EXACT)).astype(jnp.int32)
        large = jnp.minimum(large, N_BUCKETS - 1)
        bucket = jnp.where(n < MAX_EXACT, n, large)
        bias = jnp.zeros((blk, blk), F32)
        for b in range(N_BUCKETS):
            bias = jnp.where(bucket == b, tab_ref[b, h] - base, bias)
        mask = rel >= 0
        if window is not None:
            mask = mask & (rel < window)
        o_ref[0, delta] = jnp.where(mask, bias, NEG)


def _bias_tiles(tab, blk, window, rebase):
    heads = tab.shape[1]
    return pl.pallas_call(
        functools.partial(_bias_kernel, blk=blk, window=window, rebase=rebase),
        out_shape=jax.ShapeDtypeStruct((heads, 2, blk, blk), F32),
        grid=(heads,),
        in_specs=[pl.BlockSpec(memory_space=pltpu.SMEM)],
        out_specs=pl.BlockSpec((1, 2, blk, blk), lambda h: (h, 0, 0, 0)),
        compiler_params=_params(("parallel",), _nbytes((2, blk, blk), F32), 4 * _nbytes((blk, blk), F32)),
        name="bias_tiles",
    )(tab)


def _da_kernel(q_ref, k_ref, v_ref, bias_ref, lam_ref, g_ref, o_ref, *, lam_init):
    blk = q_ref.shape[0]
    qi = pl.program_id(2)
    lp = lam_ref[...]
    lam = (jnp.exp(jnp.sum(lp[0:1] * lp[1:2], axis=-1, keepdims=True))
           - jnp.exp(jnp.sum(lp[2:3] * lp[3:4], axis=-1, keepdims=True)) + lam_init)

    def softmax_pv(j, case):
        dims = slice(j * HEAD_DIM, (j + 1) * HEAD_DIM)
        q = q_ref[:, dims]
        spans = [(slice(case * blk, (case + 1) * blk), bias_ref[0, 0])]
        if case >= 1:
            spans.append((slice((case - 1) * blk, case * blk), bias_ref[0, 1]))
        if case >= 2:
            spans.append((slice(0, (case - 1) * blk), None))
        scores = []
        for rows, bias in spans:
            s = _dot_nt(q, k_ref[rows, dims])
            scores.append(s if bias is None else s + bias)
        m = functools.reduce(jnp.maximum, [jnp.max(s, axis=-1, keepdims=True) for s in scores])
        probs = [jnp.exp(s - m) for s in scores]
        norm = sum(jnp.sum(p, axis=-1, keepdims=True) for p in probs)
        out = sum(jnp.dot(p.astype(BF16), v_ref[rows, :], preferred_element_type=F32)
                  for p, (rows, _) in zip(probs, spans))
        return out / norm

    for case in range(k_ref.shape[0] // blk):
        @pl.when(qi == case)
        def _(case=case):
            o = softmax_pv(0, case) - lam * softmax_pv(1, case)
            y = o * lax.rsqrt(jnp.mean(o * o, axis=-1, keepdims=True) + EPS)
            o_ref[...] = ((y * g_ref[...]) * (1.0 - lam_init)).astype(o_ref.dtype)


def _diff_attention(qkv, bias, lam_p, subln_g, lam_init, batch, seq):
    blk = DA_BLK
    nq = seq // blk
    blocks = (2 * _nbytes((blk, DA_V_DIM), BF16) + 2 * _nbytes((seq, DA_V_DIM), BF16)
              + _nbytes((2, blk, blk), F32))
    scratch = 6 * _nbytes((blk, seq), F32)
    k_col0 = DA_QK // DA_V_DIM
    v_col0 = 2 * DA_QK // DA_V_DIM
    return pl.pallas_call(
        functools.partial(_da_kernel, lam_init=lam_init),
        out_shape=jax.ShapeDtypeStruct((batch * seq, DA_V), BF16),
        grid=(batch, DA_HEADS, nq),
        in_specs=[pl.BlockSpec((blk, DA_V_DIM), lambda b, h, i: (b * nq + i, h)),
                  pl.BlockSpec((seq, DA_V_DIM), lambda b, h, i: (b, k_col0 + h)),
                  pl.BlockSpec((seq, DA_V_DIM), lambda b, h, i: (b, v_col0 + h)),
                  pl.BlockSpec((1, 2, blk, blk), lambda b, h, i: (h, 0, 0, 0)),
                  pl.BlockSpec((4, HEAD_DIM), lambda b, h, i: (0, 0)),
                  pl.BlockSpec((1, DA_V_DIM), lambda b, h, i: (0, 0))],
        out_specs=pl.BlockSpec((blk, DA_V_DIM), lambda b, h, i: (b * nq + i, h)),
        compiler_params=_params(("parallel", "parallel", "parallel"), blocks, scratch),
        name="diff_attention",
    )(qkv, qkv, qkv, bias, lam_p, subln_g.reshape(1, DA_V_DIM))


def _swa_kernel(q_ref, kc_ref, kp_ref, vc_ref, vp_ref, bias_ref, sink_ref, o_ref):
    n = pl.program_id(1)
    is_prev = lax.broadcasted_iota(jnp.int32, (1, 2 * WINDOW), 1) < WINDOW
    no_prev = jnp.where(is_prev & (n == 0), NEG, 0.0).astype(F32)
    for hk in range(SW_KV_HEADS):
        cols = slice(hk * HEAD_DIM, (hk + 1) * HEAD_DIM)
        heads = [slice((hk * SW_GROUP + g) * HEAD_DIM, (hk * SW_GROUP + g + 1) * HEAD_DIM) for g in range(SW_GROUP)]
        keys = jnp.concatenate([kp_ref[:, cols], kc_ref[:, cols]], axis=0)
        values = jnp.concatenate([vp_ref[:, cols], vc_ref[:, cols]], axis=0)
        for i in range(q_ref.shape[0] // WINDOW):
            rows = slice(i * WINDOW, (i + 1) * WINDOW)
            band = slice(i * WINDOW, (i + 2) * WINDOW)
            q = jnp.concatenate([q_ref[rows, hd] for hd in heads], axis=0)
            s = _dot_nt(q, keys[band]) + bias_ref[hk]
            if i == 0:
                s = s + no_prev
            sink = sink_ref[hk]
            m = jnp.maximum(jnp.max(s, axis=-1, keepdims=True), sink)
            e = jnp.exp(s - m)
            den = jnp.sum(e, axis=-1, keepdims=True) + jnp.exp(sink - m)
            o = jnp.dot(e.astype(BF16), values[band], preferred_element_type=F32) / den
            for g, hd in enumerate(heads):
                o_ref[rows, hd] = o[g * WINDOW:(g + 1) * WINDOW, :].astype(o_ref.dtype)


def _sliding_attention(qkv, bias_tiles, sinks, batch, seq, group=SW_BLOCKS_PER_STEP):
    nb = seq // WINDOW
    ng = nb // group
    span = group * WINDOW
    q_col = 3 * DA_QK // SW_Q
    k_col = (3 * DA_QK + SW_Q) // SW_KV
    v_col = k_col + 1
    cur = lambda b, n: b * ng + n
    prev = lambda b, n: b * nb + jnp.maximum(n * group - 1, 0)
    rows = SW_GROUP * WINDOW
    bias = bias_tiles.reshape(SW_KV_HEADS, SW_GROUP, 2, WINDOW, WINDOW)[:, :, ::-1]
    bias = bias.transpose(0, 1, 3, 2, 4).reshape(SW_KV_HEADS, rows, 2 * WINDOW)
    sink_cols = jnp.repeat(sinks.astype(F32).reshape(SW_KV_HEADS, SW_GROUP), WINDOW, axis=1).reshape(
        SW_KV_HEADS, rows, 1)
    blocks = (2 * _nbytes((span, SW_Q), BF16) + 2 * _nbytes((span + WINDOW, SW_KV), BF16)
              + _nbytes((SW_KV_HEADS, rows, 2 * WINDOW), F32) + _nbytes((SW_KV_HEADS, rows, 128), F32))
    return pl.pallas_call(
        _swa_kernel,
        out_shape=jax.ShapeDtypeStruct((batch * seq, SW_Q), BF16),
        grid=(batch, ng),
        in_specs=[pl.BlockSpec((span, SW_Q), lambda b, n: (cur(b, n), q_col)),
                  pl.BlockSpec((span, SW_KV), lambda b, n: (cur(b, n), k_col)),
                  pl.BlockSpec((WINDOW, SW_KV), lambda b, n: (prev(b, n), k_col)),
                  pl.BlockSpec((span, SW_KV), lambda b, n: (cur(b, n), v_col)),
                  pl.BlockSpec((WINDOW, SW_KV), lambda b, n: (prev(b, n), v_col)),
                  pl.BlockSpec((SW_KV_HEADS, rows, 2 * WINDOW), lambda b, n: (0, 0, 0)),
                  pl.BlockSpec((SW_KV_HEADS, rows, 1), lambda b, n: (0, 0, 0))],
        out_specs=pl.BlockSpec((span, SW_Q), lambda b, n: (cur(b, n), 0)),
        compiler_params=_params(("parallel", "parallel"), blocks, 16 * group * _nbytes((rows, 2 * WINDOW), F32)),
        name="sliding_attention",
    )(qkv, qkv, qkv, qkv, qkv, bias, sink_cols)


def _merge_kernel(oa_ref, ob_ref, wa_ref, wb_ref, ga_ref, gb_ref, o_ref):
    a = jnp.dot(oa_ref[...], wa_ref[...], preferred_element_type=F32)
    b = jnp.dot(ob_ref[...], wb_ref[...], preferred_element_type=F32)
    o_ref[...] = (ga_ref[...].astype(F32) * a + gb_ref[...].astype(F32) * b).astype(o_ref.dtype)


def _merge(o_a, o_b, w_a, w_b, gates, bm=1024, bn=1024):
    t, ka = o_a.shape
    kb = o_b.shape[1]
    d = w_a.shape[1]
    nj = d // bn
    blocks = (_nbytes((bm, ka), BF16) + _nbytes((bm, kb), BF16) + _nbytes((ka, bn), BF16)
              + _nbytes((kb, bn), BF16) + 2 * _nbytes((bm, bn), gates.dtype) + _nbytes((bm, bn), BF16))
    return pl.pallas_call(
        _merge_kernel,
        out_shape=jax.ShapeDtypeStruct((t, d), BF16),
        grid=(t // bm, nj),
        in_specs=[pl.BlockSpec((bm, ka), lambda i, j: (i, 0)),
                  pl.BlockSpec((bm, kb), lambda i, j: (i, 0)),
                  pl.BlockSpec((ka, bn), lambda i, j: (0, j)),
                  pl.BlockSpec((kb, bn), lambda i, j: (0, j)),
                  pl.BlockSpec((bm, bn), lambda i, j: (i, j)),
                  pl.BlockSpec((bm, bn), lambda i, j: (i, nj + j))],
        out_specs=pl.BlockSpec((bm, bn), lambda i, j: (i, j)),
        compiler_params=_params(("parallel", "parallel"), blocks, 2 * _nbytes((bm, bn), F32)),
        name="merge",
    )(o_a, o_b, w_a, w_b, gates, gates)


def _sort_pairs(n):
    pairs = []

    def merge(lo, hi, r):
        step = r * 2
        if step < hi - lo:
            merge(lo, hi, step)
            merge(lo + r, hi, step)
            pairs.extend((i, i + r) for i in range(lo + r, hi - r, step))
        else:
            pairs.append((lo, lo + r))

    def sort(lo, hi):
        if hi - lo >= 1:
            mid = lo + (hi - lo) // 2
            sort(lo, mid)
            sort(mid + 1, hi)
            merge(lo, hi, 1)

    sort(0, n - 1)
    return pairs


_SORT16 = _sort_pairs(PEER_TOPK)


def _sort_desc(xs):
    xs = list(xs)
    for i, j in _SORT16:
        xs[i], xs[j] = jnp.maximum(xs[i], xs[j]), jnp.minimum(xs[i], xs[j])
    return xs


def _merge_top(a, b):
    k = PEER_TOPK
    xs = [jnp.maximum(a[i], b[k - 1 - i]) for i in range(k)]
    d = k // 2
    while d >= 1:
        for i in range(k):
            if not i & d:
                xs[i], xs[i + d] = jnp.maximum(xs[i], xs[i + d]), jnp.minimum(xs[i], xs[i + d])
        d //= 2
    return xs


def _top16_over_rows(s):
    groups = [s[a * 8:(a + 1) * 8, :] for a in range(s.shape[0] // 8)]
    xs = _sort_desc(groups)
    for shift in (4, 2, 1):
        xs = _merge_top(xs, [pltpu.roll(x, shift, 0) for x in xs])
    return xs


def _count_leading(pred, values):
    n = len(values)

    def pick(lo, hi, taken):
        if not taken:
            return values[(lo + hi) // 2 - 1]
        mid = (lo + hi) // 2
        return jnp.where(taken[0], pick(mid, hi, taken[1:]), pick(lo, mid, taken[1:]))

    taken = []
    step = n // 2
    while step >= 1:
        taken.append(pred(pick(0, n, taken)))
        step //= 2
    total = None
    for i, t in enumerate(taken):
        part = jnp.where(t, float(n >> (i + 1)), 0.0)
        total = part if total is None else total + part
    return total + jnp.where(pred(values[n - 1]), 1.0, 0.0)


def _route_kernel(q_ref, k1_ref, k2_ref, cnt_ref, e1_ref, rank_ref, e2_ref, top_ref):
    q = q_ref[...]
    s1 = _dot_nt(k1_ref[...], q[:, :KEY_DIM])
    s2 = _dot_nt(k2_ref[...], q[:, KEY_DIM:])
    v1 = _top16_over_rows(s1)
    v2 = _top16_over_rows(s2)
    k = PEER_TOPK
    top = [v1[0] + v2[b] for b in range(k)]
    rest = [v1[a] + v2[b] for a in range(1, k) for b in range(k) if (a + 1) * (b + 1) <= k]
    pad = jnp.full(top[0].shape, -jnp.inf, F32)
    rest = rest + [pad] * (-len(rest) % k)
    for g in range(len(rest) // k):
        top = _merge_top(top, _sort_desc(rest[g * k:(g + 1) * k]))
    z = jnp.ones_like(top[0])
    for c in top[1:]:
        z = z + jnp.exp(c - top[0])
    tau = top[k - 1][0:1]
    best = [v[0:1] for v in v2]
    cnt = _count_leading(lambda b: s1 + b >= tau, best)
    rank = _count_leading(lambda b: b > s2, best)
    cnt_ref[0] = cnt
    rank_ref[0] = _pack_rows(rank.astype(BF16))
    e1_ref[0] = jnp.exp(s1 - v1[0][0:1]) / z[0:1]
    e2_ref[0] = _pack_rows(jnp.exp(s2 - v2[0][0:1]).astype(BF16))
    top_ref[0] = 1.0 / z[0:1]


def _route(q, k1, k2, tm=1024):
    t = q.shape[0]
    wide = jax.ShapeDtypeStruct((PEER_HEADS, N_KEYS, t), F32)
    narrow = jax.ShapeDtypeStruct((PEER_HEADS, N_KEYS // 2, t), jnp.uint32)
    spec = pl.BlockSpec((1, N_KEYS, tm), lambda i, h: (h, 0, i))
    narrow_spec = pl.BlockSpec((1, N_KEYS // 2, tm), lambda i, h: (h, 0, i))
    blocks = _nbytes((tm, 2 * KEY_DIM), BF16) + 3 * _nbytes((N_KEYS, tm), F32)
    return pl.pallas_call(
        _route_kernel,
        out_shape=(wide, wide, narrow, narrow, jax.ShapeDtypeStruct((PEER_HEADS, 1, t), F32)),
        grid=(t // tm, PEER_HEADS),
        in_specs=[pl.BlockSpec((tm, 2 * KEY_DIM), lambda i, h: (i, h)),
                  pl.BlockSpec((N_KEYS, KEY_DIM), lambda i, h: (0, 0)),
                  pl.BlockSpec((N_KEYS, KEY_DIM), lambda i, h: (0, 0))],
        out_specs=(spec, spec, narrow_spec, narrow_spec, pl.BlockSpec((1, 1, tm), lambda i, h: (h, 0, i))),
        compiler_params=_params(("parallel", "parallel"), blocks, 24 * _nbytes((N_KEYS, tm), F32)),
        name="peer_route",
    )(q, k1, k2)


def _peer_up_kernel(inv_u_ref, inv_v_ref, u_ref, h_ref, inv_ref, wscale_ref, cnt_ref, e1_ref, rank_ref, e2_ref,
                    o_ref, act_a, act_b, *, expert_tiles):
    s = pl.program_id(0)
    first_scale_block = (jnp.maximum(s - 1, 0) % expert_tiles) * (act_a.shape[0] // FP8_ROW_BLOCK)

    @pl.when(s == 0)
    def _():
        act_a[...] = jnp.zeros(act_a.shape, F32)
        act_b[...] = jnp.zeros(act_b.shape, F32)

    packed = (N_KEYS // BF16_ROWS, BF16_ROWS, 128)
    chunks = u_ref.shape[0]
    blocks_per_chunk = act_a.shape[0] // N_KEYS // chunks

    def epilogue_block(act_old, r):
        rows = pl.ds(pl.multiple_of(r * N_KEYS, N_KEYS), N_KEYS)
        out_rows = pl.ds(pl.multiple_of(r * (N_KEYS // 4), N_KEYS // 4), N_KEYS // 4)
        scale_block = first_scale_block + r // (FP8_ROW_BLOCK // N_KEYS)
        inv_u, inv_v = inv_u_ref[scale_block], inv_v_ref[scale_block]
        cnt_rows = [cnt_ref[h, pl.ds(r, 1), :] for h in range(PEER_HEADS)]
        e1_rows = [e1_ref[h, pl.ds(r, 1), :] * (wscale_ref[...] * inv_v) for h in range(PEER_HEADS)]
        for c in range(act_old.shape[1] // 128):
            cols = slice(c * 128, (c + 1) * 128)
            act = act_old[rows, cols]
            act_old[rows, cols] = jnp.zeros_like(act)
            act = act * (inv_ref[:, cols] * inv_u)
            gate = None
            for h in range(PEER_HEADS):
                cnt = jnp.tile(jnp.broadcast_to(cnt_rows[h][:, cols], packed[1:]).astype(BF16), (packed[0], 1))
                e1 = jnp.tile(jnp.broadcast_to(e1_rows[h][:, cols], packed[1:]).astype(BF16), (packed[0], 1))
                routed = _unpack_rows(rank_ref[h, :, cols]) < cnt
                term = jnp.where(routed, _unpack_rows(e2_ref[h, :, cols]) * e1, jnp.zeros((), BF16))
                gate = term if gate is None else gate + term
            w = jax.nn.gelu(act).astype(BF16) * gate
            o_ref[out_rows, cols] = _pack_rows(w.astype(FP8))

    def step(act_new, act_old):
        def chunk(k, carry):
            act_new[...] += _dot_nt(u_ref[k], h_ref[k])
            for b in range(blocks_per_chunk):
                epilogue_block(act_old, k * blocks_per_chunk + b)
            return carry

        lax.fori_loop(0, chunks, chunk, 0, unroll=PEER_UP_UNROLL)

    @pl.when(s % 2 == 0)
    def _():
        step(act_a, act_b)

    @pl.when(s % 2 == 1)
    def _():
        step(act_b, act_a)


def _peer_up(u, inv_u, inv_v, hn, inv_h, w_scale, cnt, e1, rank, e2, te=PEER_UP_TE, tm=512):
    chunks, n_exp, dc = u.shape
    t = hn.shape[1]
    rows = te // N_KEYS
    assert rows % chunks == 0
    nj = n_exp // te
    steps = (t // tm) * nj

    def tile_of(step):
        return step // nj, step % nj

    def now(s):
        return tile_of(jnp.minimum(s, steps - 1))

    def lag(s):
        return tile_of(jnp.maximum(s - 1, 0))

    row_spec = pl.BlockSpec((PEER_HEADS, rows, tm), lambda s: (0, lag(s)[1], lag(s)[0]))
    full_spec = pl.BlockSpec((PEER_HEADS, N_KEYS // 2, tm), lambda s: (0, 0, lag(s)[0]))
    blocks = (_nbytes((chunks, te, dc), FP8) + _nbytes((chunks, tm, dc), FP8)
              + 2 * _nbytes((PEER_HEADS, N_KEYS, tm), BF16)
              + 2 * _nbytes((PEER_HEADS, rows, tm), F32) + _nbytes((te, tm), BF16))
    smem = pl.BlockSpec(memory_space=pltpu.SMEM)
    return pl.pallas_call(
        functools.partial(_peer_up_kernel, expert_tiles=nj),
        out_shape=jax.ShapeDtypeStruct((n_exp // 4, t), jnp.uint32),
        grid=(steps + 1,),
        in_specs=[smem, smem,
                  pl.BlockSpec((chunks, te, dc), lambda s: (0, now(s)[1], 0)),
                  pl.BlockSpec((chunks, tm, dc), lambda s: (0, now(s)[0], 0)),
                  pl.BlockSpec((1, tm), lambda s: (0, lag(s)[0])),
                  pl.BlockSpec((1, tm), lambda s: (0, lag(s)[0])),
                  row_spec, row_spec, full_spec, full_spec],
        out_specs=pl.BlockSpec((te // 4, tm), lambda s: (lag(s)[1], lag(s)[0])),
        scratch_shapes=[pltpu.VMEM((te, tm), F32), pltpu.VMEM((te, tm), F32)],
        compiler_params=_params(("arbitrary",), blocks, 3 * _nbytes((te, tm), F32)),
        name="peer_up",
    )(inv_u, inv_v, u, hn, inv_h, w_scale, cnt, e1, rank, e2)


def _peer_down_kernel(vt_ref, w_ref, inv_ref, x_ref, o_ref, acc_ref):
    kk = pl.program_id(2)

    @pl.when(kk == 0)
    def _():
        acc_ref[...] = jnp.zeros(acc_ref.shape, F32)

    acc_ref[...] += jnp.dot(vt_ref[...], _unpack_rows(w_ref[...], FP8), preferred_element_type=F32)

    @pl.when(kk == pl.num_programs(2) - 1)
    def _():
        o_ref[...] = x_ref[...] + (acc_ref[...] * inv_ref[...]).T


def _peer_down(vt, wt, inv_w_scale, x, bd=1024, bt=1024, tk=4096):
    d, n_exp = vt.shape
    t = wt.shape[1]
    blocks = (_nbytes((bd, tk), FP8) + _nbytes((tk, bt), FP8) + 2 * _nbytes((bt, bd), F32))
    return pl.pallas_call(
        _peer_down_kernel,
        out_shape=jax.ShapeDtypeStruct((t, d), F32),
        grid=(d // bd, t // bt, n_exp // tk),
        in_specs=[pl.BlockSpec((bd, tk), lambda i, j, k: (i, k)),
                  pl.BlockSpec((tk // 4, bt), lambda i, j, k: (k, j)),
                  pl.BlockSpec((1, bt), lambda i, j, k: (0, j)),
                  pl.BlockSpec((bt, bd), lambda i, j, k: (j, i))],
        out_specs=pl.BlockSpec((bt, bd), lambda i, j, k: (j, i)),
        scratch_shapes=[pltpu.VMEM((bd, bt), F32)],
        compiler_params=_params(("parallel", "parallel", "arbitrary"), blocks, 3 * _nbytes((bd, bt), F32)),
        name="peer_down",
    )(vt, wt, inv_w_scale, x)


def _w_scale(h_norm, u_norm, top_weight, inv_v):
    bound = h_norm.reshape(1, -1) * u_norm * jnp.sum(top_weight, axis=0) * jnp.max(inv_v)
    return _pow2_scale(bound)


def _ple_kernel(rows_ref, g_ref, wg_ref, p_ref, wp_ref, x_ref, o_ref, h_ref):
    @pl.when(pl.program_id(1) == 0)
    def _():
        rows = rows_ref[...]
        y = rows * lax.rsqrt(jnp.mean(rows * rows, axis=-1, keepdims=True) + EPS)
        h_ref[...] = (y * g_ref[...]).astype(h_ref.dtype)

    gate = jax.nn.sigmoid(jnp.dot(h_ref[...], wg_ref[...], preferred_element_type=F32))
    emb = jnp.dot(p_ref[...], wp_ref[...], preferred_element_type=F32)
    o_ref[...] = x_ref[...] + gate * emb


def _ple(x, g, w_gate, p, w_proj, bm=512, bn=512):
    t, d = x.shape
    pd = p.shape[1]
    n = w_gate.shape[1]
    blocks = (_nbytes((bm, d), F32) + _nbytes((d, bn), BF16) + _nbytes((bm, pd), BF16)
              + _nbytes((pd, bn), BF16) + 2 * _nbytes((bm, bn), F32))
    return pl.pallas_call(
        _ple_kernel,
        out_shape=jax.ShapeDtypeStruct((t, n), F32),
        grid=(t // bm, n // bn),
        in_specs=[pl.BlockSpec((bm, d), lambda i, j: (i, 0)),
                  pl.BlockSpec((1, d), lambda i, j: (0, 0)),
                  pl.BlockSpec((d, bn), lambda i, j: (0, j)),
                  pl.BlockSpec((bm, pd), lambda i, j: (i, 0)),
                  pl.BlockSpec((pd, bn), lambda i, j: (0, j)),
                  pl.BlockSpec((bm, bn), lambda i, j: (i, j))],
        out_specs=pl.BlockSpec((bm, bn), lambda i, j: (i, j)),
        scratch_shapes=[pltpu.VMEM((bm, d), BF16)],
        compiler_params=_params(("parallel", "arbitrary"), blocks, 2 * _nbytes((bm, bn), F32) + 2 * _nbytes((bm, d), F32)),
        name="ple",
    )(x, g.reshape(1, d), w_gate, p, w_proj, x)


def _qkv_column_scale():
    s = HEAD_DIM ** -0.5
    parts = [(DA_QK, s), (DA_QK, 1.0), (DA_V, 1.0), (SW_Q, s), (SW_KV, 1.0), (SW_KV, 1.0)]
    return jnp.concatenate([jnp.full((1, w), v, F32) for w, v in parts], axis=1)


@jax.jit
def kernel(x, p, positions, rel_bias, norm_mix, w_in, da_lambda, da_subln, sw_sinks, w_br_a, w_br_b, w_out,
           norm_ffn, peer_wq, peer_k1, peer_k2, peer_u, peer_v, norm_ple, ple_gate, ple_proj, norm_final):
    del positions
    batch, seq, d = x.shape
    t = batch * seq
    depth = w_in.shape[0]
    xf = x.reshape(t, d)
    da_bias = _bias_tiles(rel_bias[:, :DA_HEADS], DA_BLK, None, True)
    sw_bias = _bias_tiles(rel_bias[:, DA_HEADS:], WINDOW, WINDOW, False)
    col_scale = _qkv_column_scale()
    tile = pl.BlockSpec((1, 1024), lambda i, j: (0, j))
    for i in range(depth):
        lam_init = 0.8 - 0.6 * math.exp(-0.3 * i)
        h = _rmsnorm(xf, norm_mix[i], BF16)
        qkv = _matmul(_mm_scale_kernel, h, _narrow(w_in, i, 0, QKV_WIDTH, 1024), [col_scale], [tile],
                      BF16, 1024, 1024, "proj_qkv")
        gates = _matmul(_mm_sigmoid_kernel, h, _narrow(w_in, i, QKV_WIDTH, w_in.shape[2] - QKV_WIDTH, 1024),
                        [], [], BF16, 1024, 1024, "proj_gates")
        o_a = _diff_attention(qkv, da_bias, da_lambda[i], da_subln[i], lam_init, batch, seq)
        o_b = _sliding_attention(qkv, sw_bias, sw_sinks[i], batch, seq)
        merged = _merge(o_a, o_b, _narrow(w_br_a, i), _narrow(w_br_b, i), gates)
        xf = _matmul(_mm_residual_kernel, merged, _narrow(w_out, i), [xf],
                     [pl.BlockSpec((1024, 512), lambda i, j: (i, j))], F32, 1024, 512, "proj_out")
        hn, hn8, hn_inv, hn_norm = _rmsnorm_fp8(xf, norm_ffn[i], PEER_UP_CHUNKS)
        q = _matmul(_mm_plain_kernel, hn, _narrow(peer_wq, i), [], [], BF16, 1024, 1024, "peer_query")
        cnt, e1, rank, e2, top = _route(q, peer_k1[i].astype(BF16), peer_k2[i].astype(BF16))
        u8, inv_u, u_norm = _fp8_rows(peer_u, i, "chunked", PEER_UP_CHUNKS)
        vt8, inv_v, _ = _fp8_rows(peer_v, i, "transposed")
        w_scale, inv_w_scale = _w_scale(hn_norm, u_norm, top, inv_v)
        wt = _peer_up(u8, inv_u, inv_v, hn8, hn_inv.reshape(1, t), w_scale, cnt, e1, rank, e2)
        xf = _peer_down(vt8, wt, inv_w_scale, xf)
        xf = _ple(xf, norm_ple[i], _narrow(ple_gate, i), p[i].reshape(t, -1).astype(BF16), ple_proj[i].astype(BF16))
    return _rmsnorm(xf, norm_final, F32).reshape(batch, seq, d)
```

```python
import functools
import math

import jax
import jax.numpy as jnp
from jax import lax
from jax.experimental import pallas as pl
from jax.experimental.pallas import tpu as pltpu

F32 = jnp.float32
BF16 = jnp.bfloat16
FP8 = jnp.float8_e4m3fn
FP8_TARGET = 240.0
FP8_TINY = 1e-30

HEAD_DIM = 128
DA_HEADS = 8
DA_V_DIM = 2 * HEAD_DIM
SW_Q_HEADS = 16
SW_KV_HEADS = 4
SW_GROUP = SW_Q_HEADS // SW_KV_HEADS
WINDOW = 128
N_BUCKETS = 32
MAX_EXACT = N_BUCKETS // 2
MAX_DIST = 128
NEG = -1e30
DA_QK = DA_HEADS * 2 * HEAD_DIM
DA_V = DA_HEADS * DA_V_DIM
SW_Q = SW_Q_HEADS * HEAD_DIM
SW_KV = SW_KV_HEADS * HEAD_DIM
QKV_WIDTH = 3 * DA_QK + SW_Q + 2 * SW_KV
PEER_HEADS = 8
N_KEYS = 128
PEER_TOPK = 16
KEY_DIM = 128
EPS = 1e-6

V7X_VMEM_REQUEST_CAP = 60 * 1024 * 1024
BF16_ROWS = 16
DA_BLK = 512
SW_BLOCKS_PER_STEP = 4
PEER_UP_TE = 1024
FP8_ROW_BLOCK = 512
PEER_UP_CHUNKS = 4
PEER_UP_UNROLL = 2
PEER_UP_SUB_ROWS = 64


def _nbytes(shape, dtype):
    return math.prod(shape) * jnp.dtype(dtype).itemsize


def _params(semantics, block_bytes, scratch_bytes=0, flags=None):
    need = int(1.25 * (2 * block_bytes + scratch_bytes)) + (4 << 20)
    return pltpu.CompilerParams(dimension_semantics=semantics,
                                vmem_limit_bytes=min(need, V7X_VMEM_REQUEST_CAP), flags=flags)


def _pack_rows(x):
    return pltpu.bitcast(x, jnp.uint32)


def _unpack_rows(x, dtype=BF16):
    return pltpu.bitcast(x, dtype)


def _dot_nt(a, b):
    return lax.dot_general(a, b, (((1,), (1,)), ((), ())), preferred_element_type=F32)


def _rmsnorm_kernel(x_ref, g_ref, o_ref):
    x = x_ref[...]
    y = x * lax.rsqrt(jnp.mean(x * x, axis=-1, keepdims=True) + EPS)
    o_ref[...] = (y * g_ref[...]).astype(o_ref.dtype)


def _rmsnorm(x, g, out_dtype, rows=256):
    t, d = x.shape
    blocks = _nbytes((rows, d), F32) + _nbytes((rows, d), out_dtype)
    return pl.pallas_call(
        _rmsnorm_kernel,
        out_shape=jax.ShapeDtypeStruct((t, d), out_dtype),
        grid=(t // rows,),
        in_specs=[pl.BlockSpec((rows, d), lambda i: (i, 0)),
                  pl.BlockSpec((1, d), lambda i: (0, 0))],
        out_specs=pl.BlockSpec((rows, d), lambda i: (i, 0)),
        compiler_params=_params(("parallel",), blocks, _nbytes((rows, d), F32)),
        name="rmsnorm",
    )(x, g.reshape(1, d))


def _rmsnorm_fp8_kernel(x_ref, g_ref, o_ref, oc_ref, inv_ref, norm_ref):
    x = x_ref[...]
    y = x * lax.rsqrt(jnp.mean(x * x, axis=-1, keepdims=True) + EPS) * g_ref[...]
    o_ref[...] = y.astype(o_ref.dtype)
    norm_ref[...] = jnp.sqrt(jnp.sum(y * y, axis=-1, keepdims=True))
    amax = jnp.maximum(jnp.max(jnp.abs(y), axis=-1, keepdims=True), FP8_TINY)
    inv_ref[...] = amax * (1.0 / FP8_TARGET)
    y8 = y * (FP8_TARGET / amax)
    dc = oc_ref.shape[2]
    for k in range(oc_ref.shape[0]):
        oc_ref[k] = y8[:, k * dc:(k + 1) * dc].astype(oc_ref.dtype)


def _rmsnorm_fp8(x, g, chunks, rows=256):
    t, d = x.shape
    dc = d // chunks
    blocks = _nbytes((rows, d), F32) + _nbytes((rows, d), BF16) + _nbytes((rows, d), FP8) + _nbytes((rows, 128), F32)
    return pl.pallas_call(
        _rmsnorm_fp8_kernel,
        out_shape=(jax.ShapeDtypeStruct((t, d), BF16), jax.ShapeDtypeStruct((chunks, t, dc), FP8),
                   jax.ShapeDtypeStruct((t, 1), F32), jax.ShapeDtypeStruct((t, 1), F32)),
        grid=(t // rows,),
        in_specs=[pl.BlockSpec((rows, d), lambda i: (i, 0)),
                  pl.BlockSpec((1, d), lambda i: (0, 0))],
        out_specs=(pl.BlockSpec((rows, d), lambda i: (i, 0)),
                   pl.BlockSpec((chunks, rows, dc), lambda i: (0, i, 0)),
                   pl.BlockSpec((rows, 1), lambda i: (i, 0)),
                   pl.BlockSpec((rows, 1), lambda i: (i, 0))),
        compiler_params=_params(("parallel",), blocks, 2 * _nbytes((rows, d), F32)),
        name="rmsnorm_fp8",
    )(x, g.reshape(1, d))


CAST_BLOCK_BYTES = 8 << 20


def _cast_kernel(x_ref, o_ref):
    o_ref[...] = x_ref[...].astype(o_ref.dtype)


def _narrow(w, layer, col0=0, ncols=None, bc=None):
    _, r, c = w.shape
    ncols = c if ncols is None else ncols
    bc = ncols if bc is None else bc
    br = min(r, CAST_BLOCK_BYTES // (bc * 4))
    return pl.pallas_call(
        _cast_kernel,
        out_shape=jax.ShapeDtypeStruct((r, ncols), BF16),
        grid=(r // br, ncols // bc),
        in_specs=[pl.BlockSpec((None, br, bc), lambda i, j: (layer, i, col0 // bc + j))],
        out_specs=pl.BlockSpec((br, bc), lambda i, j: (i, j)),
        compiler_params=_params(("parallel", "parallel"), _nbytes((br, bc), F32) + _nbytes((br, bc), BF16),
                                _nbytes((br, bc), F32)),
        name="narrow",
    )(w)


def _pow2_scale(amax):
    shift = jnp.floor(jnp.log2(FP8_TARGET / jnp.maximum(amax, FP8_TINY)))
    return jnp.exp2(shift), jnp.exp2(-shift)


def _fp8_rows_kernel(x_ref, o_ref, inv_ref, norm_ref, *, layout):
    x = x_ref[...]
    norm = jnp.sqrt(jnp.max(jnp.sum(x * x, axis=1, keepdims=True), axis=0, keepdims=True))
    norm_ref[...] = jnp.broadcast_to(norm, norm_ref.shape)
    amax = jnp.max(jnp.max(jnp.abs(x), axis=0, keepdims=True), axis=1, keepdims=True)
    scale, inv = _pow2_scale(amax)
    inv_ref[...] = jnp.broadcast_to(inv, inv_ref.shape)
    y = x * scale
    if layout == "chunked":
        dc = o_ref.shape[2]
        for k in range(o_ref.shape[0]):
            o_ref[k] = y[:, k * dc:(k + 1) * dc].astype(o_ref.dtype)
    else:
        o_ref[...] = y.T.astype(o_ref.dtype)


def _fp8_rows(w, layer, layout, chunks=None, br=FP8_ROW_BLOCK):
    _, r, c = w.shape
    if layout == "chunked":
        shape = (chunks, r, c // chunks)
        out_spec = pl.BlockSpec((chunks, br, c // chunks), lambda i: (0, i, 0))
    else:
        shape = (c, r)
        out_spec = pl.BlockSpec((c, br), lambda i: (0, i))
    small = jax.ShapeDtypeStruct((r // br, 1, 128), F32)
    small_spec = pl.BlockSpec((1, 1, 128), lambda i: (i, 0, 0))
    out, inv, norm = pl.pallas_call(
        functools.partial(_fp8_rows_kernel, layout=layout),
        out_shape=(jax.ShapeDtypeStruct(shape, FP8), small, small),
        grid=(r // br,),
        in_specs=[pl.BlockSpec((None, br, c), lambda i: (layer, i, 0))],
        out_specs=(out_spec, small_spec, small_spec),
        compiler_params=_params(("parallel",), _nbytes((br, c), F32) + _nbytes((br, c), FP8), 2 * _nbytes((br, c), F32)),
        name="fp8_rows_" + layout,
    )(w)
    return out, inv[:, 0, 0], jnp.max(norm)


def _mm_scale_kernel(a_ref, b_ref, s_ref, o_ref):
    acc = jnp.dot(a_ref[...], b_ref[...], preferred_element_type=F32)
    o_ref[...] = (acc * s_ref[...]).astype(o_ref.dtype)


def _mm_sigmoid_kernel(a_ref, b_ref, o_ref):
    acc = jnp.dot(a_ref[...], b_ref[...], preferred_element_type=F32)
    o_ref[...] = jax.nn.sigmoid(acc).astype(o_ref.dtype)


def _mm_plain_kernel(a_ref, b_ref, o_ref):
    o_ref[...] = jnp.dot(a_ref[...], b_ref[...], preferred_element_type=F32).astype(o_ref.dtype)


def _mm_residual_kernel(a_ref, b_ref, x_ref, o_ref):
    o_ref[...] = x_ref[...] + jnp.dot(a_ref[...], b_ref[...], preferred_element_type=F32)


def _matmul(body, a, b, extra, extra_specs, out_dtype, bm, bn, name):
    m, k = a.shape
    n = b.shape[1]
    blocks = (_nbytes((bm, k), a.dtype) + _nbytes((k, bn), b.dtype) + _nbytes((bm, bn), out_dtype)
              + sum(_nbytes(s.block_shape, e.dtype) for s, e in zip(extra_specs, extra)))
    return pl.pallas_call(
        body,
        out_shape=jax.ShapeDtypeStruct((m, n), out_dtype),
        grid=(m // bm, n // bn),
        in_specs=[pl.BlockSpec((bm, k), lambda i, j: (i, 0)),
                  pl.BlockSpec((k, bn), lambda i, j: (0, j))] + list(extra_specs),
        out_specs=pl.BlockSpec((bm, bn), lambda i, j: (i, j)),
        compiler_params=_params(("parallel", "parallel"), blocks, _nbytes((bm, bn), F32)),
        name=name,
    )(a, b, *extra)


def _bias_kernel(tab_ref, o_ref, *, blk, window, rebase):
    h = pl.program_id(0)
    r = lax.broadcasted_iota(jnp.int32, (blk, blk), 0)
    c = lax.broadcasted_iota(jnp.int32, (blk, blk), 1)
    base = tab_ref[N_BUCKETS - 1, h] if rebase else 0.0
    for delta in (0, 1):
        rel = r - c + delta * blk
        n = jnp.maximum(rel, 0)
        nf = jnp.maximum(n, 1).astype(F32)
        large = MAX_EXACT + (jnp.log(nf / MAX_EXACT) / math.log(MAX_DIST / MAX_EXACT)
                             * (N_BUCKETS - MAX_EXACT)).astype(jnp.int32)
        large = jnp.minimum(large, N_BUCKETS - 1)
        bucket = jnp.where(n < MAX_EXACT, n, large)
        bias = jnp.zeros((blk, blk), F32)
        for b in range(N_BUCKETS):
            bias = jnp.where(bucket == b, tab_ref[b, h] - base, bias)
        mask = rel >= 0
        if window is not None:
            mask = mask & (rel < window)
        o_ref[0, delta] = jnp.where(mask, bias, NEG)


def _bias_tiles(tab, blk, window, rebase):
    heads = tab.shape[1]
    return pl.pallas_call(
        functools.partial(_bias_kernel, blk=blk, window=window, rebase=rebase),
        out_shape=jax.ShapeDtypeStruct((heads, 2, blk, blk), F32),
        grid=(heads,),
        in_specs=[pl.BlockSpec(memory_space=pltpu.SMEM)],
        out_specs=pl.BlockSpec((1, 2, blk, blk), lambda h: (h, 0, 0, 0)),
        compiler_params=_params(("parallel",), _nbytes((2, blk, blk), F32), 4 * _nbytes((blk, blk), F32)),
        name="bias_tiles",
    )(tab)


def _da_kernel(q_ref, k_ref, v_ref, bias_ref, lam_ref, g_ref, o_ref, *, lam_init):
    blk = q_ref.shape[0]
    qi = pl.program_id(2)
    lp = lam_ref[...]
    lam = (jnp.exp(jnp.sum(lp[0:1] * lp[1:2], axis=-1, keepdims=True))
           - jnp.exp(jnp.sum(lp[2:3] * lp[3:4], axis=-1, keepdims=True)) + lam_init)

    def softmax_pv(j, case):
        dims = slice(j * HEAD_DIM, (j + 1) * HEAD_DIM)
        q = q_ref[:, dims]
        spans = [(slice(case * blk, (case + 1) * blk), bias_ref[0, 0])]
        if case >= 1:
            spans.append((slice((case - 1) * blk, case * blk), bias_ref[0, 1]))
        if case >= 2:
            spans.append((slice(0, (case - 1) * blk), None))
        scores = []
        for rows, bias in spans:
            s = _dot_nt(q, k_ref[rows, dims])
            scores.append(s if bias is None else s + bias)
        m = functools.reduce(jnp.maximum, [jnp.max(s, axis=-1, keepdims=True) for s in scores])
        probs = [jnp.exp(s - m) for s in scores]
        norm = sum(jnp.sum(p, axis=-1, keepdims=True) for p in probs)
        out = sum(jnp.dot(p.astype(BF16), v_ref[rows, :], preferred_element_type=F32)
                  for p, (rows, _) in zip(probs, spans))
        return out / norm

    for case in range(k_ref.shape[0] // blk):
        @pl.when(qi == case)
        def _(case=case):
            o = softmax_pv(0, case) - lam * softmax_pv(1, case)
            y = o * lax.rsqrt(jnp.mean(o * o, axis=-1, keepdims=True) + EPS)
            o_ref[...] = ((y * g_ref[...]) * (1.0 - lam_init)).astype(o_ref.dtype)


def _diff_attention(qkv, bias, lam_p, subln_g, lam_init, batch, seq):
    blk = DA_BLK
    nq = seq // blk
    blocks = (2 * _nbytes((blk, DA_V_DIM), BF16) + 2 * _nbytes((seq, DA_V_DIM), BF16)
              + _nbytes((2, blk, blk), F32))
    scratch = 6 * _nbytes((blk, seq), F32)
    k_col0 = DA_QK // DA_V_DIM
    v_col0 = 2 * DA_QK // DA_V_DIM
    return pl.pallas_call(
        functools.partial(_da_kernel, lam_init=lam_init),
        out_shape=jax.ShapeDtypeStruct((batch * seq, DA_V), BF16),
        grid=(batch, DA_HEADS, nq),
        in_specs=[pl.BlockSpec((blk, DA_V_DIM), lambda b, h, i: (b * nq + i, h)),
                  pl.BlockSpec((seq, DA_V_DIM), lambda b, h, i: (b, k_col0 + h)),
                  pl.BlockSpec((seq, DA_V_DIM), lambda b, h, i: (b, v_col0 + h)),
                  pl.BlockSpec((1, 2, blk, blk), lambda b, h, i: (h, 0, 0, 0)),
                  pl.BlockSpec((4, HEAD_DIM), lambda b, h, i: (0, 0)),
                  pl.BlockSpec((1, DA_V_DIM), lambda b, h, i: (0, 0))],
        out_specs=pl.BlockSpec((blk, DA_V_DIM), lambda b, h, i: (b * nq + i, h)),
        compiler_params=_params(("parallel", "parallel", "parallel"), blocks, scratch),
        name="diff_attention",
    )(qkv, qkv, qkv, bias, lam_p, subln_g.reshape(1, DA_V_DIM))


def _swa_kernel(q_ref, kc_ref, kp_ref, vc_ref, vp_ref, bias_ref, sink_ref, o_ref):
    n = pl.program_id(1)
    is_prev = lax.broadcasted_iota(jnp.int32, (1, 2 * WINDOW), 1) < WINDOW
    no_prev = jnp.where(is_prev & (n == 0), NEG, 0.0).astype(F32)
    for hk in range(SW_KV_HEADS):
        cols = slice(hk * HEAD_DIM, (hk + 1) * HEAD_DIM)
        heads = [slice((hk * SW_GROUP + g) * HEAD_DIM, (hk * SW_GROUP + g + 1) * HEAD_DIM) for g in range(SW_GROUP)]
        keys = jnp.concatenate([kp_ref[:, cols], kc_ref[:, cols]], axis=0)
        values = jnp.concatenate([vp_ref[:, cols], vc_ref[:, cols]], axis=0)
        for i in range(q_ref.shape[0] // WINDOW):
            rows = slice(i * WINDOW, (i + 1) * WINDOW)
            band = slice(i * WINDOW, (i + 2) * WINDOW)
            q = jnp.concatenate([q_ref[rows, hd] for hd in heads], axis=0)
            s = _dot_nt(q, keys[band]) + bias_ref[hk]
            if i == 0:
                s = s + no_prev
            sink = sink_ref[hk]
            m = jnp.maximum(jnp.max(s, axis=-1, keepdims=True), sink)
            e = jnp.exp(s - m)
            den = jnp.sum(e, axis=-1, keepdims=True) + jnp.exp(sink - m)
            o = jnp.dot(e.astype(BF16), values[band], preferred_element_type=F32) / den
            for g, hd in enumerate(heads):
                o_ref[rows, hd] = o[g * WINDOW:(g + 1) * WINDOW, :].astype(o_ref.dtype)


def _sliding_attention(qkv, bias_tiles, sinks, batch, seq, group=SW_BLOCKS_PER_STEP):
    nb = seq // WINDOW
    ng = nb // group
    span = group * WINDOW
    q_col = 3 * DA_QK // SW_Q
    k_col = (3 * DA_QK + SW_Q) // SW_KV
    v_col = k_col + 1
    cur = lambda b, n: b * ng + n
    prev = lambda b, n: b * nb + jnp.maximum(n * group - 1, 0)
    rows = SW_GROUP * WINDOW
    bias = bias_tiles.reshape(SW_KV_HEADS, SW_GROUP, 2, WINDOW, WINDOW)[:, :, ::-1]
    bias = bias.transpose(0, 1, 3, 2, 4).reshape(SW_KV_HEADS, rows, 2 * WINDOW)
    sink_cols = jnp.repeat(sinks.astype(F32).reshape(SW_KV_HEADS, SW_GROUP), WINDOW, axis=1).reshape(
        SW_KV_HEADS, rows, 1)
    blocks = (2 * _nbytes((span, SW_Q), BF16) + 2 * _nbytes((span + WINDOW, SW_KV), BF16)
              + _nbytes((SW_KV_HEADS, rows, 2 * WINDOW), F32) + _nbytes((SW_KV_HEADS, rows, 128), F32))
    return pl.pallas_call(
        _swa_kernel,
        out_shape=jax.ShapeDtypeStruct((batch * seq, SW_Q), BF16),
        grid=(batch, ng),
        in_specs=[pl.BlockSpec((span, SW_Q), lambda b, n: (cur(b, n), q_col)),
                  pl.BlockSpec((span, SW_KV), lambda b, n: (cur(b, n), k_col)),
                  pl.BlockSpec((WINDOW, SW_KV), lambda b, n: (prev(b, n), k_col)),
                  pl.BlockSpec((span, SW_KV), lambda b, n: (cur(b, n), v_col)),
                  pl.BlockSpec((WINDOW, SW_KV), lambda b, n: (prev(b, n), v_col)),
                  pl.BlockSpec((SW_KV_HEADS, rows, 2 * WINDOW), lambda b, n: (0, 0, 0)),
                  pl.BlockSpec((SW_KV_HEADS, rows, 1), lambda b, n: (0, 0, 0))],
        out_specs=pl.BlockSpec((span, SW_Q), lambda b, n: (cur(b, n), 0)),
        compiler_params=_params(("parallel", "parallel"), blocks, 16 * group * _nbytes((rows, 2 * WINDOW), F32)),
        name="sliding_attention",
    )(qkv, qkv, qkv, qkv, qkv, bias, sink_cols)


def _merge_kernel(oa_ref, ob_ref, wa_ref, wb_ref, ga_ref, gb_ref, o_ref):
    a = jnp.dot(oa_ref[...], wa_ref[...], preferred_element_type=F32)
    b = jnp.dot(ob_ref[...], wb_ref[...], preferred_element_type=F32)
    o_ref[...] = (ga_ref[...].astype(F32) * a + gb_ref[...].astype(F32) * b).astype(o_ref.dtype)


def _merge(o_a, o_b, w_a, w_b, gates, bm=1024, bn=1024):
    t, ka = o_a.shape
    kb = o_b.shape[1]
    d = w_a.shape[1]
    nj = d // bn
    blocks = (_nbytes((bm, ka), BF16) + _nbytes((bm, kb), BF16) + _nbytes((ka, bn), BF16)
              + _nbytes((kb, bn), BF16) + 2 * _nbytes((bm, bn), gates.dtype) + _nbytes((bm, bn), BF16))
    return pl.pallas_call(
        _merge_kernel,
        out_shape=jax.ShapeDtypeStruct((t, d), BF16),
        grid=(t // bm, nj),
        in_specs=[pl.BlockSpec((bm, ka), lambda i, j: (i, 0)),
                  pl.BlockSpec((bm, kb), lambda i, j: (i, 0)),
                  pl.BlockSpec((ka, bn), lambda i, j: (0, j)),
                  pl.BlockSpec((kb, bn), lambda i, j: (0, j)),
                  pl.BlockSpec((bm, bn), lambda i, j: (i, j)),
                  pl.BlockSpec((bm, bn), lambda i, j: (i, nj + j))],
        out_specs=pl.BlockSpec((bm, bn), lambda i, j: (i, j)),
        compiler_params=_params(("parallel", "parallel"), blocks, 2 * _nbytes((bm, bn), F32)),
        name="merge",
    )(o_a, o_b, w_a, w_b, gates, gates)


def _sort_pairs(n):
    pairs = []

    def merge(lo, hi, r):
        step = r * 2
        if step < hi - lo:
            merge(lo, hi, step)
            merge(lo + r, hi, step)
            pairs.extend((i, i + r) for i in range(lo + r, hi - r, step))
        else:
            pairs.append((lo, lo + r))

    def sort(lo, hi):
        if hi - lo >= 1:
            mid = lo + (hi - lo) // 2
            sort(lo, mid)
            sort(mid + 1, hi)
            merge(lo, hi, 1)

    sort(0, n - 1)
    return pairs


_SORT16 = _sort_pairs(PEER_TOPK)


def _sort_desc(xs):
    xs = list(xs)
    for i, j in _SORT16:
        xs[i], xs[j] = jnp.maximum(xs[i], xs[j]), jnp.minimum(xs[i], xs[j])
    return xs


def _merge_top(a, b):
    k = PEER_TOPK
    xs = [jnp.maximum(a[i], b[k - 1 - i]) for i in range(k)]
    d = k // 2
    while d >= 1:
        for i in range(k):
            if not i & d:
                xs[i], xs[i + d] = jnp.maximum(xs[i], xs[i + d]), jnp.minimum(xs[i], xs[i + d])
        d //= 2
    return xs


def _top16_over_rows(s):
    groups = [s[a * 8:(a + 1) * 8, :] for a in range(s.shape[0] // 8)]
    xs = _sort_desc(groups)
    for shift in (4, 2, 1):
        xs = _merge_top(xs, [pltpu.roll(x, shift, 0) for x in xs])
    return xs


def _count_leading(pred, values):
    n = len(values)

    def pick(lo, hi, taken):
        if not taken:
            return values[(lo + hi) // 2 - 1]
        mid = (lo + hi) // 2
        return jnp.where(taken[0], pick(mid, hi, taken[1:]), pick(lo, mid, taken[1:]))

    taken = []
    step = n // 2
    while step >= 1:
        taken.append(pred(pick(0, n, taken)))
        step //= 2
    total = None
    for i, t in enumerate(taken):
        part = jnp.where(t, float(n >> (i + 1)), 0.0)
        total = part if total is None else total + part
    return total + jnp.where(pred(values[n - 1]), 1.0, 0.0)


def _route_kernel(q_ref, k1_ref, k2_ref, cnt_ref, e1_ref, rank_ref, e2_ref, top_ref):
    q = q_ref[...]
    s1 = _dot_nt(k1_ref[...], q[:, :KEY_DIM])
    s2 = _dot_nt(k2_ref[...], q[:, KEY_DIM:])
    v1 = _top16_over_rows(s1)
    v2 = _top16_over_rows(s2)
    k = PEER_TOPK
    top = [v1[0] + v2[b] for b in range(k)]
    rest = [v1[a] + v2[b] for a in range(1, k) for b in range(k) if (a + 1) * (b + 1) <= k]
    pad = jnp.full(top[0].shape, -jnp.inf, F32)
    rest = rest + [pad] * (-len(rest) % k)
    for g in range(len(rest) // k):
        top = _merge_top(top, _sort_desc(rest[g * k:(g + 1) * k]))
    z = jnp.ones_like(top[0])
    for c in top[1:]:
        z = z + jnp.exp(c - top[0])
    tau = top[k - 1][0:1]
    best = [v[0:1] for v in v2]
    cnt = _count_leading(lambda b: s1 + b >= tau, best)
    rank = _count_leading(lambda b: b > s2, best)
    cnt_ref[0] = cnt
    rank_ref[0] = _pack_rows(rank.astype(BF16))
    e1_ref[0] = jnp.exp(s1 - v1[0][0:1]) / z[0:1]
    e2_ref[0] = _pack_rows(jnp.exp(s2 - v2[0][0:1]).astype(BF16))
    top_ref[0] = 1.0 / z[0:1]


def _route(q, k1, k2, tm=1024):
    t = q.shape[0]
    wide = jax.ShapeDtypeStruct((PEER_HEADS, N_KEYS, t), F32)
    narrow = jax.ShapeDtypeStruct((PEER_HEADS, N_KEYS // 2, t), jnp.uint32)
    spec = pl.BlockSpec((1, N_KEYS, tm), lambda i, h: (h, 0, i))
    narrow_spec = pl.BlockSpec((1, N_KEYS // 2, tm), lambda i, h: (h, 0, i))
    blocks = _nbytes((tm, 2 * KEY_DIM), BF16) + 3 * _nbytes((N_KEYS, tm), F32)
    return pl.pallas_call(
        _route_kernel,
        out_shape=(wide, wide, narrow, narrow, jax.ShapeDtypeStruct((PEER_HEADS, 1, t), F32)),
        grid=(t // tm, PEER_HEADS),
        in_specs=[pl.BlockSpec((tm, 2 * KEY_DIM), lambda i, h: (i, h)),
                  pl.BlockSpec((N_KEYS, KEY_DIM), lambda i, h: (0, 0)),
                  pl.BlockSpec((N_KEYS, KEY_DIM), lambda i, h: (0, 0))],
        out_specs=(spec, spec, narrow_spec, narrow_spec, pl.BlockSpec((1, 1, tm), lambda i, h: (h, 0, i))),
        compiler_params=_params(("parallel", "parallel"), blocks, 24 * _nbytes((N_KEYS, tm), F32)),
        name="peer_route",
    )(q, k1, k2)


def _peer_up_kernel(inv_u_ref, inv_v_ref, u_ref, h_ref, inv_ref, wscale_ref, cnt_ref, e1_ref, rank_ref, e2_ref,
                    o_ref, act_a, act_b, *, expert_tiles):
    s = pl.program_id(0)
    first_scale_block = (jnp.maximum(s - 1, 0) % expert_tiles) * (act_a.shape[0] // FP8_ROW_BLOCK)

    @pl.when(s == 0)
    def _():
        act_a[...] = jnp.zeros(act_a.shape, F32)
        act_b[...] = jnp.zeros(act_b.shape, F32)

    sub = PEER_UP_SUB_ROWS
    reps = sub // BF16_ROWS
    chunks = u_ref.shape[0]
    blocks_per_chunk = act_a.shape[0] // N_KEYS // chunks

    def epilogue_block(act_old, r):
        scale_block = first_scale_block + r // (FP8_ROW_BLOCK // N_KEYS)
        inv_u, inv_v = inv_u_ref[scale_block], inv_v_ref[scale_block]
        cnt_rows = [cnt_ref[h, pl.ds(r, 1), :] for h in range(PEER_HEADS)]
        e1_rows = [e1_ref[h, pl.ds(r, 1), :] * (wscale_ref[...] * inv_v) for h in range(PEER_HEADS)]
        for c in range(act_old.shape[1] // 128):
            cols = slice(c * 128, (c + 1) * 128)
            for part in range(N_KEYS // sub):
                act_rows = pl.ds(pl.multiple_of(r * N_KEYS + part * sub, sub), sub)
                in_rows = slice(part * sub // 2, (part + 1) * sub // 2)
                out_rows = pl.ds(pl.multiple_of((r * N_KEYS + part * sub) // 4, sub // 4), sub // 4)
                act = act_old[act_rows, cols]
                act_old[act_rows, cols] = jnp.zeros_like(act)
                act = act * (inv_ref[:, cols] * inv_u)
                gate = None
                for h in range(PEER_HEADS):
                    cnt = jnp.tile(jnp.broadcast_to(cnt_rows[h][:, cols], (BF16_ROWS, 128)).astype(BF16), (reps, 1))
                    e1 = jnp.tile(jnp.broadcast_to(e1_rows[h][:, cols], (BF16_ROWS, 128)).astype(BF16), (reps, 1))
                    routed = _unpack_rows(rank_ref[h, in_rows, cols]) < cnt
                    term = jnp.where(routed, _unpack_rows(e2_ref[h, in_rows, cols]) * e1, jnp.zeros((), BF16))
                    gate = term if gate is None else gate + term
                w = jax.nn.gelu(act).astype(BF16) * gate
                o_ref[out_rows, cols] = _pack_rows(w.astype(FP8))

    def step(act_new, act_old):
        def chunk(k, carry):
            act_new[...] += _dot_nt(u_ref[k], h_ref[k])
            for b in range(blocks_per_chunk):
                epilogue_block(act_old, k * blocks_per_chunk + b)
            return carry

        lax.fori_loop(0, chunks, chunk, 0, unroll=PEER_UP_UNROLL)

    @pl.when(s % 2 == 0)
    def _():
        step(act_a, act_b)

    @pl.when(s % 2 == 1)
    def _():
        step(act_b, act_a)


def _peer_up(u, inv_u, inv_v, hn, inv_h, w_scale, cnt, e1, rank, e2, te=PEER_UP_TE, tm=512):
    chunks, n_exp, dc = u.shape
    t = hn.shape[1]
    rows = te // N_KEYS
    assert rows % chunks == 0
    nj = n_exp // te
    steps = (t // tm) * nj

    def tile_of(step):
        return step // nj, step % nj

    def now(s):
        return tile_of(jnp.minimum(s, steps - 1))

    def lag(s):
        return tile_of(jnp.maximum(s - 1, 0))

    row_spec = pl.BlockSpec((PEER_HEADS, rows, tm), lambda s: (0, lag(s)[1], lag(s)[0]))
    full_spec = pl.BlockSpec((PEER_HEADS, N_KEYS // 2, tm), lambda s: (0, 0, lag(s)[0]))
    blocks = (_nbytes((chunks, te, dc), FP8) + _nbytes((chunks, tm, dc), FP8)
              + 2 * _nbytes((PEER_HEADS, N_KEYS, tm), BF16)
              + 2 * _nbytes((PEER_HEADS, rows, tm), F32) + _nbytes((te, tm), BF16))
    smem = pl.BlockSpec(memory_space=pltpu.SMEM)
    return pl.pallas_call(
        functools.partial(_peer_up_kernel, expert_tiles=nj),
        out_shape=jax.ShapeDtypeStruct((n_exp // 4, t), jnp.uint32),
        grid=(steps + 1,),
        in_specs=[smem, smem,
                  pl.BlockSpec((chunks, te, dc), lambda s: (0, now(s)[1], 0)),
                  pl.BlockSpec((chunks, tm, dc), lambda s: (0, now(s)[0], 0)),
                  pl.BlockSpec((1, tm), lambda s: (0, lag(s)[0])),
                  pl.BlockSpec((1, tm), lambda s: (0, lag(s)[0])),
                  row_spec, row_spec, full_spec, full_spec],
        out_specs=pl.BlockSpec((te // 4, tm), lambda s: (lag(s)[1], lag(s)[0])),
        scratch_shapes=[pltpu.VMEM((te, tm), F32), pltpu.VMEM((te, tm), F32)],
        compiler_params=_params(("arbitrary",), blocks, 3 * _nbytes((te, tm), F32)),
        name="peer_up",
    )(inv_u, inv_v, u, hn, inv_h, w_scale, cnt, e1, rank, e2)


def _peer_down_kernel(vt_ref, w_ref, inv_ref, x_ref, o_ref, acc_ref):
    kk = pl.program_id(2)

    @pl.when(kk == 0)
    def _():
        acc_ref[...] = jnp.zeros(acc_ref.shape, F32)

    acc_ref[...] += jnp.dot(vt_ref[...], _unpack_rows(w_ref[...], FP8), preferred_element_type=F32)

    @pl.when(kk == pl.num_programs(2) - 1)
    def _():
        o_ref[...] = x_ref[...] + (acc_ref[...] * inv_ref[...]).T


def _peer_down(vt, wt, inv_w_scale, x, bd=1024, bt=1024, tk=4096):
    d, n_exp = vt.shape
    t = wt.shape[1]
    blocks = (_nbytes((bd, tk), FP8) + _nbytes((tk, bt), FP8) + 2 * _nbytes((bt, bd), F32))
    return pl.pallas_call(
        _peer_down_kernel,
        out_shape=jax.ShapeDtypeStruct((t, d), F32),
        grid=(d // bd, t // bt, n_exp // tk),
        in_specs=[pl.BlockSpec((bd, tk), lambda i, j, k: (i, k)),
                  pl.BlockSpec((tk // 4, bt), lambda i, j, k: (k, j)),
                  pl.BlockSpec((1, bt), lambda i, j, k: (0, j)),
                  pl.BlockSpec((bt, bd), lambda i, j, k: (j, i))],
        out_specs=pl.BlockSpec((bt, bd), lambda i, j, k: (j, i)),
        scratch_shapes=[pltpu.VMEM((bd, bt), F32)],
        compiler_params=_params(("parallel", "parallel", "arbitrary"), blocks, 3 * _nbytes((bd, bt), F32)),
        name="peer_down",
    )(vt, wt, inv_w_scale, x)


def _w_scale(h_norm, u_norm, top_weight, inv_v):
    bound = h_norm.reshape(1, -1) * u_norm * jnp.sum(top_weight, axis=0) * jnp.max(inv_v)
    return _pow2_scale(bound)


def _ple_kernel(h_ref, wg_ref, p_ref, wp_ref, x_ref, o_ref):
    gate = jax.nn.sigmoid(jnp.dot(h_ref[...], wg_ref[...], preferred_element_type=F32))
    emb = jnp.dot(p_ref[...], wp_ref[...], preferred_element_type=F32)
    o_ref[...] = x_ref[...] + gate * emb


def _ple(hp, w_gate, p, w_proj, x, bm=1024, bn=512):
    t, d = hp.shape
    pd = p.shape[1]
    n = w_gate.shape[1]
    blocks = (_nbytes((bm, d), BF16) + _nbytes((d, bn), BF16) + _nbytes((bm, pd), BF16)
              + _nbytes((pd, bn), BF16) + 2 * _nbytes((bm, bn), F32))
    return pl.pallas_call(
        _ple_kernel,
        out_shape=jax.ShapeDtypeStruct((t, n), F32),
        grid=(t // bm, n // bn),
        in_specs=[pl.BlockSpec((bm, d), lambda i, j: (i, 0)),
                  pl.BlockSpec((d, bn), lambda i, j: (0, j)),
                  pl.BlockSpec((bm, pd), lambda i, j: (i, 0)),
                  pl.BlockSpec((pd, bn), lambda i, j: (0, j)),
                  pl.BlockSpec((bm, bn), lambda i, j: (i, j))],
        out_specs=pl.BlockSpec((bm, bn), lambda i, j: (i, j)),
        compiler_params=_params(("parallel", "parallel"), blocks, 2 * _nbytes((bm, bn), F32)),
        name="ple",
    )(hp, w_gate, p, w_proj, x)


def _qkv_column_scale():
    s = HEAD_DIM ** -0.5
    parts = [(DA_QK, s), (DA_QK, 1.0), (DA_V, 1.0), (SW_Q, s), (SW_KV, 1.0), (SW_KV, 1.0)]
    return jnp.concatenate([jnp.full((1, w), v, F32) for w, v in parts], axis=1)


@jax.jit
def kernel(x, p, positions, rel_bias, norm_mix, w_in, da_lambda, da_subln, sw_sinks, w_br_a, w_br_b, w_out,
           norm_ffn, peer_wq, peer_k1, peer_k2, peer_u, peer_v, norm_ple, ple_gate, ple_proj, norm_final):
    del positions
    batch, seq, d = x.shape
    t = batch * seq
    depth = w_in.shape[0]
    xf = x.reshape(t, d)
    da_bias = _bias_tiles(rel_bias[:, :DA_HEADS], DA_BLK, None, True)
    sw_bias = _bias_tiles(rel_bias[:, DA_HEADS:], WINDOW, WINDOW, False)
    col_scale = _qkv_column_scale()
    tile = pl.BlockSpec((1, 1024), lambda i, j: (0, j))
    for i in range(depth):
        lam_init = 0.8 - 0.6 * math.exp(-0.3 * i)
        h = _rmsnorm(xf, norm_mix[i], BF16)
        qkv = _matmul(_mm_scale_kernel, h, _narrow(w_in, i, 0, QKV_WIDTH, 1024), [col_scale], [tile],
                      BF16, 1024, 1024, "proj_qkv")
        gates = _matmul(_mm_sigmoid_kernel, h, _narrow(w_in, i, QKV_WIDTH, w_in.shape[2] - QKV_WIDTH, 1024),
                        [], [], BF16, 1024, 1024, "proj_gates")
        o_a = _diff_attention(qkv, da_bias, da_lambda[i], da_subln[i], lam_init, batch, seq)
        o_b = _sliding_attention(qkv, sw_bias, sw_sinks[i], batch, seq)
        merged = _merge(o_a, o_b, _narrow(w_br_a, i), _narrow(w_br_b, i), gates)
        xf = _matmul(_mm_residual_kernel, merged, _narrow(w_out, i), [xf],
                     [pl.BlockSpec((1024, 512), lambda i, j: (i, j))], F32, 1024, 512, "proj_out")
        hn, hn8, hn_inv, hn_norm = _rmsnorm_fp8(xf, norm_ffn[i], PEER_UP_CHUNKS)
        q = _matmul(_mm_plain_kernel, hn, _narrow(peer_wq, i), [], [], BF16, 1024, 1024, "peer_query")
        cnt, e1, rank, e2, top = _route(q, peer_k1[i].astype(BF16), peer_k2[i].astype(BF16))
        u8, inv_u, u_norm = _fp8_rows(peer_u, i, "chunked", PEER_UP_CHUNKS)
        vt8, inv_v, _ = _fp8_rows(peer_v, i, "transposed")
        w_scale, inv_w_scale = _w_scale(hn_norm, u_norm, top, inv_v)
        wt = _peer_up(u8, inv_u, inv_v, hn8, hn_inv.reshape(1, t), w_scale, cnt, e1, rank, e2)
        xf = _peer_down(vt8, wt, inv_w_scale, xf)
        hp = _rmsnorm(xf, norm_ple[i], BF16)
        xf = _ple(hp, _narrow(ple_gate, i), p[i].reshape(t, -1).astype(BF16), ple_proj[i].astype(BF16), xf)
    return _rmsnorm(xf, norm_final, F32).reshape(batch, seq, d)
```

```python
import functools
import math

import jax
import jax.numpy as jnp
from jax import lax
from jax.experimental import pallas as pl
from jax.experimental.pallas import tpu as pltpu

F32 = jnp.float32
BF16 = jnp.bfloat16
FP8 = jnp.float8_e4m3fn
FP8_TARGET = 240.0
FP8_TINY = 1e-30

HEAD_DIM = 128
DA_HEADS = 8
DA_V_DIM = 2 * HEAD_DIM
SW_Q_HEADS = 16
SW_KV_HEADS = 4
SW_GROUP = SW_Q_HEADS // SW_KV_HEADS
WINDOW = 128
N_BUCKETS = 32
MAX_EXACT = N_BUCKETS // 2
MAX_DIST = 128
NEG = -1e30
DA_QK = DA_HEADS * 2 * HEAD_DIM
DA_V = DA_HEADS * DA_V_DIM
SW_Q = SW_Q_HEADS * HEAD_DIM
SW_KV = SW_KV_HEADS * HEAD_DIM
QKV_WIDTH = 3 * DA_QK + SW_Q + 2 * SW_KV
PEER_HEADS = 8
N_KEYS = 128
PEER_TOPK = 16
KEY_DIM = 128
EPS = 1e-6

V7X_VMEM_REQUEST_CAP = 60 * 1024 * 1024
BF16_ROWS = 16
DA_BLK = 512
DA_HEADS_PER_STEP = 2
SW_BLOCKS_PER_STEP = 8
PEER_UP_TE = 1024
FP8_ROW_BLOCK = 512
PEER_UP_CHUNKS = 4
PEER_UP_UNROLL = 2
PEER_UP_SUB_ROWS = 64


def _nbytes(shape, dtype):
    return math.prod(shape) * jnp.dtype(dtype).itemsize


def _params(semantics, block_bytes, scratch_bytes=0, flags=None):
    need = int(1.25 * (2 * block_bytes + scratch_bytes)) + (4 << 20)
    return pltpu.CompilerParams(dimension_semantics=semantics,
                                vmem_limit_bytes=min(need, V7X_VMEM_REQUEST_CAP), flags=flags)


def _pack_rows(x):
    return pltpu.bitcast(x, jnp.uint32)


def _unpack_rows(x, dtype=BF16):
    return pltpu.bitcast(x, dtype)


def _dot_nt(a, b):
    return lax.dot_general(a, b, (((1,), (1,)), ((), ())), preferred_element_type=F32)


def _rmsnorm_kernel(x_ref, g_ref, o_ref):
    x = x_ref[...]
    y = x * lax.rsqrt(jnp.mean(x * x, axis=-1, keepdims=True) + EPS)
    o_ref[...] = (y * g_ref[...]).astype(o_ref.dtype)


def _rmsnorm(x, g, out_dtype, rows=512):
    t, d = x.shape
    blocks = _nbytes((rows, d), F32) + _nbytes((rows, d), out_dtype)
    return pl.pallas_call(
        _rmsnorm_kernel,
        out_shape=jax.ShapeDtypeStruct((t, d), out_dtype),
        grid=(t // rows,),
        in_specs=[pl.BlockSpec((rows, d), lambda i: (i, 0)),
                  pl.BlockSpec((1, d), lambda i: (0, 0))],
        out_specs=pl.BlockSpec((rows, d), lambda i: (i, 0)),
        compiler_params=_params(("parallel",), blocks, _nbytes((rows, d), F32)),
        name="rmsnorm",
    )(x, g.reshape(1, d))


def _rmsnorm_fp8_kernel(x_ref, g_ref, o_ref, oc_ref, inv_ref, norm_ref):
    x = x_ref[...]
    y = x * lax.rsqrt(jnp.mean(x * x, axis=-1, keepdims=True) + EPS) * g_ref[...]
    o_ref[...] = y.astype(o_ref.dtype)
    norm_ref[...] = jnp.sqrt(jnp.sum(y * y, axis=-1, keepdims=True))
    amax = jnp.maximum(jnp.max(jnp.abs(y), axis=-1, keepdims=True), FP8_TINY)
    inv_ref[...] = amax * (1.0 / FP8_TARGET)
    y8 = y * (FP8_TARGET / amax)
    dc = oc_ref.shape[2]
    for k in range(oc_ref.shape[0]):
        oc_ref[k] = y8[:, k * dc:(k + 1) * dc].astype(oc_ref.dtype)


def _rmsnorm_fp8(x, g, chunks, rows=256):
    t, d = x.shape
    dc = d // chunks
    blocks = _nbytes((rows, d), F32) + _nbytes((rows, d), BF16) + _nbytes((rows, d), FP8) + _nbytes((rows, 128), F32)
    return pl.pallas_call(
        _rmsnorm_fp8_kernel,
        out_shape=(jax.ShapeDtypeStruct((t, d), BF16), jax.ShapeDtypeStruct((chunks, t, dc), FP8),
                   jax.ShapeDtypeStruct((t, 1), F32), jax.ShapeDtypeStruct((t, 1), F32)),
        grid=(t // rows,),
        in_specs=[pl.BlockSpec((rows, d), lambda i: (i, 0)),
                  pl.BlockSpec((1, d), lambda i: (0, 0))],
        out_specs=(pl.BlockSpec((rows, d), lambda i: (i, 0)),
                   pl.BlockSpec((chunks, rows, dc), lambda i: (0, i, 0)),
                   pl.BlockSpec((rows, 1), lambda i: (i, 0)),
                   pl.BlockSpec((rows, 1), lambda i: (i, 0))),
        compiler_params=_params(("parallel",), blocks, 2 * _nbytes((rows, d), F32)),
        name="rmsnorm_fp8",
    )(x, g.reshape(1, d))


CAST_BLOCK_BYTES = 8 << 20


def _cast_kernel(x_ref, o_ref):
    o_ref[...] = x_ref[...].astype(o_ref.dtype)


def _narrow(w, layer, col0=0, ncols=None, bc=None):
    _, r, c = w.shape
    ncols = c if ncols is None else ncols
    bc = ncols if bc is None else bc
    br = min(r, CAST_BLOCK_BYTES // (bc * 4))
    return pl.pallas_call(
        _cast_kernel,
        out_shape=jax.ShapeDtypeStruct((r, ncols), BF16),
        grid=(r // br, ncols // bc),
        in_specs=[pl.BlockSpec((None, br, bc), lambda i, j: (layer, i, col0 // bc + j))],
        out_specs=pl.BlockSpec((br, bc), lambda i, j: (i, j)),
        compiler_params=_params(("parallel", "parallel"), _nbytes((br, bc), F32) + _nbytes((br, bc), BF16),
                                _nbytes((br, bc), F32)),
        name="narrow",
    )(w)


def _pow2_scale(amax):
    shift = jnp.floor(jnp.log2(FP8_TARGET / jnp.maximum(amax, FP8_TINY)))
    return jnp.exp2(shift), jnp.exp2(-shift)


def _fp8_rows_kernel(x_ref, o_ref, inv_ref, norm_ref, *, layout):
    x = x_ref[...]
    norm = jnp.sqrt(jnp.max(jnp.sum(x * x, axis=1, keepdims=True), axis=0, keepdims=True))
    norm_ref[...] = jnp.broadcast_to(norm, norm_ref.shape)
    amax = jnp.max(jnp.max(jnp.abs(x), axis=0, keepdims=True), axis=1, keepdims=True)
    scale, inv = _pow2_scale(amax)
    inv_ref[...] = jnp.broadcast_to(inv, inv_ref.shape)
    y = x * scale
    if layout == "chunked":
        dc = o_ref.shape[2]
        for k in range(o_ref.shape[0]):
            o_ref[k] = y[:, k * dc:(k + 1) * dc].astype(o_ref.dtype)
    else:
        o_ref[...] = y.T.astype(o_ref.dtype)


def _fp8_rows(w, layer, layout, chunks=None, br=FP8_ROW_BLOCK):
    _, r, c = w.shape
    if layout == "chunked":
        shape = (chunks, r, c // chunks)
        out_spec = pl.BlockSpec((chunks, br, c // chunks), lambda i: (0, i, 0))
    else:
        shape = (c, r)
        out_spec = pl.BlockSpec((c, br), lambda i: (0, i))
    small = jax.ShapeDtypeStruct((r // br, 1, 128), F32)
    small_spec = pl.BlockSpec((1, 1, 128), lambda i: (i, 0, 0))
    out, inv, norm = pl.pallas_call(
        functools.partial(_fp8_rows_kernel, layout=layout),
        out_shape=(jax.ShapeDtypeStruct(shape, FP8), small, small),
        grid=(r // br,),
        in_specs=[pl.BlockSpec((None, br, c), lambda i: (layer, i, 0))],
        out_specs=(out_spec, small_spec, small_spec),
        compiler_params=_params(("parallel",), _nbytes((br, c), F32) + _nbytes((br, c), FP8), 2 * _nbytes((br, c), F32)),
        name="fp8_rows_" + layout,
    )(w)
    return out, inv[:, 0, 0], jnp.max(norm)


def _mm_scale_kernel(a_ref, b_ref, s_ref, o_ref):
    acc = jnp.dot(a_ref[...], b_ref[...], preferred_element_type=F32)
    o_ref[...] = (acc * s_ref[...]).astype(o_ref.dtype)


def _mm_sigmoid_kernel(a_ref, b_ref, o_ref):
    acc = jnp.dot(a_ref[...], b_ref[...], preferred_element_type=F32)
    o_ref[...] = jax.nn.sigmoid(acc).astype(o_ref.dtype)


def _mm_plain_kernel(a_ref, b_ref, o_ref):
    o_ref[...] = jnp.dot(a_ref[...], b_ref[...], preferred_element_type=F32).astype(o_ref.dtype)


def _mm_residual_kernel(a_ref, b_ref, x_ref, o_ref):
    o_ref[...] = x_ref[...] + jnp.dot(a_ref[...], b_ref[...], preferred_element_type=F32)


def _matmul(body, a, b, extra, extra_specs, out_dtype, bm, bn, name):
    m, k = a.shape
    n = b.shape[1]
    blocks = (_nbytes((bm, k), a.dtype) + _nbytes((k, bn), b.dtype) + _nbytes((bm, bn), out_dtype)
              + sum(_nbytes(s.block_shape, e.dtype) for s, e in zip(extra_specs, extra)))
    return pl.pallas_call(
        body,
        out_shape=jax.ShapeDtypeStruct((m, n), out_dtype),
        grid=(m // bm, n // bn),
        in_specs=[pl.BlockSpec((bm, k), lambda i, j: (i, 0)),
                  pl.BlockSpec((k, bn), lambda i, j: (0, j))] + list(extra_specs),
        out_specs=pl.BlockSpec((bm, bn), lambda i, j: (i, j)),
        compiler_params=_params(("parallel", "parallel"), blocks, _nbytes((bm, bn), F32)),
        name=name,
    )(a, b, *extra)


def _bias_kernel(tab_ref, o_ref, *, blk, window, rebase):
    h = pl.program_id(0)
    r = lax.broadcasted_iota(jnp.int32, (blk, blk), 0)
    c = lax.broadcasted_iota(jnp.int32, (blk, blk), 1)
    base = tab_ref[N_BUCKETS - 1, h] if rebase else 0.0
    for delta in (0, 1):
        rel = r - c + delta * blk
        n = jnp.maximum(rel, 0)
        nf = jnp.maximum(n, 1).astype(F32)
        large = MAX_EXACT + (jnp.log(nf / MAX_EXACT) / math.log(MAX_DIST / MAX_EXACT)
                             * (N_BUCKETS - MAX_EXACT)).astype(jnp.int32)
        large = jnp.minimum(large, N_BUCKETS - 1)
        bucket = jnp.where(n < MAX_EXACT, n, large)
        bias = jnp.zeros((blk, blk), F32)
        for b in range(N_BUCKETS):
            bias = jnp.where(bucket == b, tab_ref[b, h] - base, bias)
        mask = rel >= 0
        if window is not None:
            mask = mask & (rel < window)
        o_ref[0, delta] = jnp.where(mask, bias, NEG)


def _bias_tiles(tab, blk, window, rebase):
    heads = tab.shape[1]
    return pl.pallas_call(
        functools.partial(_bias_kernel, blk=blk, window=window, rebase=rebase),
        out_shape=jax.ShapeDtypeStruct((heads, 2, blk, blk), F32),
        grid=(heads,),
        in_specs=[pl.BlockSpec(memory_space=pltpu.SMEM)],
        out_specs=pl.BlockSpec((1, 2, blk, blk), lambda h: (h, 0, 0, 0)),
        compiler_params=_params(("parallel",), _nbytes((2, blk, blk), F32), 4 * _nbytes((blk, blk), F32)),
        name="bias_tiles",
    )(tab)


def _da_kernel(q_ref, k_ref, v_ref, bias_ref, lam_ref, g_ref, o_ref, *, lam_init):
    blk = q_ref.shape[0]
    qi = pl.program_id(2)
    lp = lam_ref[...]
    lam = (jnp.exp(jnp.sum(lp[0:1] * lp[1:2], axis=-1, keepdims=True))
           - jnp.exp(jnp.sum(lp[2:3] * lp[3:4], axis=-1, keepdims=True)) + lam_init)

    def softmax_pv(head, j, case):
        dims = slice((2 * head + j) * HEAD_DIM, (2 * head + j + 1) * HEAD_DIM)
        vdims = slice(head * DA_V_DIM, (head + 1) * DA_V_DIM)
        q = q_ref[:, dims]
        spans = [(slice(case * blk, (case + 1) * blk), bias_ref[head, 0])]
        if case >= 1:
            spans.append((slice((case - 1) * blk, case * blk), bias_ref[head, 1]))
        if case >= 2:
            spans.append((slice(0, (case - 1) * blk), None))
        scores = []
        for rows, bias in spans:
            s = _dot_nt(q, k_ref[rows, dims])
            scores.append(s if bias is None else s + bias)
        m = functools.reduce(jnp.maximum, [jnp.max(s, axis=-1, keepdims=True) for s in scores])
        probs = [jnp.exp(s - m) for s in scores]
        norm = sum(jnp.sum(p, axis=-1, keepdims=True) for p in probs)
        out = sum(jnp.dot(p.astype(BF16), v_ref[rows, vdims], preferred_element_type=F32)
                  for p, (rows, _) in zip(probs, spans))
        return out / norm

    for case in range(k_ref.shape[0] // blk):
        @pl.when(qi == case)
        def _(case=case):
            for head in range(q_ref.shape[1] // DA_V_DIM):
                o = softmax_pv(head, 0, case) - lam * softmax_pv(head, 1, case)
                y = o * lax.rsqrt(jnp.mean(o * o, axis=-1, keepdims=True) + EPS)
                o_ref[:, head * DA_V_DIM:(head + 1) * DA_V_DIM] = (
                    (y * g_ref[...]) * (1.0 - lam_init)).astype(o_ref.dtype)


def _diff_attention(qkv, bias, lam_p, subln_g, lam_init, batch, seq, heads=DA_HEADS_PER_STEP):
    blk = DA_BLK
    nq = seq // blk
    width = heads * DA_V_DIM
    blocks = (2 * _nbytes((blk, width), BF16) + 2 * _nbytes((seq, width), BF16)
              + _nbytes((heads, 2, blk, blk), F32))
    scratch = 6 * heads * _nbytes((blk, seq), F32)
    k_col0 = DA_QK // width
    v_col0 = 2 * DA_QK // width
    return pl.pallas_call(
        functools.partial(_da_kernel, lam_init=lam_init),
        out_shape=jax.ShapeDtypeStruct((batch * seq, DA_V), BF16),
        grid=(batch, DA_HEADS // heads, nq),
        in_specs=[pl.BlockSpec((blk, width), lambda b, h, i: (b * nq + i, h)),
                  pl.BlockSpec((seq, width), lambda b, h, i: (b, k_col0 + h)),
                  pl.BlockSpec((seq, width), lambda b, h, i: (b, v_col0 + h)),
                  pl.BlockSpec((heads, 2, blk, blk), lambda b, h, i: (h, 0, 0, 0)),
                  pl.BlockSpec((4, HEAD_DIM), lambda b, h, i: (0, 0)),
                  pl.BlockSpec((1, DA_V_DIM), lambda b, h, i: (0, 0))],
        out_specs=pl.BlockSpec((blk, width), lambda b, h, i: (b * nq + i, h)),
        compiler_params=_params(("parallel", "parallel", "parallel"), blocks, scratch),
        name="diff_attention",
    )(qkv, qkv, qkv, bias, lam_p, subln_g.reshape(1, DA_V_DIM))


def _swa_kernel(q_ref, kc_ref, kp_ref, vc_ref, vp_ref, bias_ref, sink_ref, o_ref):
    n = pl.program_id(1)
    is_prev = lax.broadcasted_iota(jnp.int32, (1, 2 * WINDOW), 1) < WINDOW
    no_prev = jnp.where(is_prev & (n == 0), NEG, 0.0).astype(F32)
    for hk in range(SW_KV_HEADS):
        cols = slice(hk * HEAD_DIM, (hk + 1) * HEAD_DIM)
        heads = [slice((hk * SW_GROUP + g) * HEAD_DIM, (hk * SW_GROUP + g + 1) * HEAD_DIM) for g in range(SW_GROUP)]
        keys = jnp.concatenate([kp_ref[:, cols], kc_ref[:, cols]], axis=0)
        values = jnp.concatenate([vp_ref[:, cols], vc_ref[:, cols]], axis=0)
        for i in range(q_ref.shape[0] // WINDOW):
            rows = slice(i * WINDOW, (i + 1) * WINDOW)
            band = slice(i * WINDOW, (i + 2) * WINDOW)
            q = jnp.concatenate([q_ref[rows, hd] for hd in heads], axis=0)
            s = _dot_nt(q, keys[band]) + bias_ref[hk]
            if i == 0:
                s = s + no_prev
            sink = sink_ref[hk]
            m = jnp.maximum(jnp.max(s, axis=-1, keepdims=True), sink)
            e = jnp.exp(s - m)
            den = jnp.sum(e, axis=-1, keepdims=True) + jnp.exp(sink - m)
            o = jnp.dot(e.astype(BF16), values[band], preferred_element_type=F32) / den
            for g, hd in enumerate(heads):
                o_ref[rows, hd] = o[g * WINDOW:(g + 1) * WINDOW, :].astype(o_ref.dtype)


def _sliding_attention(qkv, bias_tiles, sinks, batch, seq, group=SW_BLOCKS_PER_STEP):
    nb = seq // WINDOW
    ng = nb // group
    span = group * WINDOW
    q_col = 3 * DA_QK // SW_Q
    k_col = (3 * DA_QK + SW_Q) // SW_KV
    v_col = k_col + 1
    cur = lambda b, n: b * ng + n
    prev = lambda b, n: b * nb + jnp.maximum(n * group - 1, 0)
    rows = SW_GROUP * WINDOW
    bias = bias_tiles.reshape(SW_KV_HEADS, SW_GROUP, 2, WINDOW, WINDOW)[:, :, ::-1]
    bias = bias.transpose(0, 1, 3, 2, 4).reshape(SW_KV_HEADS, rows, 2 * WINDOW)
    sink_cols = jnp.repeat(sinks.astype(F32).reshape(SW_KV_HEADS, SW_GROUP), WINDOW, axis=1).reshape(
        SW_KV_HEADS, rows, 1)
    blocks = (2 * _nbytes((span, SW_Q), BF16) + 2 * _nbytes((span + WINDOW, SW_KV), BF16)
              + _nbytes((SW_KV_HEADS, rows, 2 * WINDOW), F32) + _nbytes((SW_KV_HEADS, rows, 128), F32))
    return pl.pallas_call(
        _swa_kernel,
        out_shape=jax.ShapeDtypeStruct((batch * seq, SW_Q), BF16),
        grid=(batch, ng),
        in_specs=[pl.BlockSpec((span, SW_Q), lambda b, n: (cur(b, n), q_col)),
                  pl.BlockSpec((span, SW_KV), lambda b, n: (cur(b, n), k_col)),
                  pl.BlockSpec((WINDOW, SW_KV), lambda b, n: (prev(b, n), k_col)),
                  pl.BlockSpec((span, SW_KV), lambda b, n: (cur(b, n), v_col)),
                  pl.BlockSpec((WINDOW, SW_KV), lambda b, n: (prev(b, n), v_col)),
                  pl.BlockSpec((SW_KV_HEADS, rows, 2 * WINDOW), lambda b, n: (0, 0, 0)),
                  pl.BlockSpec((SW_KV_HEADS, rows, 1), lambda b, n: (0, 0, 0))],
        out_specs=pl.BlockSpec((span, SW_Q), lambda b, n: (cur(b, n), 0)),
        compiler_params=_params(("parallel", "parallel"), blocks, 16 * group * _nbytes((rows, 2 * WINDOW), F32)),
        name="sliding_attention",
    )(qkv, qkv, qkv, qkv, qkv, bias, sink_cols)


def _merge_kernel(oa_ref, ob_ref, wa_ref, wb_ref, ga_ref, gb_ref, o_ref):
    a = jnp.dot(oa_ref[...], wa_ref[...], preferred_element_type=F32)
    b = jnp.dot(ob_ref[...], wb_ref[...], preferred_element_type=F32)
    o_ref[...] = (ga_ref[...].astype(F32) * a + gb_ref[...].astype(F32) * b).astype(o_ref.dtype)


def _merge(o_a, o_b, w_a, w_b, gates, bm=1024, bn=1024):
    t, ka = o_a.shape
    kb = o_b.shape[1]
    d = w_a.shape[1]
    nj = d // bn
    blocks = (_nbytes((bm, ka), BF16) + _nbytes((bm, kb), BF16) + _nbytes((ka, bn), BF16)
              + _nbytes((kb, bn), BF16) + 2 * _nbytes((bm, bn), gates.dtype) + _nbytes((bm, bn), BF16))
    return pl.pallas_call(
        _merge_kernel,
        out_shape=jax.ShapeDtypeStruct((t, d), BF16),
        grid=(t // bm, nj),
        in_specs=[pl.BlockSpec((bm, ka), lambda i, j: (i, 0)),
                  pl.BlockSpec((bm, kb), lambda i, j: (i, 0)),
                  pl.BlockSpec((ka, bn), lambda i, j: (0, j)),
                  pl.BlockSpec((kb, bn), lambda i, j: (0, j)),
                  pl.BlockSpec((bm, bn), lambda i, j: (i, j)),
                  pl.BlockSpec((bm, bn), lambda i, j: (i, nj + j))],
        out_specs=pl.BlockSpec((bm, bn), lambda i, j: (i, j)),
        compiler_params=_params(("parallel", "parallel"), blocks, 2 * _nbytes((bm, bn), F32)),
        name="merge",
    )(o_a, o_b, w_a, w_b, gates, gates)


def _sort_pairs(n):
    pairs = []

    def merge(lo, hi, r):
        step = r * 2
        if step < hi - lo:
            merge(lo, hi, step)
            merge(lo + r, hi, step)
            pairs.extend((i, i + r) for i in range(lo + r, hi - r, step))
        else:
            pairs.append((lo, lo + r))

    def sort(lo, hi):
        if hi - lo >= 1:
            mid = lo + (hi - lo) // 2
            sort(lo, mid)
            sort(mid + 1, hi)
            merge(lo, hi, 1)

    sort(0, n - 1)
    return pairs


_SORT16 = _sort_pairs(PEER_TOPK)


def _sort_desc(xs):
    xs = list(xs)
    for i, j in _SORT16:
        xs[i], xs[j] = jnp.maximum(xs[i], xs[j]), jnp.minimum(xs[i], xs[j])
    return xs


def _merge_top(a, b):
    k = PEER_TOPK
    xs = [jnp.maximum(a[i], b[k - 1 - i]) for i in range(k)]
    d = k // 2
    while d >= 1:
        for i in range(k):
            if not i & d:
                xs[i], xs[i + d] = jnp.maximum(xs[i], xs[i + d]), jnp.minimum(xs[i], xs[i + d])
        d //= 2
    return xs


def _top16_over_rows(s):
    groups = [s[a * 8:(a + 1) * 8, :] for a in range(s.shape[0] // 8)]
    xs = _sort_desc(groups)
    for shift in (4, 2, 1):
        xs = _merge_top(xs, [pltpu.roll(x, shift, 0) for x in xs])
    return xs


def _count_leading(pred, values):
    n = len(values)

    def pick(lo, hi, taken):
        if not taken:
            return values[(lo + hi) // 2 - 1]
        mid = (lo + hi) // 2
        return jnp.where(taken[0], pick(mid, hi, taken[1:]), pick(lo, mid, taken[1:]))

    taken = []
    step = n // 2
    while step >= 1:
        taken.append(pred(pick(0, n, taken)))
        step //= 2
    total = None
    for i, t in enumerate(taken):
        part = jnp.where(t, float(n >> (i + 1)), 0.0)
        total = part if total is None else total + part
    return total + jnp.where(pred(values[n - 1]), 1.0, 0.0)


def _route_kernel(q_ref, k1_ref, k2_ref, cnt_ref, e1_ref, rank_ref, e2_ref, top_ref):
    q = q_ref[...]
    s1 = _dot_nt(k1_ref[...], q[:, :KEY_DIM])
    s2 = _dot_nt(k2_ref[...], q[:, KEY_DIM:])
    v1 = _top16_over_rows(s1)
    v2 = _top16_over_rows(s2)
    k = PEER_TOPK
    top = [v1[0] + v2[b] for b in range(k)]
    rest = [v1[a] + v2[b] for a in range(1, k) for b in range(k) if (a + 1) * (b + 1) <= k]
    pad = jnp.full(top[0].shape, -jnp.inf, F32)
    rest = rest + [pad] * (-len(rest) % k)
    for g in range(len(rest) // k):
        top = _merge_top(top, _sort_desc(rest[g * k:(g + 1) * k]))
    z = jnp.ones_like(top[0])
    for c in top[1:]:
        z = z + jnp.exp(c - top[0])
    tau = top[k - 1][0:1]
    best = [v[0:1] for v in v2]
    cnt = _count_leading(lambda b: s1 + b >= tau, best)
    rank = _count_leading(lambda b: b > s2, best)
    cnt_ref[0] = cnt
    rank_ref[0] = _pack_rows(rank.astype(BF16))
    e1_ref[0] = jnp.exp(s1 - v1[0][0:1]) / z[0:1]
    e2_ref[0] = _pack_rows(jnp.exp(s2 - v2[0][0:1]).astype(BF16))
    top_ref[0] = 1.0 / z[0:1]


def _route(q, k1, k2, tm=1024):
    t = q.shape[0]
    wide = jax.ShapeDtypeStruct((PEER_HEADS, N_KEYS, t), F32)
    narrow = jax.ShapeDtypeStruct((PEER_HEADS, N_KEYS // 2, t), jnp.uint32)
    spec = pl.BlockSpec((1, N_KEYS, tm), lambda i, h: (h, 0, i))
    narrow_spec = pl.BlockSpec((1, N_KEYS // 2, tm), lambda i, h: (h, 0, i))
    blocks = _nbytes((tm, 2 * KEY_DIM), BF16) + 3 * _nbytes((N_KEYS, tm), F32)
    return pl.pallas_call(
        _route_kernel,
        out_shape=(wide, wide, narrow, narrow, jax.ShapeDtypeStruct((PEER_HEADS, 1, t), F32)),
        grid=(t // tm, PEER_HEADS),
        in_specs=[pl.BlockSpec((tm, 2 * KEY_DIM), lambda i, h: (i, h)),
                  pl.BlockSpec((N_KEYS, KEY_DIM), lambda i, h: (0, 0)),
                  pl.BlockSpec((N_KEYS, KEY_DIM), lambda i, h: (0, 0))],
        out_specs=(spec, spec, narrow_spec, narrow_spec, pl.BlockSpec((1, 1, tm), lambda i, h: (h, 0, i))),
        compiler_params=_params(("parallel", "parallel"), blocks, 24 * _nbytes((N_KEYS, tm), F32)),
        name="peer_route",
    )(q, k1, k2)


def _peer_up_kernel(inv_u_ref, inv_v_ref, u_ref, h_ref, inv_ref, wscale_ref, cnt_ref, e1_ref, rank_ref, e2_ref,
                    o_ref, act_a, act_b, *, expert_tiles):
    s = pl.program_id(0)
    first_scale_block = (jnp.maximum(s - 1, 0) % expert_tiles) * (act_a.shape[0] // FP8_ROW_BLOCK)

    @pl.when(s == 0)
    def _():
        act_a[...] = jnp.zeros(act_a.shape, F32)
        act_b[...] = jnp.zeros(act_b.shape, F32)

    sub = PEER_UP_SUB_ROWS
    reps = sub // BF16_ROWS
    chunks = u_ref.shape[0]
    blocks_per_chunk = act_a.shape[0] // N_KEYS // chunks

    def epilogue_block(act_old, r):
        scale_block = first_scale_block + r // (FP8_ROW_BLOCK // N_KEYS)
        inv_u, inv_v = inv_u_ref[scale_block], inv_v_ref[scale_block]
        cnt_rows = [cnt_ref[h, pl.ds(r, 1), :] for h in range(PEER_HEADS)]
        e1_rows = [e1_ref[h, pl.ds(r, 1), :] * (wscale_ref[...] * inv_v) for h in range(PEER_HEADS)]
        for c in range(act_old.shape[1] // 128):
            cols = slice(c * 128, (c + 1) * 128)
            for part in range(N_KEYS // sub):
                act_rows = pl.ds(pl.multiple_of(r * N_KEYS + part * sub, sub), sub)
                in_rows = slice(part * sub // 2, (part + 1) * sub // 2)
                out_rows = pl.ds(pl.multiple_of((r * N_KEYS + part * sub) // 4, sub // 4), sub // 4)
                act = act_old[act_rows, cols]
                act_old[act_rows, cols] = jnp.zeros_like(act)
                act = act * (inv_ref[:, cols] * inv_u)
                gate = None
                for h in range(PEER_HEADS):
                    cnt = jnp.tile(jnp.broadcast_to(cnt_rows[h][:, cols], (BF16_ROWS, 128)).astype(BF16), (reps, 1))
                    e1 = jnp.tile(jnp.broadcast_to(e1_rows[h][:, cols], (BF16_ROWS, 128)).astype(BF16), (reps, 1))
                    routed = _unpack_rows(rank_ref[h, in_rows, cols]) < cnt
                    term = jnp.where(routed, _unpack_rows(e2_ref[h, in_rows, cols]) * e1, jnp.zeros((), BF16))
                    gate = term if gate is None else gate + term
                w = jax.nn.gelu(act).astype(BF16) * gate
                o_ref[out_rows, cols] = _pack_rows(w.astype(FP8))

    def step(act_new, act_old):
        def chunk(k, carry):
            act_new[...] += _dot_nt(u_ref[k], h_ref[k])
            for b in range(blocks_per_chunk):
                epilogue_block(act_old, k * blocks_per_chunk + b)
            return carry

        lax.fori_loop(0, chunks, chunk, 0, unroll=PEER_UP_UNROLL)

    @pl.when(s % 2 == 0)
    def _():
        step(act_a, act_b)

    @pl.when(s % 2 == 1)
    def _():
        step(act_b, act_a)


def _peer_up(u, inv_u, inv_v, hn, inv_h, w_scale, cnt, e1, rank, e2, te=PEER_UP_TE, tm=512):
    chunks, n_exp, dc = u.shape
    t = hn.shape[1]
    rows = te // N_KEYS
    assert rows % chunks == 0
    nj = n_exp // te
    steps = (t // tm) * nj

    def tile_of(step):
        return step // nj, step % nj

    def now(s):
        return tile_of(jnp.minimum(s, steps - 1))

    def lag(s):
        return tile_of(jnp.maximum(s - 1, 0))

    row_spec = pl.BlockSpec((PEER_HEADS, rows, tm), lambda s: (0, lag(s)[1], lag(s)[0]))
    full_spec = pl.BlockSpec((PEER_HEADS, N_KEYS // 2, tm), lambda s: (0, 0, lag(s)[0]))
    blocks = (_nbytes((chunks, te, dc), FP8) + _nbytes((chunks, tm, dc), FP8)
              + 2 * _nbytes((PEER_HEADS, N_KEYS, tm), BF16)
              + 2 * _nbytes((PEER_HEADS, rows, tm), F32) + _nbytes((te, tm), BF16))
    smem = pl.BlockSpec(memory_space=pltpu.SMEM)
    return pl.pallas_call(
        functools.partial(_peer_up_kernel, expert_tiles=nj),
        out_shape=jax.ShapeDtypeStruct((n_exp // 4, t), jnp.uint32),
        grid=(steps + 1,),
        in_specs=[smem, smem,
                  pl.BlockSpec((chunks, te, dc), lambda s: (0, now(s)[1], 0)),
                  pl.BlockSpec((chunks, tm, dc), lambda s: (0, now(s)[0], 0)),
                  pl.BlockSpec((1, tm), lambda s: (0, lag(s)[0])),
                  pl.BlockSpec((1, tm), lambda s: (0, lag(s)[0])),
                  row_spec, row_spec, full_spec, full_spec],
        out_specs=pl.BlockSpec((te // 4, tm), lambda s: (lag(s)[1], lag(s)[0])),
        scratch_shapes=[pltpu.VMEM((te, tm), F32), pltpu.VMEM((te, tm), F32)],
        compiler_params=_params(("arbitrary",), blocks, 3 * _nbytes((te, tm), F32)),
        name="peer_up",
    )(inv_u, inv_v, u, hn, inv_h, w_scale, cnt, e1, rank, e2)


def _peer_down_kernel(vt_ref, w_ref, inv_ref, x_ref, o_ref, acc_ref):
    kk = pl.program_id(2)

    @pl.when(kk == 0)
    def _():
        acc_ref[...] = jnp.zeros(acc_ref.shape, F32)

    acc_ref[...] += jnp.dot(vt_ref[...], _unpack_rows(w_ref[...], FP8), preferred_element_type=F32)

    @pl.when(kk == pl.num_programs(2) - 1)
    def _():
        o_ref[...] = x_ref[...] + (acc_ref[...] * inv_ref[...]).T


def _peer_down(vt, wt, inv_w_scale, x, bd=1024, bt=1024, tk=4096):
    d, n_exp = vt.shape
    t = wt.shape[1]
    blocks = (_nbytes((bd, tk), FP8) + _nbytes((tk, bt), FP8) + 2 * _nbytes((bt, bd), F32))
    return pl.pallas_call(
        _peer_down_kernel,
        out_shape=jax.ShapeDtypeStruct((t, d), F32),
        grid=(d // bd, t // bt, n_exp // tk),
        in_specs=[pl.BlockSpec((bd, tk), lambda i, j, k: (i, k)),
                  pl.BlockSpec((tk // 4, bt), lambda i, j, k: (k, j)),
                  pl.BlockSpec((1, bt), lambda i, j, k: (0, j)),
                  pl.BlockSpec((bt, bd), lambda i, j, k: (j, i))],
        out_specs=pl.BlockSpec((bt, bd), lambda i, j, k: (j, i)),
        scratch_shapes=[pltpu.VMEM((bd, bt), F32)],
        compiler_params=_params(("parallel", "parallel", "arbitrary"), blocks, 3 * _nbytes((bd, bt), F32)),
        name="peer_down",
    )(vt, wt, inv_w_scale, x)


def _w_scale(h_norm, u_norm, top_weight, inv_v):
    bound = h_norm.reshape(1, -1) * u_norm * jnp.sum(top_weight, axis=0) * jnp.max(inv_v)
    return _pow2_scale(bound)


def _ple_kernel(h_ref, wg_ref, p_ref, wp_ref, x_ref, o_ref):
    gate = jax.nn.sigmoid(jnp.dot(h_ref[...], wg_ref[...], preferred_element_type=F32))
    emb = jnp.dot(p_ref[...], wp_ref[...], preferred_element_type=F32)
    o_ref[...] = x_ref[...] + gate * emb


def _ple(hp, w_gate, p, w_proj, x, bm=1024, bn=512):
    t, d = hp.shape
    pd = p.shape[1]
    n = w_gate.shape[1]
    blocks = (_nbytes((bm, d), BF16) + _nbytes((d, bn), BF16) + _nbytes((bm, pd), BF16)
              + _nbytes((pd, bn), BF16) + 2 * _nbytes((bm, bn), F32))
    return pl.pallas_call(
        _ple_kernel,
        out_shape=jax.ShapeDtypeStruct((t, n), F32),
        grid=(t // bm, n // bn),
        in_specs=[pl.BlockSpec((bm, d), lambda i, j: (i, 0)),
                  pl.BlockSpec((d, bn), lambda i, j: (0, j)),
                  pl.BlockSpec((bm, pd), lambda i, j: (i, 0)),
                  pl.BlockSpec((pd, bn), lambda i, j: (0, j)),
                  pl.BlockSpec((bm, bn), lambda i, j: (i, j))],
        out_specs=pl.BlockSpec((bm, bn), lambda i, j: (i, j)),
        compiler_params=_params(("parallel", "parallel"), blocks, 2 * _nbytes((bm, bn), F32)),
        name="ple",
    )(hp, w_gate, p, w_proj, x)


def _qkv_column_scale():
    s = HEAD_DIM ** -0.5
    parts = [(DA_QK, s), (DA_QK, 1.0), (DA_V, 1.0), (SW_Q, s), (SW_KV, 1.0), (SW_KV, 1.0)]
    return jnp.concatenate([jnp.full((1, w), v, F32) for w, v in parts], axis=1)


@jax.jit
def kernel(x, p, positions, rel_bias, norm_mix, w_in, da_lambda, da_subln, sw_sinks, w_br_a, w_br_b, w_out,
           norm_ffn, peer_wq, peer_k1, peer_k2, peer_u, peer_v, norm_ple, ple_gate, ple_proj, norm_final):
    del positions
    batch, seq, d = x.shape
    t = batch * seq
    depth = w_in.shape[0]
    xf = x.reshape(t, d)
    da_bias = _bias_tiles(rel_bias[:, :DA_HEADS], DA_BLK, None, True)
    sw_bias = _bias_tiles(rel_bias[:, DA_HEADS:], WINDOW, WINDOW, False)
    col_scale = _qkv_column_scale()
    tile = pl.BlockSpec((1, 1024), lambda i, j: (0, j))
    for i in range(depth):
        lam_init = 0.8 - 0.6 * math.exp(-0.3 * i)
        h = _rmsnorm(xf, norm_mix[i], BF16)
        qkv = _matmul(_mm_scale_kernel, h, _narrow(w_in, i, 0, QKV_WIDTH, 1024), [col_scale], [tile],
                      BF16, 1024, 1024, "proj_qkv")
        gates = _matmul(_mm_sigmoid_kernel, h, _narrow(w_in, i, QKV_WIDTH, w_in.shape[2] - QKV_WIDTH, 1024),
                        [], [], BF16, 1024, 1024, "proj_gates")
        o_a = _diff_attention(qkv, da_bias, da_lambda[i], da_subln[i], lam_init, batch, seq)
        o_b = _sliding_attention(qkv, sw_bias, sw_sinks[i], batch, seq)
        merged = _merge(o_a, o_b, _narrow(w_br_a, i), _narrow(w_br_b, i), gates)
        xf = _matmul(_mm_residual_kernel, merged, _narrow(w_out, i), [xf],
                     [pl.BlockSpec((1024, 512), lambda i, j: (i, j))], F32, 1024, 512, "proj_out")
        hn, hn8, hn_inv, hn_norm = _rmsnorm_fp8(xf, norm_ffn[i], PEER_UP_CHUNKS)
        q = _matmul(_mm_plain_kernel, hn, _narrow(peer_wq, i), [], [], BF16, 1024, 1024, "peer_query")
        cnt, e1, rank, e2, top = _route(q, peer_k1[i].astype(BF16), peer_k2[i].astype(BF16))
        u8, inv_u, u_norm = _fp8_rows(peer_u, i, "chunked", PEER_UP_CHUNKS)
        vt8, inv_v, _ = _fp8_rows(peer_v, i, "transposed")
        w_scale, inv_w_scale = _w_scale(hn_norm, u_norm, top, inv_v)
        wt = _peer_up(u8, inv_u, inv_v, hn8, hn_inv.reshape(1, t), w_scale, cnt, e1, rank, e2)
        xf = _peer_down(vt8, wt, inv_w_scale, xf)
        hp = _rmsnorm(xf, norm_ple[i], BF16)
        xf = _ple(hp, _narrow(ple_gate, i), p[i].reshape(t, -1).astype(BF16), ple_proj[i].astype(BF16), xf)
    return _rmsnorm(xf, norm_final, F32).reshape(batch, seq, d)
```

```python
import functools
import math

import jax
import jax.numpy as jnp
from jax import lax
from jax.experimental import pallas as pl
from jax.experimental.pallas import tpu as pltpu

F32 = jnp.float32
BF16 = jnp.bfloat16
FP8 = jnp.float8_e4m3fn
FP8_TARGET = 240.0
FP8_TINY = 1e-30

HEAD_DIM = 128
DA_HEADS = 8
DA_V_DIM = 2 * HEAD_DIM
SW_Q_HEADS = 16
SW_KV_HEADS = 4
SW_GROUP = SW_Q_HEADS // SW_KV_HEADS
WINDOW = 128
N_BUCKETS = 32
MAX_EXACT = N_BUCKETS // 2
MAX_DIST = 128
NEG = -1e30
DA_QK = DA_HEADS * 2 * HEAD_DIM
DA_V = DA_HEADS * DA_V_DIM
SW_Q = SW_Q_HEADS * HEAD_DIM
SW_KV = SW_KV_HEADS * HEAD_DIM
QKV_WIDTH = 3 * DA_QK + SW_Q + 2 * SW_KV
PEER_HEADS = 8
N_KEYS = 128
PEER_TOPK = 16
KEY_DIM = 128
EPS = 1e-6

V7X_VMEM_REQUEST_CAP = 60 * 1024 * 1024
BF16_ROWS = 16
DA_BLK = 512
DA_HEADS_PER_STEP = 2
SW_BLOCKS_PER_STEP = 8
PEER_UP_TE = 1024
FP8_ROW_BLOCK = 512
PEER_UP_CHUNKS = 4
PEER_UP_UNROLL = 2
PEER_UP_SUB_ROWS = 64


def _nbytes(shape, dtype):
    return math.prod(shape) * jnp.dtype(dtype).itemsize


def _params(semantics, block_bytes, scratch_bytes=0, flags=None):
    need = int(1.25 * (2 * block_bytes + scratch_bytes)) + (4 << 20)
    return pltpu.CompilerParams(dimension_semantics=semantics,
                                vmem_limit_bytes=min(need, V7X_VMEM_REQUEST_CAP), flags=flags)


def _pack_rows(x):
    return pltpu.bitcast(x, jnp.uint32)


def _unpack_rows(x, dtype=BF16):
    return pltpu.bitcast(x, dtype)


def _dot_nt(a, b):
    return lax.dot_general(a, b, (((1,), (1,)), ((), ())), preferred_element_type=F32)


def _rmsnorm_kernel(x_ref, g_ref, o_ref):
    x = x_ref[...]
    y = x * lax.rsqrt(jnp.mean(x * x, axis=-1, keepdims=True) + EPS)
    o_ref[...] = (y * g_ref[...]).astype(o_ref.dtype)


def _rmsnorm(x, g, out_dtype, rows=512):
    t, d = x.shape
    blocks = _nbytes((rows, d), F32) + _nbytes((rows, d), out_dtype)
    return pl.pallas_call(
        _rmsnorm_kernel,
        out_shape=jax.ShapeDtypeStruct((t, d), out_dtype),
        grid=(t // rows,),
        in_specs=[pl.BlockSpec((rows, d), lambda i: (i, 0)),
                  pl.BlockSpec((1, d), lambda i: (0, 0))],
        out_specs=pl.BlockSpec((rows, d), lambda i: (i, 0)),
        compiler_params=_params(("parallel",), blocks, _nbytes((rows, d), F32)),
        name="rmsnorm",
    )(x, g.reshape(1, d))


def _rmsnorm_fp8_kernel(x_ref, g_ref, o_ref, oc_ref, inv_ref, norm_ref):
    x = x_ref[...]
    y = x * lax.rsqrt(jnp.mean(x * x, axis=-1, keepdims=True) + EPS) * g_ref[...]
    o_ref[...] = y.astype(o_ref.dtype)
    norm_ref[...] = jnp.sqrt(jnp.sum(y * y, axis=-1, keepdims=True))
    amax = jnp.maximum(jnp.max(jnp.abs(y), axis=-1, keepdims=True), FP8_TINY)
    inv_ref[...] = amax * (1.0 / FP8_TARGET)
    y8 = y * (FP8_TARGET / amax)
    dc = oc_ref.shape[2]
    for k in range(oc_ref.shape[0]):
        oc_ref[k] = y8[:, k * dc:(k + 1) * dc].astype(oc_ref.dtype)


def _rmsnorm_fp8(x, g, chunks, rows=256):
    t, d = x.shape
    dc = d // chunks
    blocks = _nbytes((rows, d), F32) + _nbytes((rows, d), BF16) + _nbytes((rows, d), FP8) + _nbytes((rows, 128), F32)
    return pl.pallas_call(
        _rmsnorm_fp8_kernel,
        out_shape=(jax.ShapeDtypeStruct((t, d), BF16), jax.ShapeDtypeStruct((chunks, t, dc), FP8),
                   jax.ShapeDtypeStruct((t, 1), F32), jax.ShapeDtypeStruct((t, 1), F32)),
        grid=(t // rows,),
        in_specs=[pl.BlockSpec((rows, d), lambda i: (i, 0)),
                  pl.BlockSpec((1, d), lambda i: (0, 0))],
        out_specs=(pl.BlockSpec((rows, d), lambda i: (i, 0)),
                   pl.BlockSpec((chunks, rows, dc), lambda i: (0, i, 0)),
                   pl.BlockSpec((rows, 1), lambda i: (i, 0)),
                   pl.BlockSpec((rows, 1), lambda i: (i, 0))),
        compiler_params=_params(("parallel",), blocks, 2 * _nbytes((rows, d), F32)),
        name="rmsnorm_fp8",
    )(x, g.reshape(1, d))


CAST_BLOCK_BYTES = 8 << 20


def _cast_kernel(x_ref, o_ref):
    o_ref[...] = x_ref[...].astype(o_ref.dtype)


def _narrow(w, layer, col0=0, ncols=None, bc=None):
    _, r, c = w.shape
    ncols = c if ncols is None else ncols
    bc = ncols if bc is None else bc
    br = min(r, CAST_BLOCK_BYTES // (bc * 4))
    return pl.pallas_call(
        _cast_kernel,
        out_shape=jax.ShapeDtypeStruct((r, ncols), BF16),
        grid=(r // br, ncols // bc),
        in_specs=[pl.BlockSpec((None, br, bc), lambda i, j: (layer, i, col0 // bc + j))],
        out_specs=pl.BlockSpec((br, bc), lambda i, j: (i, j)),
        compiler_params=_params(("parallel", "parallel"), _nbytes((br, bc), F32) + _nbytes((br, bc), BF16),
                                _nbytes((br, bc), F32)),
        name="narrow",
    )(w)


def _pow2_scale(amax):
    shift = jnp.floor(jnp.log2(FP8_TARGET / jnp.maximum(amax, FP8_TINY)))
    return jnp.exp2(shift), jnp.exp2(-shift)


def _fp8_rows_kernel(x_ref, o_ref, inv_ref, norm_ref, *, layout):
    x = x_ref[...]
    norm = jnp.sqrt(jnp.max(jnp.sum(x * x, axis=1, keepdims=True), axis=0, keepdims=True))
    norm_ref[...] = jnp.broadcast_to(norm, norm_ref.shape)
    amax = jnp.max(jnp.max(jnp.abs(x), axis=0, keepdims=True), axis=1, keepdims=True)
    scale, inv = _pow2_scale(amax)
    inv_ref[...] = jnp.broadcast_to(inv, inv_ref.shape)
    y = x * scale
    if layout == "chunked":
        dc = o_ref.shape[2]
        for k in range(o_ref.shape[0]):
            o_ref[k] = y[:, k * dc:(k + 1) * dc].astype(o_ref.dtype)
    else:
        o_ref[...] = y.T.astype(o_ref.dtype)


def _fp8_rows(w, layer, layout, chunks=None, br=FP8_ROW_BLOCK):
    _, r, c = w.shape
    if layout == "chunked":
        shape = (chunks, r, c // chunks)
        out_spec = pl.BlockSpec((chunks, br, c // chunks), lambda i: (0, i, 0))
    else:
        shape = (c, r)
        out_spec = pl.BlockSpec((c, br), lambda i: (0, i))
    small = jax.ShapeDtypeStruct((r // br, 1, 128), F32)
    small_spec = pl.BlockSpec((1, 1, 128), lambda i: (i, 0, 0))
    out, inv, norm = pl.pallas_call(
        functools.partial(_fp8_rows_kernel, layout=layout),
        out_shape=(jax.ShapeDtypeStruct(shape, FP8), small, small),
        grid=(r // br,),
        in_specs=[pl.BlockSpec((None, br, c), lambda i: (layer, i, 0))],
        out_specs=(out_spec, small_spec, small_spec),
        compiler_params=_params(("parallel",), _nbytes((br, c), F32) + _nbytes((br, c), FP8), 2 * _nbytes((br, c), F32)),
        name="fp8_rows_" + layout,
    )(w)
    return out, inv[:, 0, 0], jnp.max(norm)


def _mm_scale_kernel(a_ref, b_ref, s_ref, o_ref):
    acc = jnp.dot(a_ref[...], b_ref[...], preferred_element_type=F32)
    o_ref[...] = (acc * s_ref[...]).astype(o_ref.dtype)


def _mm_sigmoid_kernel(a_ref, b_ref, o_ref):
    acc = jnp.dot(a_ref[...], b_ref[...], preferred_element_type=F32)
    o_ref[...] = jax.nn.sigmoid(acc).astype(o_ref.dtype)


def _mm_plain_kernel(a_ref, b_ref, o_ref):
    o_ref[...] = jnp.dot(a_ref[...], b_ref[...], preferred_element_type=F32).astype(o_ref.dtype)


def _mm_residual_kernel(a_ref, b_ref, x_ref, o_ref):
    o_ref[...] = x_ref[...] + jnp.dot(a_ref[...], b_ref[...], preferred_element_type=F32)


def _matmul(body, a, b, extra, extra_specs, out_dtype, bm, bn, name):
    m, k = a.shape
    n = b.shape[1]
    blocks = (_nbytes((bm, k), a.dtype) + _nbytes((k, bn), b.dtype) + _nbytes((bm, bn), out_dtype)
              + sum(_nbytes(s.block_shape, e.dtype) for s, e in zip(extra_specs, extra)))
    return pl.pallas_call(
        body,
        out_shape=jax.ShapeDtypeStruct((m, n), out_dtype),
        grid=(m // bm, n // bn),
        in_specs=[pl.BlockSpec((bm, k), lambda i, j: (i, 0)),
                  pl.BlockSpec((k, bn), lambda i, j: (0, j))] + list(extra_specs),
        out_specs=pl.BlockSpec((bm, bn), lambda i, j: (i, j)),
        compiler_params=_params(("parallel", "parallel"), blocks, _nbytes((bm, bn), F32)),
        name=name,
    )(a, b, *extra)


def _bias_kernel(tab_ref, o_ref, *, blk, window, rebase):
    h = pl.program_id(0)
    r = lax.broadcasted_iota(jnp.int32, (blk, blk), 0)
    c = lax.broadcasted_iota(jnp.int32, (blk, blk), 1)
    base = tab_ref[N_BUCKETS - 1, h] if rebase else 0.0
    for delta in (0, 1):
        rel = r - c + delta * blk
        n = jnp.maximum(rel, 0)
        nf = jnp.maximum(n, 1).astype(F32)
        large = MAX_EXACT + (jnp.log(nf / MAX_EXACT) / math.log(MAX_DIST / MAX_EXACT)
                             * (N_BUCKETS - MAX_EXACT)).astype(jnp.int32)
        large = jnp.minimum(large, N_BUCKETS - 1)
        bucket = jnp.where(n < MAX_EXACT, n, large)
        bias = jnp.zeros((blk, blk), F32)
        for b in range(N_BUCKETS):
            bias = jnp.where(bucket == b, tab_ref[b, h] - base, bias)
        mask = rel >= 0
        if window is not None:
            mask = mask & (rel < window)
        o_ref[0, delta] = jnp.where(mask, bias, NEG)


def _bias_tiles(tab, blk, window, rebase):
    heads = tab.shape[1]
    return pl.pallas_call(
        functools.partial(_bias_kernel, blk=blk, window=window, rebase=rebase),
        out_shape=jax.ShapeDtypeStruct((heads, 2, blk, blk), F32),
        grid=(heads,),
        in_specs=[pl.BlockSpec(memory_space=pltpu.SMEM)],
        out_specs=pl.BlockSpec((1, 2, blk, blk), lambda h: (h, 0, 0, 0)),
        compiler_params=_params(("parallel",), _nbytes((2, blk, blk), F32), 4 * _nbytes((blk, blk), F32)),
        name="bias_tiles",
    )(tab)


def _da_kernel(q_ref, k_ref, v_ref, bias_ref, lam_ref, g_ref, o_ref, *, lam_init):
    blk = q_ref.shape[0]
    qi = pl.program_id(2)
    lp = lam_ref[...]
    lam = (jnp.exp(jnp.sum(lp[0:1] * lp[1:2], axis=-1, keepdims=True))
           - jnp.exp(jnp.sum(lp[2:3] * lp[3:4], axis=-1, keepdims=True)) + lam_init)

    def softmax_pv(head, j, case):
        dims = slice((2 * head + j) * HEAD_DIM, (2 * head + j + 1) * HEAD_DIM)
        vdims = slice(head * DA_V_DIM, (head + 1) * DA_V_DIM)
        q = q_ref[:, dims]
        spans = [(slice(case * blk, (case + 1) * blk), bias_ref[head, 0])]
        if case >= 1:
            spans.append((slice((case - 1) * blk, case * blk), bias_ref[head, 1]))
        if case >= 2:
            spans.append((slice(0, (case - 1) * blk), None))
        scores = []
        for rows, bias in spans:
            s = _dot_nt(q, k_ref[rows, dims])
            scores.append(s if bias is None else s + bias)
        m = functools.reduce(jnp.maximum, [jnp.max(s, axis=-1, keepdims=True) for s in scores])
        probs = [jnp.exp(s - m) for s in scores]
        norm = sum(jnp.sum(p, axis=-1, keepdims=True) for p in probs)
        out = sum(jnp.dot(p.astype(BF16), v_ref[rows, vdims], preferred_element_type=F32)
                  for p, (rows, _) in zip(probs, spans))
        return out / norm

    for case in range(k_ref.shape[0] // blk):
        @pl.when(qi == case)
        def _(case=case):
            for head in range(q_ref.shape[1] // DA_V_DIM):
                o = softmax_pv(head, 0, case) - lam * softmax_pv(head, 1, case)
                y = o * lax.rsqrt(jnp.mean(o * o, axis=-1, keepdims=True) + EPS)
                o_ref[:, head * DA_V_DIM:(head + 1) * DA_V_DIM] = (
                    (y * g_ref[...]) * (1.0 - lam_init)).astype(o_ref.dtype)


def _diff_attention(qkv, bias, lam_p, subln_g, lam_init, batch, seq, heads=DA_HEADS_PER_STEP):
    blk = DA_BLK
    nq = seq // blk
    width = heads * DA_V_DIM
    blocks = (2 * _nbytes((blk, width), BF16) + 2 * _nbytes((seq, width), BF16)
              + _nbytes((heads, 2, blk, blk), F32))
    scratch = 6 * heads * _nbytes((blk, seq), F32)
    k_col0 = DA_QK // width
    v_col0 = 2 * DA_QK // width
    return pl.pallas_call(
        functools.partial(_da_kernel, lam_init=lam_init),
        out_shape=jax.ShapeDtypeStruct((batch * seq, DA_V), BF16),
        grid=(batch, DA_HEADS // heads, nq),
        in_specs=[pl.BlockSpec((blk, width), lambda b, h, i: (b * nq + i, h)),
                  pl.BlockSpec((seq, width), lambda b, h, i: (b, k_col0 + h)),
                  pl.BlockSpec((seq, width), lambda b, h, i: (b, v_col0 + h)),
                  pl.BlockSpec((heads, 2, blk, blk), lambda b, h, i: (h, 0, 0, 0)),
                  pl.BlockSpec((4, HEAD_DIM), lambda b, h, i: (0, 0)),
                  pl.BlockSpec((1, DA_V_DIM), lambda b, h, i: (0, 0))],
        out_specs=pl.BlockSpec((blk, width), lambda b, h, i: (b * nq + i, h)),
        compiler_params=_params(("parallel", "parallel", "parallel"), blocks, scratch),
        name="diff_attention",
    )(qkv, qkv, qkv, bias, lam_p, subln_g.reshape(1, DA_V_DIM))


def _swa_kernel(q_ref, kc_ref, kp_ref, vc_ref, vp_ref, bias_ref, sink_ref, o_ref):
    n = pl.program_id(1)
    is_prev = lax.broadcasted_iota(jnp.int32, (1, 2 * WINDOW), 1) < WINDOW
    no_prev = jnp.where(is_prev & (n == 0), NEG, 0.0).astype(F32)
    for hk in range(SW_KV_HEADS):
        cols = slice(hk * HEAD_DIM, (hk + 1) * HEAD_DIM)
        heads = [slice((hk * SW_GROUP + g) * HEAD_DIM, (hk * SW_GROUP + g + 1) * HEAD_DIM) for g in range(SW_GROUP)]
        keys = jnp.concatenate([kp_ref[:, cols], kc_ref[:, cols]], axis=0)
        values = jnp.concatenate([vp_ref[:, cols], vc_ref[:, cols]], axis=0)
        for i in range(q_ref.shape[0] // WINDOW):
            rows = slice(i * WINDOW, (i + 1) * WINDOW)
            band = slice(i * WINDOW, (i + 2) * WINDOW)
            q = jnp.concatenate([q_ref[rows, hd] for hd in heads], axis=0)
            s = _dot_nt(q, keys[band]) + bias_ref[hk]
            if i == 0:
                s = s + no_prev
            sink = sink_ref[hk]
            m = jnp.maximum(jnp.max(s, axis=-1, keepdims=True), sink)
            e = jnp.exp(s - m)
            den = jnp.sum(e, axis=-1, keepdims=True) + jnp.exp(sink - m)
            o = jnp.dot(e.astype(BF16), values[band], preferred_element_type=F32) / den
            for g, hd in enumerate(heads):
                o_ref[rows, hd] = o[g * WINDOW:(g + 1) * WINDOW, :].astype(o_ref.dtype)


def _sliding_attention(qkv, bias_tiles, sinks, batch, seq, group=SW_BLOCKS_PER_STEP):
    nb = seq // WINDOW
    ng = nb // group
    span = group * WINDOW
    q_col = 3 * DA_QK // SW_Q
    k_col = (3 * DA_QK + SW_Q) // SW_KV
    v_col = k_col + 1
    cur = lambda b, n: b * ng + n
    prev = lambda b, n: b * nb + jnp.maximum(n * group - 1, 0)
    rows = SW_GROUP * WINDOW
    bias = bias_tiles.reshape(SW_KV_HEADS, SW_GROUP, 2, WINDOW, WINDOW)[:, :, ::-1]
    bias = bias.transpose(0, 1, 3, 2, 4).reshape(SW_KV_HEADS, rows, 2 * WINDOW)
    sink_cols = jnp.repeat(sinks.astype(F32).reshape(SW_KV_HEADS, SW_GROUP), WINDOW, axis=1).reshape(
        SW_KV_HEADS, rows, 1)
    blocks = (2 * _nbytes((span, SW_Q), BF16) + 2 * _nbytes((span + WINDOW, SW_KV), BF16)
              + _nbytes((SW_KV_HEADS, rows, 2 * WINDOW), F32) + _nbytes((SW_KV_HEADS, rows, 128), F32))
    return pl.pallas_call(
        _swa_kernel,
        out_shape=jax.ShapeDtypeStruct((batch * seq, SW_Q), BF16),
        grid=(batch, ng),
        in_specs=[pl.BlockSpec((span, SW_Q), lambda b, n: (cur(b, n), q_col)),
                  pl.BlockSpec((span, SW_KV), lambda b, n: (cur(b, n), k_col)),
                  pl.BlockSpec((WINDOW, SW_KV), lambda b, n: (prev(b, n), k_col)),
                  pl.BlockSpec((span, SW_KV), lambda b, n: (cur(b, n), v_col)),
                  pl.BlockSpec((WINDOW, SW_KV), lambda b, n: (prev(b, n), v_col)),
                  pl.BlockSpec((SW_KV_HEADS, rows, 2 * WINDOW), lambda b, n: (0, 0, 0)),
                  pl.BlockSpec((SW_KV_HEADS, rows, 1), lambda b, n: (0, 0, 0))],
        out_specs=pl.BlockSpec((span, SW_Q), lambda b, n: (cur(b, n), 0)),
        compiler_params=_params(("parallel", "parallel"), blocks, 16 * group * _nbytes((rows, 2 * WINDOW), F32)),
        name="sliding_attention",
    )(qkv, qkv, qkv, qkv, qkv, bias, sink_cols)


def _merge_kernel(oa_ref, ob_ref, wa_ref, wb_ref, ga_ref, gb_ref, o_ref):
    a = jnp.dot(oa_ref[...], wa_ref[...], preferred_element_type=F32)
    b = jnp.dot(ob_ref[...], wb_ref[...], preferred_element_type=F32)
    o_ref[...] = (ga_ref[...].astype(F32) * a + gb_ref[...].astype(F32) * b).astype(o_ref.dtype)


def _merge(o_a, o_b, w_a, w_b, gates, bm=1024, bn=1024):
    t, ka = o_a.shape
    kb = o_b.shape[1]
    d = w_a.shape[1]
    nj = d // bn
    blocks = (_nbytes((bm, ka), BF16) + _nbytes((bm, kb), BF16) + _nbytes((ka, bn), BF16)
              + _nbytes((kb, bn), BF16) + 2 * _nbytes((bm, bn), gates.dtype) + _nbytes((bm, bn), BF16))
    return pl.pallas_call(
        _merge_kernel,
        out_shape=jax.ShapeDtypeStruct((t, d), BF16),
        grid=(t // bm, nj),
        in_specs=[pl.BlockSpec((bm, ka), lambda i, j: (i, 0)),
                  pl.BlockSpec((bm, kb), lambda i, j: (i, 0)),
                  pl.BlockSpec((ka, bn), lambda i, j: (0, j)),
                  pl.BlockSpec((kb, bn), lambda i, j: (0, j)),
                  pl.BlockSpec((bm, bn), lambda i, j: (i, j)),
                  pl.BlockSpec((bm, bn), lambda i, j: (i, nj + j))],
        out_specs=pl.BlockSpec((bm, bn), lambda i, j: (i, j)),
        compiler_params=_params(("parallel", "parallel"), blocks, 2 * _nbytes((bm, bn), F32)),
        name="merge",
    )(o_a, o_b, w_a, w_b, gates, gates)


def _sort_pairs(n):
    pairs = []

    def merge(lo, hi, r):
        step = r * 2
        if step < hi - lo:
            merge(lo, hi, step)
            merge(lo + r, hi, step)
            pairs.extend((i, i + r) for i in range(lo + r, hi - r, step))
        else:
            pairs.append((lo, lo + r))

    def sort(lo, hi):
        if hi - lo >= 1:
            mid = lo + (hi - lo) // 2
            sort(lo, mid)
            sort(mid + 1, hi)
            merge(lo, hi, 1)

    sort(0, n - 1)
    return pairs


_SORT16 = _sort_pairs(PEER_TOPK)


def _sort_desc(xs):
    xs = list(xs)
    for i, j in _SORT16:
        xs[i], xs[j] = jnp.maximum(xs[i], xs[j]), jnp.minimum(xs[i], xs[j])
    return xs


def _merge_top(a, b):
    k = PEER_TOPK
    xs = [jnp.maximum(a[i], b[k - 1 - i]) for i in range(k)]
    d = k // 2
    while d >= 1:
        for i in range(k):
            if not i & d:
                xs[i], xs[i + d] = jnp.maximum(xs[i], xs[i + d]), jnp.minimum(xs[i], xs[i + d])
        d //= 2
    return xs


def _top16_over_rows(s):
    groups = [s[a * 8:(a + 1) * 8, :] for a in range(s.shape[0] // 8)]
    xs = _sort_desc(groups)
    for shift in (4, 2, 1):
        xs = _merge_top(xs, [pltpu.roll(x, shift, 0) for x in xs])
    return xs


def _count_leading(pred, values):
    n = len(values)

    def pick(lo, hi, taken):
        if not taken:
            return values[(lo + hi) // 2 - 1]
        mid = (lo + hi) // 2
        return jnp.where(taken[0], pick(mid, hi, taken[1:]), pick(lo, mid, taken[1:]))

    taken = []
    step = n // 2
    while step >= 1:
        taken.append(pred(pick(0, n, taken)))
        step //= 2
    total = None
    for i, t in enumerate(taken):
        part = jnp.where(t, float(n >> (i + 1)), 0.0)
        total = part if total is None else total + part
    return total + jnp.where(pred(values[n - 1]), 1.0, 0.0)


def _route_kernel(q_ref, k1_ref, k2_ref, cnt_ref, e1_ref, rank_ref, e2_ref, top_ref):
    q = q_ref[...]
    s1 = _dot_nt(k1_ref[...], q[:, :KEY_DIM])
    s2 = _dot_nt(k2_ref[...], q[:, KEY_DIM:])
    v1 = _top16_over_rows(s1)
    v2 = _top16_over_rows(s2)
    k = PEER_TOPK
    top = [v1[0] + v2[b] for b in range(k)]
    rest = [v1[a] + v2[b] for a in range(1, k) for b in range(k) if (a + 1) * (b + 1) <= k]
    pad = jnp.full(top[0].shape, -jnp.inf, F32)
    rest = rest + [pad] * (-len(rest) % k)
    for g in range(len(rest) // k):
        top = _merge_top(top, _sort_desc(rest[g * k:(g + 1) * k]))
    z = jnp.ones_like(top[0])
    for c in top[1:]:
        z = z + jnp.exp(c - top[0])
    tau = top[k - 1][0:1]
    best = [v[0:1] for v in v2]
    cnt = _count_leading(lambda b: s1 + b >= tau, best)
    rank = _count_leading(lambda b: b > s2, best)
    cnt_ref[0] = cnt
    rank_ref[0] = _pack_rows(rank.astype(BF16))
    e1_ref[0] = jnp.exp(s1 - v1[0][0:1]) / z[0:1]
    e2_ref[0] = _pack_rows(jnp.exp(s2 - v2[0][0:1]).astype(BF16))
    top_ref[0] = 1.0 / z[0:1]


def _route(q, k1, k2, tm=1024):
    t = q.shape[0]
    wide = jax.ShapeDtypeStruct((PEER_HEADS, N_KEYS, t), F32)
    narrow = jax.ShapeDtypeStruct((PEER_HEADS, N_KEYS // 2, t), jnp.uint32)
    spec = pl.BlockSpec((1, N_KEYS, tm), lambda i, h: (h, 0, i))
    narrow_spec = pl.BlockSpec((1, N_KEYS // 2, tm), lambda i, h: (h, 0, i))
    blocks = _nbytes((tm, 2 * KEY_DIM), BF16) + 3 * _nbytes((N_KEYS, tm), F32)
    return pl.pallas_call(
        _route_kernel,
        out_shape=(wide, wide, narrow, narrow, jax.ShapeDtypeStruct((PEER_HEADS, 1, t), F32)),
        grid=(t // tm, PEER_HEADS),
        in_specs=[pl.BlockSpec((tm, 2 * KEY_DIM), lambda i, h: (i, h)),
                  pl.BlockSpec((N_KEYS, KEY_DIM), lambda i, h: (0, 0)),
                  pl.BlockSpec((N_KEYS, KEY_DIM), lambda i, h: (0, 0))],
        out_specs=(spec, spec, narrow_spec, narrow_spec, pl.BlockSpec((1, 1, tm), lambda i, h: (h, 0, i))),
        compiler_params=_params(("parallel", "parallel"), blocks, 24 * _nbytes((N_KEYS, tm), F32)),
        name="peer_route",
    )(q, k1, k2)


def _peer_up_kernel(inv_u_ref, inv_v_ref, u_ref, h_ref, inv_ref, wscale_ref, cnt_ref, e1_ref, rank_ref, e2_ref,
                    o_ref, act_a, act_b, *, expert_tiles):
    s = pl.program_id(0)
    first_scale_block = (jnp.maximum(s - 1, 0) % expert_tiles) * (act_a.shape[0] // FP8_ROW_BLOCK)

    @pl.when(s == 0)
    def _():
        act_a[...] = jnp.zeros(act_a.shape, F32)
        act_b[...] = jnp.zeros(act_b.shape, F32)

    sub = PEER_UP_SUB_ROWS
    reps = sub // BF16_ROWS
    chunks = u_ref.shape[0]
    blocks_per_chunk = act_a.shape[0] // N_KEYS // chunks

    def epilogue_block(act_old, r):
        scale_block = first_scale_block + r // (FP8_ROW_BLOCK // N_KEYS)
        inv_u, inv_v = inv_u_ref[scale_block], inv_v_ref[scale_block]
        cnt_rows = [cnt_ref[h, pl.ds(r, 1), :] for h in range(PEER_HEADS)]
        e1_rows = [e1_ref[h, pl.ds(r, 1), :] * (wscale_ref[...] * inv_v) for h in range(PEER_HEADS)]
        for c in range(act_old.shape[1] // 128):
            cols = slice(c * 128, (c + 1) * 128)
            for part in range(N_KEYS // sub):
                act_rows = pl.ds(pl.multiple_of(r * N_KEYS + part * sub, sub), sub)
                in_rows = slice(part * sub // 2, (part + 1) * sub // 2)
                out_rows = pl.ds(pl.multiple_of((r * N_KEYS + part * sub) // 4, sub // 4), sub // 4)
                act = act_old[act_rows, cols]
                act_old[act_rows, cols] = jnp.zeros_like(act)
                act = act * (inv_ref[:, cols] * inv_u)
                gate = None
                for h in range(PEER_HEADS):
                    cnt = jnp.tile(jnp.broadcast_to(cnt_rows[h][:, cols], (BF16_ROWS, 128)).astype(BF16), (reps, 1))
                    e1 = jnp.tile(jnp.broadcast_to(e1_rows[h][:, cols], (BF16_ROWS, 128)).astype(BF16), (reps, 1))
                    routed = _unpack_rows(rank_ref[h, in_rows, cols]) < cnt
                    term = jnp.where(routed, _unpack_rows(e2_ref[h, in_rows, cols]) * e1, jnp.zeros((), BF16))
                    gate = term if gate is None else gate + term
                w = jax.nn.gelu(act).astype(BF16) * gate
                o_ref[out_rows, cols] = _pack_rows(w.astype(FP8))

    def step(act_new, act_old):
        def chunk(k, carry):
            act_new[...] += _dot_nt(u_ref[k], h_ref[k])
            for b in range(blocks_per_chunk):
                epilogue_block(act_old, k * blocks_per_chunk + b)
            return carry

        lax.fori_loop(0, chunks, chunk, 0, unroll=PEER_UP_UNROLL)

    @pl.when(s % 2 == 0)
    def _():
        step(act_a, act_b)

    @pl.when(s % 2 == 1)
    def _():
        step(act_b, act_a)


def _peer_up(u, inv_u, inv_v, hn, inv_h, w_scale, cnt, e1, rank, e2, te=PEER_UP_TE, tm=512):
    chunks, n_exp, dc = u.shape
    t = hn.shape[1]
    rows = te // N_KEYS
    assert rows % chunks == 0
    nj = n_exp // te
    steps = (t // tm) * nj

    def tile_of(step):
        return step // nj, step % nj

    def now(s):
        return tile_of(jnp.minimum(s, steps - 1))

    def lag(s):
        return tile_of(jnp.maximum(s - 1, 0))

    row_spec = pl.BlockSpec((PEER_HEADS, rows, tm), lambda s: (0, lag(s)[1], lag(s)[0]))
    full_spec = pl.BlockSpec((PEER_HEADS, N_KEYS // 2, tm), lambda s: (0, 0, lag(s)[0]))
    blocks = (_nbytes((chunks, te, dc), FP8) + _nbytes((chunks, tm, dc), FP8)
              + 2 * _nbytes((PEER_HEADS, N_KEYS, tm), BF16)
              + 2 * _nbytes((PEER_HEADS, rows, tm), F32) + _nbytes((te, tm), BF16))
    smem = pl.BlockSpec(memory_space=pltpu.SMEM)
    return pl.pallas_call(
        functools.partial(_peer_up_kernel, expert_tiles=nj),
        out_shape=jax.ShapeDtypeStruct((n_exp // 4, t), jnp.uint32),
        grid=(steps + 1,),
        in_specs=[smem, smem,
                  pl.BlockSpec((chunks, te, dc), lambda s: (0, now(s)[1], 0)),
                  pl.BlockSpec((chunks, tm, dc), lambda s: (0, now(s)[0], 0)),
                  pl.BlockSpec((1, tm), lambda s: (0, lag(s)[0])),
                  pl.BlockSpec((1, tm), lambda s: (0, lag(s)[0])),
                  row_spec, row_spec, full_spec, full_spec],
        out_specs=pl.BlockSpec((te // 4, tm), lambda s: (lag(s)[1], lag(s)[0])),
        scratch_shapes=[pltpu.VMEM((te, tm), F32), pltpu.VMEM((te, tm), F32)],
        compiler_params=_params(("arbitrary",), blocks, 3 * _nbytes((te, tm), F32)),
        name="peer_up",
    )(inv_u, inv_v, u, hn, inv_h, w_scale, cnt, e1, rank, e2)


def _peer_down_kernel(vt_ref, w_ref, inv_ref, x_ref, o_ref, acc_ref):
    kk = pl.program_id(2)

    @pl.when(kk == 0)
    def _():
        acc_ref[...] = jnp.zeros(acc_ref.shape, F32)

    acc_ref[...] += jnp.dot(vt_ref[...], _unpack_rows(w_ref[...], FP8), preferred_element_type=F32)

    @pl.when(kk == pl.num_programs(2) - 1)
    def _():
        o_ref[...] = x_ref[...] + (acc_ref[...] * inv_ref[...]).T


def _peer_down(vt, wt, inv_w_scale, x, bd=1024, bt=1024, tk=4096):
    d, n_exp = vt.shape
    t = wt.shape[1]
    blocks = (_nbytes((bd, tk), FP8) + _nbytes((tk, bt), FP8) + 2 * _nbytes((bt, bd), F32))
    return pl.pallas_call(
        _peer_down_kernel,
        out_shape=jax.ShapeDtypeStruct((t, d), F32),
        grid=(d // bd, t // bt, n_exp // tk),
        in_specs=[pl.BlockSpec((bd, tk), lambda i, j, k: (i, k)),
                  pl.BlockSpec((tk // 4, bt), lambda i, j, k: (k, j)),
                  pl.BlockSpec((1, bt), lambda i, j, k: (0, j)),
                  pl.BlockSpec((bt, bd), lambda i, j, k: (j, i))],
        out_specs=pl.BlockSpec((bt, bd), lambda i, j, k: (j, i)),
        scratch_shapes=[pltpu.VMEM((bd, bt), F32)],
        compiler_params=_params(("parallel", "parallel", "arbitrary"), blocks, 3 * _nbytes((bd, bt), F32)),
        name="peer_down",
    )(vt, wt, inv_w_scale, x)


def _w_scale(h_norm, u_norm, top_weight, inv_v):
    bound = h_norm.reshape(1, -1) * u_norm * jnp.sum(top_weight, axis=0) * jnp.max(inv_v)
    return _pow2_scale(bound)


def _ple_kernel(h_ref, wg_ref, p_ref, wp_ref, x_ref, o_ref):
    gate = jax.nn.sigmoid(jnp.dot(h_ref[...], wg_ref[...], preferred_element_type=F32))
    emb = jnp.dot(p_ref[...], wp_ref[...], preferred_element_type=F32)
    o_ref[...] = x_ref[...] + gate * emb


def _ple(hp, w_gate, p, w_proj, x, bm=1024, bn=1024):
    t, d = hp.shape
    pd = p.shape[1]
    n = w_gate.shape[1]
    blocks = (_nbytes((bm, d), BF16) + _nbytes((d, bn), BF16) + _nbytes((bm, pd), BF16)
              + _nbytes((pd, bn), BF16) + 2 * _nbytes((bm, bn), F32))
    return pl.pallas_call(
        _ple_kernel,
        out_shape=jax.ShapeDtypeStruct((t, n), F32),
        grid=(t // bm, n // bn),
        in_specs=[pl.BlockSpec((bm, d), lambda i, j: (i, 0)),
                  pl.BlockSpec((d, bn), lambda i, j: (0, j)),
                  pl.BlockSpec((bm, pd), lambda i, j: (i, 0)),
                  pl.BlockSpec((pd, bn), lambda i, j: (0, j)),
                  pl.BlockSpec((bm, bn), lambda i, j: (i, j))],
        out_specs=pl.BlockSpec((bm, bn), lambda i, j: (i, j)),
        compiler_params=_params(("parallel", "parallel"), blocks, 2 * _nbytes((bm, bn), F32)),
        name="ple",
    )(hp, w_gate, p, w_proj, x)


def _qkv_column_scale():
    s = HEAD_DIM ** -0.5
    parts = [(DA_QK, s), (DA_QK, 1.0), (DA_V, 1.0), (SW_Q, s), (SW_KV, 1.0), (SW_KV, 1.0)]
    return jnp.concatenate([jnp.full((1, w), v, F32) for w, v in parts], axis=1)


@jax.jit
def kernel(x, p, positions, rel_bias, norm_mix, w_in, da_lambda, da_subln, sw_sinks, w_br_a, w_br_b, w_out,
           norm_ffn, peer_wq, peer_k1, peer_k2, peer_u, peer_v, norm_ple, ple_gate, ple_proj, norm_final):
    del positions
    batch, seq, d = x.shape
    t = batch * seq
    depth = w_in.shape[0]
    xf = x.reshape(t, d)
    da_bias = _bias_tiles(rel_bias[:, :DA_HEADS], DA_BLK, None, True)
    sw_bias = _bias_tiles(rel_bias[:, DA_HEADS:], WINDOW, WINDOW, False)
    col_scale = _qkv_column_scale()
    tile = pl.BlockSpec((1, 1024), lambda i, j: (0, j))
    for i in range(depth):
        lam_init = 0.8 - 0.6 * math.exp(-0.3 * i)
        h = _rmsnorm(xf, norm_mix[i], BF16)
        qkv = _matmul(_mm_scale_kernel, h, _narrow(w_in, i, 0, QKV_WIDTH, 1024), [col_scale], [tile],
                      BF16, 1024, 1024, "proj_qkv")
        gates = _matmul(_mm_sigmoid_kernel, h, _narrow(w_in, i, QKV_WIDTH, w_in.shape[2] - QKV_WIDTH, 1024),
                        [], [], BF16, 1024, 1024, "proj_gates")
        o_a = _diff_attention(qkv, da_bias, da_lambda[i], da_subln[i], lam_init, batch, seq)
        o_b = _sliding_attention(qkv, sw_bias, sw_sinks[i], batch, seq)
        merged = _merge(o_a, o_b, _narrow(w_br_a, i), _narrow(w_br_b, i), gates)
        xf = _matmul(_mm_residual_kernel, merged, _narrow(w_out, i), [xf],
                     [pl.BlockSpec((1024, 1024), lambda i, j: (i, j))], F32, 1024, 1024, "proj_out")
        hn, hn8, hn_inv, hn_norm = _rmsnorm_fp8(xf, norm_ffn[i], PEER_UP_CHUNKS)
        q = _matmul(_mm_plain_kernel, hn, _narrow(peer_wq, i), [], [], BF16, 1024, 1024, "peer_query")
        cnt, e1, rank, e2, top = _route(q, peer_k1[i].astype(BF16), peer_k2[i].astype(BF16))
        u8, inv_u, u_norm = _fp8_rows(peer_u, i, "chunked", PEER_UP_CHUNKS)
        vt8, inv_v, _ = _fp8_rows(peer_v, i, "transposed")
        w_scale, inv_w_scale = _w_scale(hn_norm, u_norm, top, inv_v)
        wt = _peer_up(u8, inv_u, inv_v, hn8, hn_inv.reshape(1, t), w_scale, cnt, e1, rank, e2)
        xf = _peer_down(vt8, wt, inv_w_scale, xf)
        hp = _rmsnorm(xf, norm_ple[i], BF16)
        xf = _ple(hp, _narrow(ple_gate, i), p[i].reshape(t, -1).astype(BF16), ple_proj[i].astype(BF16), xf)
    return _rmsnorm(xf, norm_final, F32).reshape(batch, seq, d)
```

```python
import functools
import math

import jax
import jax.numpy as jnp
from jax import lax
from jax.experimental import pallas as pl
from jax.experimental.pallas import tpu as pltpu

F32 = jnp.float32
BF16 = jnp.bfloat16
FP8 = jnp.float8_e4m3fn
FP8_TARGET = 240.0
FP8_TINY = 1e-30

HEAD_DIM = 128
DA_HEADS = 8
DA_V_DIM = 2 * HEAD_DIM
SW_Q_HEADS = 16
SW_KV_HEADS = 4
SW_GROUP = SW_Q_HEADS // SW_KV_HEADS
WINDOW = 128
N_BUCKETS = 32
MAX_EXACT = N_BUCKETS // 2
MAX_DIST = 128
NEG = -1e30
DA_QK = DA_HEADS * 2 * HEAD_DIM
DA_V = DA_HEADS * DA_V_DIM
SW_Q = SW_Q_HEADS * HEAD_DIM
SW_KV = SW_KV_HEADS * HEAD_DIM
QKV_WIDTH = 3 * DA_QK + SW_Q + 2 * SW_KV
PEER_HEADS = 8
N_KEYS = 128
PEER_TOPK = 16
KEY_DIM = 128
EPS = 1e-6

V7X_VMEM_REQUEST_CAP = 60 * 1024 * 1024
BF16_ROWS = 16
DA_BLK = 512
DA_HEADS_PER_STEP = 2
SW_BLOCKS_PER_STEP = 8
PEER_UP_TE = 1024
FP8_ROW_BLOCK = 512
PEER_UP_CHUNKS = 4
PEER_UP_UNROLL = 2
PEER_UP_SUB_ROWS = 64


def _nbytes(shape, dtype):
    return math.prod(shape) * jnp.dtype(dtype).itemsize


def _params(semantics, block_bytes, scratch_bytes=0, flags=None):
    need = int(1.25 * (2 * block_bytes + scratch_bytes)) + (4 << 20)
    return pltpu.CompilerParams(dimension_semantics=semantics,
                                vmem_limit_bytes=min(need, V7X_VMEM_REQUEST_CAP), flags=flags)


def _pack_rows(x):
    return pltpu.bitcast(x, jnp.uint32)


def _unpack_rows(x, dtype=BF16):
    return pltpu.bitcast(x, dtype)


def _dot_nt(a, b):
    return lax.dot_general(a, b, (((1,), (1,)), ((), ())), preferred_element_type=F32)


def _rmsnorm_kernel(x_ref, g_ref, o_ref):
    x = x_ref[...]
    y = x * lax.rsqrt(jnp.mean(x * x, axis=-1, keepdims=True) + EPS)
    o_ref[...] = (y * g_ref[...]).astype(o_ref.dtype)


def _rmsnorm(x, g, out_dtype, rows=512):
    t, d = x.shape
    blocks = _nbytes((rows, d), F32) + _nbytes((rows, d), out_dtype)
    return pl.pallas_call(
        _rmsnorm_kernel,
        out_shape=jax.ShapeDtypeStruct((t, d), out_dtype),
        grid=(t // rows,),
        in_specs=[pl.BlockSpec((rows, d), lambda i: (i, 0)),
                  pl.BlockSpec((1, d), lambda i: (0, 0))],
        out_specs=pl.BlockSpec((rows, d), lambda i: (i, 0)),
        compiler_params=_params(("parallel",), blocks, _nbytes((rows, d), F32)),
        name="rmsnorm",
    )(x, g.reshape(1, d))


def _rmsnorm_fp8_kernel(x_ref, g_ref, o_ref, oc_ref, inv_ref, norm_ref):
    x = x_ref[...]
    y = x * lax.rsqrt(jnp.mean(x * x, axis=-1, keepdims=True) + EPS) * g_ref[...]
    o_ref[...] = y.astype(o_ref.dtype)
    norm_ref[...] = jnp.sqrt(jnp.sum(y * y, axis=-1, keepdims=True))
    amax = jnp.maximum(jnp.max(jnp.abs(y), axis=-1, keepdims=True), FP8_TINY)
    inv_ref[...] = amax * (1.0 / FP8_TARGET)
    y8 = y * (FP8_TARGET / amax)
    dc = oc_ref.shape[2]
    for k in range(oc_ref.shape[0]):
        oc_ref[k] = y8[:, k * dc:(k + 1) * dc].astype(oc_ref.dtype)


def _rmsnorm_fp8(x, g, chunks, rows=256):
    t, d = x.shape
    dc = d // chunks
    blocks = _nbytes((rows, d), F32) + _nbytes((rows, d), BF16) + _nbytes((rows, d), FP8) + _nbytes((rows, 128), F32)
    return pl.pallas_call(
        _rmsnorm_fp8_kernel,
        out_shape=(jax.ShapeDtypeStruct((t, d), BF16), jax.ShapeDtypeStruct((chunks, t, dc), FP8),
                   jax.ShapeDtypeStruct((t, 1), F32), jax.ShapeDtypeStruct((t, 1), F32)),
        grid=(t // rows,),
        in_specs=[pl.BlockSpec((rows, d), lambda i: (i, 0)),
                  pl.BlockSpec((1, d), lambda i: (0, 0))],
        out_specs=(pl.BlockSpec((rows, d), lambda i: (i, 0)),
                   pl.BlockSpec((chunks, rows, dc), lambda i: (0, i, 0)),
                   pl.BlockSpec((rows, 1), lambda i: (i, 0)),
                   pl.BlockSpec((rows, 1), lambda i: (i, 0))),
        compiler_params=_params(("parallel",), blocks, 2 * _nbytes((rows, d), F32)),
        name="rmsnorm_fp8",
    )(x, g.reshape(1, d))


CAST_BLOCK_BYTES = 8 << 20


def _cast_kernel(x_ref, o_ref):
    o_ref[...] = x_ref[...].astype(o_ref.dtype)


def _narrow(w, layer, col0=0, ncols=None, bc=None):
    _, r, c = w.shape
    ncols = c if ncols is None else ncols
    bc = ncols if bc is None else bc
    br = min(r, CAST_BLOCK_BYTES // (bc * 4))
    return pl.pallas_call(
        _cast_kernel,
        out_shape=jax.ShapeDtypeStruct((r, ncols), BF16),
        grid=(r // br, ncols // bc),
        in_specs=[pl.BlockSpec((None, br, bc), lambda i, j: (layer, i, col0 // bc + j))],
        out_specs=pl.BlockSpec((br, bc), lambda i, j: (i, j)),
        compiler_params=_params(("parallel", "parallel"), _nbytes((br, bc), F32) + _nbytes((br, bc), BF16),
                                _nbytes((br, bc), F32)),
        name="narrow",
    )(w)


def _pow2_scale(amax):
    shift = jnp.floor(jnp.log2(FP8_TARGET / jnp.maximum(amax, FP8_TINY)))
    return jnp.exp2(shift), jnp.exp2(-shift)


def _fp8_rows_kernel(x_ref, o_ref, inv_ref, norm_ref, *, layout):
    x = x_ref[...]
    norm = jnp.sqrt(jnp.max(jnp.sum(x * x, axis=1, keepdims=True), axis=0, keepdims=True))
    norm_ref[...] = jnp.broadcast_to(norm, norm_ref.shape)
    amax = jnp.max(jnp.max(jnp.abs(x), axis=0, keepdims=True), axis=1, keepdims=True)
    scale, inv = _pow2_scale(amax)
    inv_ref[...] = jnp.broadcast_to(inv, inv_ref.shape)
    y = x * scale
    if layout == "chunked":
        dc = o_ref.shape[2]
        for k in range(o_ref.shape[0]):
            o_ref[k] = y[:, k * dc:(k + 1) * dc].astype(o_ref.dtype)
    else:
        o_ref[...] = y.T.astype(o_ref.dtype)


def _fp8_rows(w, layer, layout, chunks=None, br=FP8_ROW_BLOCK):
    _, r, c = w.shape
    if layout == "chunked":
        shape = (chunks, r, c // chunks)
        out_spec = pl.BlockSpec((chunks, br, c // chunks), lambda i: (0, i, 0))
    else:
        shape = (c, r)
        out_spec = pl.BlockSpec((c, br), lambda i: (0, i))
    small = jax.ShapeDtypeStruct((r // br, 1, 128), F32)
    small_spec = pl.BlockSpec((1, 1, 128), lambda i: (i, 0, 0))
    out, inv, norm = pl.pallas_call(
        functools.partial(_fp8_rows_kernel, layout=layout),
        out_shape=(jax.ShapeDtypeStruct(shape, FP8), small, small),
        grid=(r // br,),
        in_specs=[pl.BlockSpec((None, br, c), lambda i: (layer, i, 0))],
        out_specs=(out_spec, small_spec, small_spec),
        compiler_params=_params(("parallel",), _nbytes((br, c), F32) + _nbytes((br, c), FP8), 2 * _nbytes((br, c), F32)),
        name="fp8_rows_" + layout,
    )(w)
    return out, inv[:, 0, 0], jnp.max(norm)


def _mm_in_kernel(a_ref, b_ref, s_ref, o_ref, *, plain_blocks):
    acc = jnp.dot(a_ref[...], b_ref[...], preferred_element_type=F32)

    @pl.when(pl.program_id(1) < plain_blocks)
    def _():
        o_ref[...] = (acc * s_ref[...]).astype(o_ref.dtype)

    @pl.when(pl.program_id(1) >= plain_blocks)
    def _():
        o_ref[...] = jax.nn.sigmoid(acc).astype(o_ref.dtype)


def _mm_plain_kernel(a_ref, b_ref, o_ref):
    o_ref[...] = jnp.dot(a_ref[...], b_ref[...], preferred_element_type=F32).astype(o_ref.dtype)


def _mm_residual_kernel(a_ref, b_ref, x_ref, o_ref):
    o_ref[...] = x_ref[...] + jnp.dot(a_ref[...], b_ref[...], preferred_element_type=F32)


def _matmul(body, a, b, extra, extra_specs, out_dtype, bm, bn, name):
    m, k = a.shape
    n = b.shape[1]
    blocks = (_nbytes((bm, k), a.dtype) + _nbytes((k, bn), b.dtype) + _nbytes((bm, bn), out_dtype)
              + sum(_nbytes(s.block_shape, e.dtype) for s, e in zip(extra_specs, extra)))
    return pl.pallas_call(
        body,
        out_shape=jax.ShapeDtypeStruct((m, n), out_dtype),
        grid=(m // bm, n // bn),
        in_specs=[pl.BlockSpec((bm, k), lambda i, j: (i, 0)),
                  pl.BlockSpec((k, bn), lambda i, j: (0, j))] + list(extra_specs),
        out_specs=pl.BlockSpec((bm, bn), lambda i, j: (i, j)),
        compiler_params=_params(("parallel", "parallel"), blocks, _nbytes((bm, bn), F32)),
        name=name,
    )(a, b, *extra)


def _bias_kernel(tab_ref, o_ref, *, blk, window, rebase):
    h = pl.program_id(0)
    r = lax.broadcasted_iota(jnp.int32, (blk, blk), 0)
    c = lax.broadcasted_iota(jnp.int32, (blk, blk), 1)
    base = tab_ref[N_BUCKETS - 1, h] if rebase else 0.0
    for delta in (0, 1):
        rel = r - c + delta * blk
        n = jnp.maximum(rel, 0)
        nf = jnp.maximum(n, 1).astype(F32)
        large = MAX_EXACT + (jnp.log(nf / MAX_EXACT) / math.log(MAX_DIST / MAX_EXACT)
                             * (N_BUCKETS - MAX_EXACT)).astype(jnp.int32)
        large = jnp.minimum(large, N_BUCKETS - 1)
        bucket = jnp.where(n < MAX_EXACT, n, large)
        bias = jnp.zeros((blk, blk), F32)
        for b in range(N_BUCKETS):
            bias = jnp.where(bucket == b, tab_ref[b, h] - base, bias)
        mask = rel >= 0
        if window is not None:
            mask = mask & (rel < window)
        o_ref[0, delta] = jnp.where(mask, bias, NEG)


def _bias_tiles(tab, blk, window, rebase):
    heads = tab.shape[1]
    return pl.pallas_call(
        functools.partial(_bias_kernel, blk=blk, window=window, rebase=rebase),
        out_shape=jax.ShapeDtypeStruct((heads, 2, blk, blk), F32),
        grid=(heads,),
        in_specs=[pl.BlockSpec(memory_space=pltpu.SMEM)],
        out_specs=pl.BlockSpec((1, 2, blk, blk), lambda h: (h, 0, 0, 0)),
        compiler_params=_params(("parallel",), _nbytes((2, blk, blk), F32), 4 * _nbytes((blk, blk), F32)),
        name="bias_tiles",
    )(tab)


def _da_kernel(q_ref, k_ref, v_ref, bias_ref, lam_ref, g_ref, o_ref, *, lam_init):
    blk = q_ref.shape[0]
    qi = pl.program_id(2)
    lp = lam_ref[...]
    lam = (jnp.exp(jnp.sum(lp[0:1] * lp[1:2], axis=-1, keepdims=True))
           - jnp.exp(jnp.sum(lp[2:3] * lp[3:4], axis=-1, keepdims=True)) + lam_init)

    def softmax_pv(head, j, case):
        dims = slice((2 * head + j) * HEAD_DIM, (2 * head + j + 1) * HEAD_DIM)
        vdims = slice(head * DA_V_DIM, (head + 1) * DA_V_DIM)
        q = q_ref[:, dims]
        spans = [(slice(case * blk, (case + 1) * blk), bias_ref[head, 0])]
        if case >= 1:
            spans.append((slice((case - 1) * blk, case * blk), bias_ref[head, 1]))
        if case >= 2:
            spans.append((slice(0, (case - 1) * blk), None))
        scores = []
        for rows, bias in spans:
            s = _dot_nt(q, k_ref[rows, dims])
            scores.append(s if bias is None else s + bias)
        m = functools.reduce(jnp.maximum, [jnp.max(s, axis=-1, keepdims=True) for s in scores])
        probs = [jnp.exp(s - m) for s in scores]
        norm = sum(jnp.sum(p, axis=-1, keepdims=True) for p in probs)
        out = sum(jnp.dot(p.astype(BF16), v_ref[rows, vdims], preferred_element_type=F32)
                  for p, (rows, _) in zip(probs, spans))
        return out / norm

    for case in range(k_ref.shape[0] // blk):
        @pl.when(qi == case)
        def _(case=case):
            for head in range(q_ref.shape[1] // DA_V_DIM):
                o = softmax_pv(head, 0, case) - lam * softmax_pv(head, 1, case)
                y = o * lax.rsqrt(jnp.mean(o * o, axis=-1, keepdims=True) + EPS)
                o_ref[:, head * DA_V_DIM:(head + 1) * DA_V_DIM] = (
                    (y * g_ref[...]) * (1.0 - lam_init)).astype(o_ref.dtype)


def _diff_attention(qkv, bias, lam_p, subln_g, lam_init, batch, seq, heads=DA_HEADS_PER_STEP):
    blk = DA_BLK
    nq = seq // blk
    width = heads * DA_V_DIM
    blocks = (2 * _nbytes((blk, width), BF16) + 2 * _nbytes((seq, width), BF16)
              + _nbytes((heads, 2, blk, blk), F32))
    scratch = 6 * heads * _nbytes((blk, seq), F32)
    k_col0 = DA_QK // width
    v_col0 = 2 * DA_QK // width
    return pl.pallas_call(
        functools.partial(_da_kernel, lam_init=lam_init),
        out_shape=jax.ShapeDtypeStruct((batch * seq, DA_V), BF16),
        grid=(batch, DA_HEADS // heads, nq),
        in_specs=[pl.BlockSpec((blk, width), lambda b, h, i: (b * nq + i, h)),
                  pl.BlockSpec((seq, width), lambda b, h, i: (b, k_col0 + h)),
                  pl.BlockSpec((seq, width), lambda b, h, i: (b, v_col0 + h)),
                  pl.BlockSpec((heads, 2, blk, blk), lambda b, h, i: (h, 0, 0, 0)),
                  pl.BlockSpec((4, HEAD_DIM), lambda b, h, i: (0, 0)),
                  pl.BlockSpec((1, DA_V_DIM), lambda b, h, i: (0, 0))],
        out_specs=pl.BlockSpec((blk, width), lambda b, h, i: (b * nq + i, h)),
        compiler_params=_params(("parallel", "parallel", "parallel"), blocks, scratch),
        name="diff_attention",
    )(qkv, qkv, qkv, bias, lam_p, subln_g.reshape(1, DA_V_DIM))


def _swa_kernel(q_ref, kc_ref, kp_ref, vc_ref, vp_ref, bias_ref, sink_ref, o_ref):
    n = pl.program_id(1)
    is_prev = lax.broadcasted_iota(jnp.int32, (1, 2 * WINDOW), 1) < WINDOW
    no_prev = jnp.where(is_prev & (n == 0), NEG, 0.0).astype(F32)
    for hk in range(SW_KV_HEADS):
        cols = slice(hk * HEAD_DIM, (hk + 1) * HEAD_DIM)
        heads = [slice((hk * SW_GROUP + g) * HEAD_DIM, (hk * SW_GROUP + g + 1) * HEAD_DIM) for g in range(SW_GROUP)]
        keys = jnp.concatenate([kp_ref[:, cols], kc_ref[:, cols]], axis=0)
        values = jnp.concatenate([vp_ref[:, cols], vc_ref[:, cols]], axis=0)
        for i in range(q_ref.shape[0] // WINDOW):
            rows = slice(i * WINDOW, (i + 1) * WINDOW)
            band = slice(i * WINDOW, (i + 2) * WINDOW)
            q = jnp.concatenate([q_ref[rows, hd] for hd in heads], axis=0)
            s = _dot_nt(q, keys[band]) + bias_ref[hk]
            if i == 0:
                s = s + no_prev
            sink = sink_ref[hk]
            m = jnp.maximum(jnp.max(s, axis=-1, keepdims=True), sink)
            e = jnp.exp(s - m)
            den = jnp.sum(e, axis=-1, keepdims=True) + jnp.exp(sink - m)
            o = jnp.dot(e.astype(BF16), values[band], preferred_element_type=F32) / den
            for g, hd in enumerate(heads):
                o_ref[rows, hd] = o[g * WINDOW:(g + 1) * WINDOW, :].astype(o_ref.dtype)


def _sliding_attention(qkv, bias_tiles, sinks, batch, seq, group=SW_BLOCKS_PER_STEP):
    nb = seq // WINDOW
    ng = nb // group
    span = group * WINDOW
    q_col = 3 * DA_QK // SW_Q
    k_col = (3 * DA_QK + SW_Q) // SW_KV
    v_col = k_col + 1
    cur = lambda b, n: b * ng + n
    prev = lambda b, n: b * nb + jnp.maximum(n * group - 1, 0)
    rows = SW_GROUP * WINDOW
    bias = bias_tiles.reshape(SW_KV_HEADS, SW_GROUP, 2, WINDOW, WINDOW)[:, :, ::-1]
    bias = bias.transpose(0, 1, 3, 2, 4).reshape(SW_KV_HEADS, rows, 2 * WINDOW)
    sink_cols = jnp.repeat(sinks.astype(F32).reshape(SW_KV_HEADS, SW_GROUP), WINDOW, axis=1).reshape(
        SW_KV_HEADS, rows, 1)
    blocks = (2 * _nbytes((span, SW_Q), BF16) + 2 * _nbytes((span + WINDOW, SW_KV), BF16)
              + _nbytes((SW_KV_HEADS, rows, 2 * WINDOW), F32) + _nbytes((SW_KV_HEADS, rows, 128), F32))
    return pl.pallas_call(
        _swa_kernel,
        out_shape=jax.ShapeDtypeStruct((batch * seq, SW_Q), BF16),
        grid=(batch, ng),
        in_specs=[pl.BlockSpec((span, SW_Q), lambda b, n: (cur(b, n), q_col)),
                  pl.BlockSpec((span, SW_KV), lambda b, n: (cur(b, n), k_col)),
                  pl.BlockSpec((WINDOW, SW_KV), lambda b, n: (prev(b, n), k_col)),
                  pl.BlockSpec((span, SW_KV), lambda b, n: (cur(b, n), v_col)),
                  pl.BlockSpec((WINDOW, SW_KV), lambda b, n: (prev(b, n), v_col)),
                  pl.BlockSpec((SW_KV_HEADS, rows, 2 * WINDOW), lambda b, n: (0, 0, 0)),
                  pl.BlockSpec((SW_KV_HEADS, rows, 1), lambda b, n: (0, 0, 0))],
        out_specs=pl.BlockSpec((span, SW_Q), lambda b, n: (cur(b, n), 0)),
        compiler_params=_params(("parallel", "parallel"), blocks, 16 * group * _nbytes((rows, 2 * WINDOW), F32)),
        name="sliding_attention",
    )(qkv, qkv, qkv, qkv, qkv, bias, sink_cols)


def _merge_kernel(oa_ref, ob_ref, wa_ref, wb_ref, ga_ref, gb_ref, o_ref):
    a = jnp.dot(oa_ref[...], wa_ref[...], preferred_element_type=F32)
    b = jnp.dot(ob_ref[...], wb_ref[...], preferred_element_type=F32)
    o_ref[...] = (ga_ref[...].astype(F32) * a + gb_ref[...].astype(F32) * b).astype(o_ref.dtype)


def _merge(o_a, o_b, w_a, w_b, gates, gate_col0, bm=1024, bn=1024):
    t, ka = o_a.shape
    kb = o_b.shape[1]
    d = w_a.shape[1]
    nj = d // bn
    g0 = gate_col0 // bn
    blocks = (_nbytes((bm, ka), BF16) + _nbytes((bm, kb), BF16) + _nbytes((ka, bn), BF16)
              + _nbytes((kb, bn), BF16) + 2 * _nbytes((bm, bn), gates.dtype) + _nbytes((bm, bn), BF16))
    return pl.pallas_call(
        _merge_kernel,
        out_shape=jax.ShapeDtypeStruct((t, d), BF16),
        grid=(t // bm, nj),
        in_specs=[pl.BlockSpec((bm, ka), lambda i, j: (i, 0)),
                  pl.BlockSpec((bm, kb), lambda i, j: (i, 0)),
                  pl.BlockSpec((ka, bn), lambda i, j: (0, j)),
                  pl.BlockSpec((kb, bn), lambda i, j: (0, j)),
                  pl.BlockSpec((bm, bn), lambda i, j: (i, g0 + j)),
                  pl.BlockSpec((bm, bn), lambda i, j: (i, g0 + nj + j))],
        out_specs=pl.BlockSpec((bm, bn), lambda i, j: (i, j)),
        compiler_params=_params(("parallel", "parallel"), blocks, 2 * _nbytes((bm, bn), F32)),
        name="merge",
    )(o_a, o_b, w_a, w_b, gates, gates)


def _sort_pairs(n):
    pairs = []

    def merge(lo, hi, r):
        step = r * 2
        if step < hi - lo:
            merge(lo, hi, step)
            merge(lo + r, hi, step)
            pairs.extend((i, i + r) for i in range(lo + r, hi - r, step))
        else:
            pairs.append((lo, lo + r))

    def sort(lo, hi):
        if hi - lo >= 1:
            mid = lo + (hi - lo) // 2
            sort(lo, mid)
            sort(mid + 1, hi)
            merge(lo, hi, 1)

    sort(0, n - 1)
    return pairs


_SORT16 = _sort_pairs(PEER_TOPK)


def _sort_desc(xs):
    xs = list(xs)
    for i, j in _SORT16:
        xs[i], xs[j] = jnp.maximum(xs[i], xs[j]), jnp.minimum(xs[i], xs[j])
    return xs


def _merge_top(a, b):
    k = PEER_TOPK
    xs = [jnp.maximum(a[i], b[k - 1 - i]) for i in range(k)]
    d = k // 2
    while d >= 1:
        for i in range(k):
            if not i & d:
                xs[i], xs[i + d] = jnp.maximum(xs[i], xs[i + d]), jnp.minimum(xs[i], xs[i + d])
        d //= 2
    return xs


def _top16_over_rows(s):
    groups = [s[a * 8:(a + 1) * 8, :] for a in range(s.shape[0] // 8)]
    xs = _sort_desc(groups)
    for shift in (4, 2, 1):
        xs = _merge_top(xs, [pltpu.roll(x, shift, 0) for x in xs])
    return xs


def _count_leading(pred, values):
    n = len(values)

    def pick(lo, hi, taken):
        if not taken:
            return values[(lo + hi) // 2 - 1]
        mid = (lo + hi) // 2
        return jnp.where(taken[0], pick(mid, hi, taken[1:]), pick(lo, mid, taken[1:]))

    taken = []
    step = n // 2
    while step >= 1:
        taken.append(pred(pick(0, n, taken)))
        step //= 2
    total = None
    for i, t in enumerate(taken):
        part = jnp.where(t, float(n >> (i + 1)), 0.0)
        total = part if total is None else total + part
    return total + jnp.where(pred(values[n - 1]), 1.0, 0.0)


def _route_kernel(q_ref, k1_ref, k2_ref, cnt_ref, e1_ref, rank_ref, e2_ref, top_ref):
    q = q_ref[...]
    s1 = _dot_nt(k1_ref[...], q[:, :KEY_DIM])
    s2 = _dot_nt(k2_ref[...], q[:, KEY_DIM:])
    v1 = _top16_over_rows(s1)
    v2 = _top16_over_rows(s2)
    k = PEER_TOPK
    top = [v1[0] + v2[b] for b in range(k)]
    rest = [v1[a] + v2[b] for a in range(1, k) for b in range(k) if (a + 1) * (b + 1) <= k]
    pad = jnp.full(top[0].shape, -jnp.inf, F32)
    rest = rest + [pad] * (-len(rest) % k)
    for g in range(len(rest) // k):
        top = _merge_top(top, _sort_desc(rest[g * k:(g + 1) * k]))
    z = jnp.ones_like(top[0])
    for c in top[1:]:
        z = z + jnp.exp(c - top[0])
    tau = top[k - 1][0:1]
    best = [v[0:1] for v in v2]
    cnt = _count_leading(lambda b: s1 + b >= tau, best)
    rank = _count_leading(lambda b: b > s2, best)
    cnt_ref[0] = cnt
    rank_ref[0] = _pack_rows(rank.astype(BF16))
    e1_ref[0] = jnp.exp(s1 - v1[0][0:1]) / z[0:1]
    e2_ref[0] = _pack_rows(jnp.exp(s2 - v2[0][0:1]).astype(BF16))
    top_ref[0] = 1.0 / z[0:1]


def _route(q, k1, k2, tm=1024):
    t = q.shape[0]
    wide = jax.ShapeDtypeStruct((PEER_HEADS, N_KEYS, t), F32)
    narrow = jax.ShapeDtypeStruct((PEER_HEADS, N_KEYS // 2, t), jnp.uint32)
    spec = pl.BlockSpec((1, N_KEYS, tm), lambda i, h: (h, 0, i))
    narrow_spec = pl.BlockSpec((1, N_KEYS // 2, tm), lambda i, h: (h, 0, i))
    blocks = _nbytes((tm, 2 * KEY_DIM), BF16) + 3 * _nbytes((N_KEYS, tm), F32)
    return pl.pallas_call(
        _route_kernel,
        out_shape=(wide, wide, narrow, narrow, jax.ShapeDtypeStruct((PEER_HEADS, 1, t), F32)),
        grid=(t // tm, PEER_HEADS),
        in_specs=[pl.BlockSpec((tm, 2 * KEY_DIM), lambda i, h: (i, h)),
                  pl.BlockSpec((N_KEYS, KEY_DIM), lambda i, h: (0, 0)),
                  pl.BlockSpec((N_KEYS, KEY_DIM), lambda i, h: (0, 0))],
        out_specs=(spec, spec, narrow_spec, narrow_spec, pl.BlockSpec((1, 1, tm), lambda i, h: (h, 0, i))),
        compiler_params=_params(("parallel", "parallel"), blocks, 24 * _nbytes((N_KEYS, tm), F32)),
        name="peer_route",
    )(q, k1, k2)


def _peer_up_kernel(inv_u_ref, inv_v_ref, u_ref, h_ref, inv_ref, wscale_ref, cnt_ref, e1_ref, rank_ref, e2_ref,
                    o_ref, act_a, act_b, *, expert_tiles):
    s = pl.program_id(0)
    first_scale_block = (jnp.maximum(s - 1, 0) % expert_tiles) * (act_a.shape[0] // FP8_ROW_BLOCK)

    @pl.when(s == 0)
    def _():
        act_a[...] = jnp.zeros(act_a.shape, F32)
        act_b[...] = jnp.zeros(act_b.shape, F32)

    sub = PEER_UP_SUB_ROWS
    reps = sub // BF16_ROWS
    chunks = u_ref.shape[0]
    blocks_per_chunk = act_a.shape[0] // N_KEYS // chunks

    def epilogue_block(act_old, r):
        scale_block = first_scale_block + r // (FP8_ROW_BLOCK // N_KEYS)
        inv_u, inv_v = inv_u_ref[scale_block], inv_v_ref[scale_block]
        cnt_rows = [cnt_ref[h, pl.ds(r, 1), :] for h in range(PEER_HEADS)]
        e1_rows = [e1_ref[h, pl.ds(r, 1), :] * (wscale_ref[...] * inv_v) for h in range(PEER_HEADS)]
        for c in range(act_old.shape[1] // 128):
            cols = slice(c * 128, (c + 1) * 128)
            for part in range(N_KEYS // sub):
                act_rows = pl.ds(pl.multiple_of(r * N_KEYS + part * sub, sub), sub)
                in_rows = slice(part * sub // 2, (part + 1) * sub // 2)
                out_rows = pl.ds(pl.multiple_of((r * N_KEYS + part * sub) // 4, sub // 4), sub // 4)
                act = act_old[act_rows, cols]
                act_old[act_rows, cols] = jnp.zeros_like(act)
                act = act * (inv_ref[:, cols] * inv_u)
                gate = None
                for h in range(PEER_HEADS):
                    cnt = jnp.tile(jnp.broadcast_to(cnt_rows[h][:, cols], (BF16_ROWS, 128)).astype(BF16), (reps, 1))
                    e1 = jnp.tile(jnp.broadcast_to(e1_rows[h][:, cols], (BF16_ROWS, 128)).astype(BF16), (reps, 1))
                    routed = _unpack_rows(rank_ref[h, in_rows, cols]) < cnt
                    term = jnp.where(routed, _unpack_rows(e2_ref[h, in_rows, cols]) * e1, jnp.zeros((), BF16))
                    gate = term if gate is None else gate + term
                w = jax.nn.gelu(act).astype(BF16) * gate
                o_ref[out_rows, cols] = _pack_rows(w.astype(FP8))

    def step(act_new, act_old):
        def chunk(k, carry):
            act_new[...] += _dot_nt(u_ref[k], h_ref[k])
            for b in range(blocks_per_chunk):
                epilogue_block(act_old, k * blocks_per_chunk + b)
            return carry

        lax.fori_loop(0, chunks, chunk, 0, unroll=PEER_UP_UNROLL)

    @pl.when(s % 2 == 0)
    def _():
        step(act_a, act_b)

    @pl.when(s % 2 == 1)
    def _():
        step(act_b, act_a)


def _peer_up(u, inv_u, inv_v, hn, inv_h, w_scale, cnt, e1, rank, e2, te=PEER_UP_TE, tm=512):
    chunks, n_exp, dc = u.shape
    t = hn.shape[1]
    rows = te // N_KEYS
    assert rows % chunks == 0
    nj = n_exp // te
    steps = (t // tm) * nj

    def tile_of(step):
        return step // nj, step % nj

    def now(s):
        return tile_of(jnp.minimum(s, steps - 1))

    def lag(s):
        return tile_of(jnp.maximum(s - 1, 0))

    row_spec = pl.BlockSpec((PEER_HEADS, rows, tm), lambda s: (0, lag(s)[1], lag(s)[0]))
    full_spec = pl.BlockSpec((PEER_HEADS, N_KEYS // 2, tm), lambda s: (0, 0, lag(s)[0]))
    blocks = (_nbytes((chunks, te, dc), FP8) + _nbytes((chunks, tm, dc), FP8)
              + 2 * _nbytes((PEER_HEADS, N_KEYS, tm), BF16)
              + 2 * _nbytes((PEER_HEADS, rows, tm), F32) + _nbytes((te, tm), BF16))
    smem = pl.BlockSpec(memory_space=pltpu.SMEM)
    return pl.pallas_call(
        functools.partial(_peer_up_kernel, expert_tiles=nj),
        out_shape=jax.ShapeDtypeStruct((n_exp // 4, t), jnp.uint32),
        grid=(steps + 1,),
        in_specs=[smem, smem,
                  pl.BlockSpec((chunks, te, dc), lambda s: (0, now(s)[1], 0)),
                  pl.BlockSpec((chunks, tm, dc), lambda s: (0, now(s)[0], 0)),
                  pl.BlockSpec((1, tm), lambda s: (0, lag(s)[0])),
                  pl.BlockSpec((1, tm), lambda s: (0, lag(s)[0])),
                  row_spec, row_spec, full_spec, full_spec],
        out_specs=pl.BlockSpec((te // 4, tm), lambda s: (lag(s)[1], lag(s)[0])),
        scratch_shapes=[pltpu.VMEM((te, tm), F32), pltpu.VMEM((te, tm), F32)],
        compiler_params=_params(("arbitrary",), blocks, 3 * _nbytes((te, tm), F32)),
        name="peer_up",
    )(inv_u, inv_v, u, hn, inv_h, w_scale, cnt, e1, rank, e2)


def _peer_down_kernel(vt_ref, w_ref, inv_ref, x_ref, o_ref, acc_ref):
    kk = pl.program_id(2)

    @pl.when(kk == 0)
    def _():
        acc_ref[...] = jnp.zeros(acc_ref.shape, F32)

    acc_ref[...] += jnp.dot(vt_ref[...], _unpack_rows(w_ref[...], FP8), preferred_element_type=F32)

    @pl.when(kk == pl.num_programs(2) - 1)
    def _():
        o_ref[...] = x_ref[...] + (acc_ref[...] * inv_ref[...]).T


def _peer_down(vt, wt, inv_w_scale, x, bd=1024, bt=1024, tk=4096):
    d, n_exp = vt.shape
    t = wt.shape[1]
    blocks = (_nbytes((bd, tk), FP8) + _nbytes((tk, bt), FP8) + 2 * _nbytes((bt, bd), F32))
    return pl.pallas_call(
        _peer_down_kernel,
        out_shape=jax.ShapeDtypeStruct((t, d), F32),
        grid=(d // bd, t // bt, n_exp // tk),
        in_specs=[pl.BlockSpec((bd, tk), lambda i, j, k: (i, k)),
                  pl.BlockSpec((tk // 4, bt), lambda i, j, k: (k, j)),
                  pl.BlockSpec((1, bt), lambda i, j, k: (0, j)),
                  pl.BlockSpec((bt, bd), lambda i, j, k: (j, i))],
        out_specs=pl.BlockSpec((bt, bd), lambda i, j, k: (j, i)),
        scratch_shapes=[pltpu.VMEM((bd, bt), F32)],
        compiler_params=_params(("parallel", "parallel", "arbitrary"), blocks, 3 * _nbytes((bd, bt), F32)),
        name="peer_down",
    )(vt, wt, inv_w_scale, x)


def _w_scale(h_norm, u_norm, top_weight, inv_v):
    bound = h_norm.reshape(1, -1) * u_norm * jnp.sum(top_weight, axis=0) * jnp.max(inv_v)
    return _pow2_scale(bound)


def _ple_kernel(h_ref, wg_ref, p_ref, wp_ref, x_ref, o_ref):
    gate = jax.nn.sigmoid(jnp.dot(h_ref[...], wg_ref[...], preferred_element_type=F32))
    emb = jnp.dot(p_ref[...], wp_ref[...], preferred_element_type=F32)
    o_ref[...] = x_ref[...] + gate * emb


def _ple(hp, w_gate, p, w_proj, x, bm=1024, bn=1024):
    t, d = hp.shape
    pd = p.shape[1]
    n = w_gate.shape[1]
    blocks = (_nbytes((bm, d), BF16) + _nbytes((d, bn), BF16) + _nbytes((bm, pd), BF16)
              + _nbytes((pd, bn), BF16) + 2 * _nbytes((bm, bn), F32))
    return pl.pallas_call(
        _ple_kernel,
        out_shape=jax.ShapeDtypeStruct((t, n), F32),
        grid=(t // bm, n // bn),
        in_specs=[pl.BlockSpec((bm, d), lambda i, j: (i, 0)),
                  pl.BlockSpec((d, bn), lambda i, j: (0, j)),
                  pl.BlockSpec((bm, pd), lambda i, j: (i, 0)),
                  pl.BlockSpec((pd, bn), lambda i, j: (0, j)),
                  pl.BlockSpec((bm, bn), lambda i, j: (i, j))],
        out_specs=pl.BlockSpec((bm, bn), lambda i, j: (i, j)),
        compiler_params=_params(("parallel", "parallel"), blocks, 2 * _nbytes((bm, bn), F32)),
        name="ple",
    )(hp, w_gate, p, w_proj, x)


def _qkv_column_scale():
    s = HEAD_DIM ** -0.5
    parts = [(DA_QK, s), (DA_QK, 1.0), (DA_V, 1.0), (SW_Q, s), (SW_KV, 1.0), (SW_KV, 1.0)]
    return jnp.concatenate([jnp.full((1, w), v, F32) for w, v in parts], axis=1)


@jax.jit
def kernel(x, p, positions, rel_bias, norm_mix, w_in, da_lambda, da_subln, sw_sinks, w_br_a, w_br_b, w_out,
           norm_ffn, peer_wq, peer_k1, peer_k2, peer_u, peer_v, norm_ple, ple_gate, ple_proj, norm_final):
    del positions
    batch, seq, d = x.shape
    t = batch * seq
    depth = w_in.shape[0]
    xf = x.reshape(t, d)
    da_bias = _bias_tiles(rel_bias[:, :DA_HEADS], DA_BLK, None, True)
    sw_bias = _bias_tiles(rel_bias[:, DA_HEADS:], WINDOW, WINDOW, False)
    col_scale = jnp.concatenate([_qkv_column_scale(), jnp.ones((1, w_in.shape[2] - QKV_WIDTH), F32)], axis=1)
    tile = pl.BlockSpec((1, 1024), lambda i, j: (0, j))
    for i in range(depth):
        lam_init = 0.8 - 0.6 * math.exp(-0.3 * i)
        h = _rmsnorm(xf, norm_mix[i], BF16)
        qkv = _matmul(functools.partial(_mm_in_kernel, plain_blocks=QKV_WIDTH // 1024), h,
                      _narrow(w_in, i, 0, w_in.shape[2], 1024), [col_scale], [tile], BF16, 1024, 1024, "proj_in")
        o_a = _diff_attention(qkv, da_bias, da_lambda[i], da_subln[i], lam_init, batch, seq)
        o_b = _sliding_attention(qkv, sw_bias, sw_sinks[i], batch, seq)
        merged = _merge(o_a, o_b, _narrow(w_br_a, i), _narrow(w_br_b, i), qkv, QKV_WIDTH)
        xf = _matmul(_mm_residual_kernel, merged, _narrow(w_out, i), [xf],
                     [pl.BlockSpec((1024, 1024), lambda i, j: (i, j))], F32, 1024, 1024, "proj_out")
        hn, hn8, hn_inv, hn_norm = _rmsnorm_fp8(xf, norm_ffn[i], PEER_UP_CHUNKS)
        q = _matmul(_mm_plain_kernel, hn, _narrow(peer_wq, i), [], [], BF16, 1024, 1024, "peer_query")
        cnt, e1, rank, e2, top = _route(q, peer_k1[i].astype(BF16), peer_k2[i].astype(BF16))
        u8, inv_u, u_norm = _fp8_rows(peer_u, i, "chunked", PEER_UP_CHUNKS)
        vt8, inv_v, _ = _fp8_rows(peer_v, i, "transposed")
        w_scale, inv_w_scale = _w_scale(hn_norm, u_norm, top, inv_v)
        wt = _peer_up(u8, inv_u, inv_v, hn8, hn_inv.reshape(1, t), w_scale, cnt, e1, rank, e2)
        xf = _peer_down(vt8, wt, inv_w_scale, xf)
        hp = _rmsnorm(xf, norm_ple[i], BF16)
        xf = _ple(hp, _narrow(ple_gate, i), p[i].reshape(t, -1).astype(BF16), ple_proj[i].astype(BF16), xf)
    return _rmsnorm(xf, norm_final, F32).reshape(batch, seq, d)
```

```python
import functools
import math

import jax
import jax.numpy as jnp
from jax import lax
from jax.experimental import pallas as pl
from jax.experimental.pallas import tpu as pltpu

F32 = jnp.float32
BF16 = jnp.bfloat16
FP8 = jnp.float8_e4m3fn
FP8_TARGET = 240.0
FP8_TINY = 1e-30

HEAD_DIM = 128
DA_HEADS = 8
DA_V_DIM = 2 * HEAD_DIM
SW_Q_HEADS = 16
SW_KV_HEADS = 4
SW_GROUP = SW_Q_HEADS // SW_KV_HEADS
WINDOW = 128
N_BUCKETS = 32
MAX_EXACT = N_BUCKETS // 2
MAX_DIST = 128
NEG = -1e30
DA_QK = DA_HEADS * 2 * HEAD_DIM
DA_V = DA_HEADS * DA_V_DIM
SW_Q = SW_Q_HEADS * HEAD_DIM
SW_KV = SW_KV_HEADS * HEAD_DIM
QKV_WIDTH = 3 * DA_QK + SW_Q + 2 * SW_KV
PEER_HEADS = 8
N_KEYS = 128
PEER_TOPK = 16
KEY_DIM = 128
EPS = 1e-6

V7X_VMEM_REQUEST_CAP = 60 * 1024 * 1024
BF16_ROWS = 16
DA_BLK = 512
DA_HEADS_PER_STEP = 2
SW_BLOCKS_PER_STEP = 8
PEER_UP_TE = 1024
FP8_ROW_BLOCK = 512
PEER_UP_CHUNKS = 4
PEER_UP_UNROLL = 2
PEER_UP_SUB_ROWS = 64


def _nbytes(shape, dtype):
    return math.prod(shape) * jnp.dtype(dtype).itemsize


def _params(semantics, block_bytes, scratch_bytes=0, flags=None):
    need = int(1.25 * (2 * block_bytes + scratch_bytes)) + (4 << 20)
    return pltpu.CompilerParams(dimension_semantics=semantics,
                                vmem_limit_bytes=min(need, V7X_VMEM_REQUEST_CAP), flags=flags)


def _pack_rows(x):
    return pltpu.bitcast(x, jnp.uint32)


def _unpack_rows(x, dtype=BF16):
    return pltpu.bitcast(x, dtype)


def _dot_nt(a, b):
    return lax.dot_general(a, b, (((1,), (1,)), ((), ())), preferred_element_type=F32)


def _rmsnorm_kernel(x_ref, g_ref, o_ref):
    x = x_ref[...]
    y = x * lax.rsqrt(jnp.mean(x * x, axis=-1, keepdims=True) + EPS)
    o_ref[...] = (y * g_ref[...]).astype(o_ref.dtype)


def _rmsnorm(x, g, out_dtype, rows=512):
    t, d = x.shape
    blocks = _nbytes((rows, d), F32) + _nbytes((rows, d), out_dtype)
    return pl.pallas_call(
        _rmsnorm_kernel,
        out_shape=jax.ShapeDtypeStruct((t, d), out_dtype),
        grid=(t // rows,),
        in_specs=[pl.BlockSpec((rows, d), lambda i: (i, 0)),
                  pl.BlockSpec((1, d), lambda i: (0, 0))],
        out_specs=pl.BlockSpec((rows, d), lambda i: (i, 0)),
        compiler_params=_params(("parallel",), blocks, _nbytes((rows, d), F32)),
        name="rmsnorm",
    )(x, g.reshape(1, d))


def _rmsnorm_fp8_kernel(x_ref, g_ref, o_ref, oc_ref, inv_ref, norm_ref):
    x = x_ref[...]
    y = x * lax.rsqrt(jnp.mean(x * x, axis=-1, keepdims=True) + EPS) * g_ref[...]
    o_ref[...] = y.astype(o_ref.dtype)
    norm_ref[...] = jnp.sqrt(jnp.sum(y * y, axis=-1, keepdims=True))
    amax = jnp.maximum(jnp.max(jnp.abs(y), axis=-1, keepdims=True), FP8_TINY)
    inv_ref[...] = amax * (1.0 / FP8_TARGET)
    y8 = y * (FP8_TARGET / amax)
    dc = oc_ref.shape[2]
    for k in range(oc_ref.shape[0]):
        oc_ref[k] = y8[:, k * dc:(k + 1) * dc].astype(oc_ref.dtype)


def _rmsnorm_fp8(x, g, chunks, rows=256):
    t, d = x.shape
    dc = d // chunks
    blocks = _nbytes((rows, d), F32) + _nbytes((rows, d), BF16) + _nbytes((rows, d), FP8) + _nbytes((rows, 128), F32)
    return pl.pallas_call(
        _rmsnorm_fp8_kernel,
        out_shape=(jax.ShapeDtypeStruct((t, d), BF16), jax.ShapeDtypeStruct((chunks, t, dc), FP8),
                   jax.ShapeDtypeStruct((t, 1), F32), jax.ShapeDtypeStruct((t, 1), F32)),
        grid=(t // rows,),
        in_specs=[pl.BlockSpec((rows, d), lambda i: (i, 0)),
                  pl.BlockSpec((1, d), lambda i: (0, 0))],
        out_specs=(pl.BlockSpec((rows, d), lambda i: (i, 0)),
                   pl.BlockSpec((chunks, rows, dc), lambda i: (0, i, 0)),
                   pl.BlockSpec((rows, 1), lambda i: (i, 0)),
                   pl.BlockSpec((rows, 1), lambda i: (i, 0))),
        compiler_params=_params(("parallel",), blocks, 2 * _nbytes((rows, d), F32)),
        name="rmsnorm_fp8",
    )(x, g.reshape(1, d))


CAST_BLOCK_BYTES = 8 << 20


def _cast_kernel(x_ref, o_ref):
    o_ref[...] = x_ref[...].astype(o_ref.dtype)


def _narrow(w, layer, col0=0, ncols=None, bc=None):
    _, r, c = w.shape
    ncols = c if ncols is None else ncols
    bc = ncols if bc is None else bc
    br = min(r, CAST_BLOCK_BYTES // (bc * 4))
    return pl.pallas_call(
        _cast_kernel,
        out_shape=jax.ShapeDtypeStruct((r, ncols), BF16),
        grid=(r // br, ncols // bc),
        in_specs=[pl.BlockSpec((None, br, bc), lambda i, j: (layer, i, col0 // bc + j))],
        out_specs=pl.BlockSpec((br, bc), lambda i, j: (i, j)),
        compiler_params=_params(("parallel", "parallel"), _nbytes((br, bc), F32) + _nbytes((br, bc), BF16),
                                _nbytes((br, bc), F32)),
        name="narrow",
    )(w)


def _pow2_scale(amax):
    shift = jnp.floor(jnp.log2(FP8_TARGET / jnp.maximum(amax, FP8_TINY)))
    return jnp.exp2(shift), jnp.exp2(-shift)


def _fp8_rows_kernel(x_ref, o_ref, inv_ref, norm_ref, *, layout):
    x = x_ref[...]
    norm = jnp.sqrt(jnp.max(jnp.sum(x * x, axis=1, keepdims=True), axis=0, keepdims=True))
    norm_ref[...] = jnp.broadcast_to(norm, norm_ref.shape)
    amax = jnp.max(jnp.max(jnp.abs(x), axis=0, keepdims=True), axis=1, keepdims=True)
    scale, inv = _pow2_scale(amax)
    inv_ref[...] = jnp.broadcast_to(inv, inv_ref.shape)
    y = x * scale
    if layout == "chunked":
        dc = o_ref.shape[2]
        for k in range(o_ref.shape[0]):
            o_ref[k] = y[:, k * dc:(k + 1) * dc].astype(o_ref.dtype)
    else:
        o_ref[...] = y.T.astype(o_ref.dtype)


def _fp8_rows(w, layer, layout, chunks=None, br=FP8_ROW_BLOCK):
    _, r, c = w.shape
    if layout == "chunked":
        shape = (chunks, r, c // chunks)
        out_spec = pl.BlockSpec((chunks, br, c // chunks), lambda i: (0, i, 0))
    else:
        shape = (c, r)
        out_spec = pl.BlockSpec((c, br), lambda i: (0, i))
    small = jax.ShapeDtypeStruct((r // br, 1, 128), F32)
    small_spec = pl.BlockSpec((1, 1, 128), lambda i: (i, 0, 0))
    out, inv, norm = pl.pallas_call(
        functools.partial(_fp8_rows_kernel, layout=layout),
        out_shape=(jax.ShapeDtypeStruct(shape, FP8), small, small),
        grid=(r // br,),
        in_specs=[pl.BlockSpec((None, br, c), lambda i: (layer, i, 0))],
        out_specs=(out_spec, small_spec, small_spec),
        compiler_params=_params(("parallel",), _nbytes((br, c), F32) + _nbytes((br, c), FP8), 2 * _nbytes((br, c), F32)),
        name="fp8_rows_" + layout,
    )(w)
    return out, inv[:, 0, 0], jnp.max(norm)


def _mm_scale_kernel(a_ref, b_ref, s_ref, o_ref):
    acc = jnp.dot(a_ref[...], b_ref[...], preferred_element_type=F32)
    o_ref[...] = (acc * s_ref[...]).astype(o_ref.dtype)


def _mm_sigmoid_kernel(a_ref, b_ref, o_ref):
    acc = jnp.dot(a_ref[...], b_ref[...], preferred_element_type=F32)
    o_ref[...] = jax.nn.sigmoid(acc).astype(o_ref.dtype)


def _mm_plain_kernel(a_ref, b_ref, o_ref):
    o_ref[...] = jnp.dot(a_ref[...], b_ref[...], preferred_element_type=F32).astype(o_ref.dtype)


def _mm_residual_kernel(a_ref, b_ref, x_ref, o_ref):
    o_ref[...] = x_ref[...] + jnp.dot(a_ref[...], b_ref[...], preferred_element_type=F32)


def _matmul(body, a, b, extra, extra_specs, out_dtype, bm, bn, name):
    m, k = a.shape
    n = b.shape[1]
    blocks = (_nbytes((bm, k), a.dtype) + _nbytes((k, bn), b.dtype) + _nbytes((bm, bn), out_dtype)
              + sum(_nbytes(s.block_shape, e.dtype) for s, e in zip(extra_specs, extra)))
    return pl.pallas_call(
        body,
        out_shape=jax.ShapeDtypeStruct((m, n), out_dtype),
        grid=(m // bm, n // bn),
        in_specs=[pl.BlockSpec((bm, k), lambda i, j: (i, 0)),
                  pl.BlockSpec((k, bn), lambda i, j: (0, j))] + list(extra_specs),
        out_specs=pl.BlockSpec((bm, bn), lambda i, j: (i, j)),
        compiler_params=_params(("parallel", "parallel"), blocks, _nbytes((bm, bn), F32)),
        name=name,
    )(a, b, *extra)


def _bias_kernel(tab_ref, o_ref, *, blk, window, rebase):
    h = pl.program_id(0)
    r = lax.broadcasted_iota(jnp.int32, (blk, blk), 0)
    c = lax.broadcasted_iota(jnp.int32, (blk, blk), 1)
    base = tab_ref[N_BUCKETS - 1, h] if rebase else 0.0
    for delta in (0, 1):
        rel = r - c + delta * blk
        n = jnp.maximum(rel, 0)
        nf = jnp.maximum(n, 1).astype(F32)
        large = MAX_EXACT + (jnp.log(nf / MAX_EXACT) / math.log(MAX_DIST / MAX_EXACT)
                             * (N_BUCKETS - MAX_EXACT)).astype(jnp.int32)
        large = jnp.minimum(large, N_BUCKETS - 1)
        bucket = jnp.where(n < MAX_EXACT, n, large)
        bias = jnp.zeros((blk, blk), F32)
        for b in range(N_BUCKETS):
            bias = jnp.where(bucket == b, tab_ref[b, h] - base, bias)
        mask = rel >= 0
        if window is not None:
            mask = mask & (rel < window)
        o_ref[0, delta] = jnp.where(mask, bias, NEG)


def _bias_tiles(tab, blk, window, rebase):
    heads = tab.shape[1]
    return pl.pallas_call(
        functools.partial(_bias_kernel, blk=blk, window=window, rebase=rebase),
        out_shape=jax.ShapeDtypeStruct((heads, 2, blk, blk), F32),
        grid=(heads,),
        in_specs=[pl.BlockSpec(memory_space=pltpu.SMEM)],
        out_specs=pl.BlockSpec((1, 2, blk, blk), lambda h: (h, 0, 0, 0)),
        compiler_params=_params(("parallel",), _nbytes((2, blk, blk), F32), 4 * _nbytes((blk, blk), F32)),
        name="bias_tiles",
    )(tab)


def _da_kernel(q_ref, k_ref, v_ref, bias_ref, lam_ref, g_ref, o_ref, *, lam_init):
    blk = q_ref.shape[0]
    qi = pl.program_id(2)
    lp = lam_ref[...]
    lam = (jnp.exp(jnp.sum(lp[0:1] * lp[1:2], axis=-1, keepdims=True))
           - jnp.exp(jnp.sum(lp[2:3] * lp[3:4], axis=-1, keepdims=True)) + lam_init)

    def softmax_pv(head, j, case):
        dims = slice((2 * head + j) * HEAD_DIM, (2 * head + j + 1) * HEAD_DIM)
        vdims = slice(head * DA_V_DIM, (head + 1) * DA_V_DIM)
        q = q_ref[:, dims]
        spans = [(slice(case * blk, (case + 1) * blk), bias_ref[head, 0])]
        if case >= 1:
            spans.append((slice((case - 1) * blk, case * blk), bias_ref[head, 1]))
        if case >= 2:
            spans.append((slice(0, (case - 1) * blk), None))
        scores = []
        for rows, bias in spans:
            s = _dot_nt(q, k_ref[rows, dims])
            scores.append(s if bias is None else s + bias)
        m = functools.reduce(jnp.maximum, [jnp.max(s, axis=-1, keepdims=True) for s in scores])
        probs = [jnp.exp(s - m) for s in scores]
        norm = sum(jnp.sum(p, axis=-1, keepdims=True) for p in probs)
        out = sum(jnp.dot(p.astype(BF16), v_ref[rows, vdims], preferred_element_type=F32)
                  for p, (rows, _) in zip(probs, spans))
        return out / norm

    for case in range(k_ref.shape[0] // blk):
        @pl.when(qi == case)
        def _(case=case):
            for head in range(q_ref.shape[1] // DA_V_DIM):
                o = softmax_pv(head, 0, case) - lam * softmax_pv(head, 1, case)
                y = o * lax.rsqrt(jnp.mean(o * o, axis=-1, keepdims=True) + EPS)
                o_ref[:, head * DA_V_DIM:(head + 1) * DA_V_DIM] = (
                    (y * g_ref[...]) * (1.0 - lam_init)).astype(o_ref.dtype)


def _diff_attention(qkv, bias, lam_p, subln_g, lam_init, batch, seq, heads=DA_HEADS_PER_STEP):
    blk = DA_BLK
    nq = seq // blk
    width = heads * DA_V_DIM
    blocks = (2 * _nbytes((blk, width), BF16) + 2 * _nbytes((seq, width), BF16)
              + _nbytes((heads, 2, blk, blk), F32))
    scratch = 6 * heads * _nbytes((blk, seq), F32)
    k_col0 = DA_QK // width
    v_col0 = 2 * DA_QK // width
    return pl.pallas_call(
        functools.partial(_da_kernel, lam_init=lam_init),
        out_shape=jax.ShapeDtypeStruct((batch * seq, DA_V), BF16),
        grid=(batch, DA_HEADS // heads, nq),
        in_specs=[pl.BlockSpec((blk, width), lambda b, h, i: (b * nq + i, h)),
                  pl.BlockSpec((seq, width), lambda b, h, i: (b, k_col0 + h)),
                  pl.BlockSpec((seq, width), lambda b, h, i: (b, v_col0 + h)),
                  pl.BlockSpec((heads, 2, blk, blk), lambda b, h, i: (h, 0, 0, 0)),
                  pl.BlockSpec((4, HEAD_DIM), lambda b, h, i: (0, 0)),
                  pl.BlockSpec((1, DA_V_DIM), lambda b, h, i: (0, 0))],
        out_specs=pl.BlockSpec((blk, width), lambda b, h, i: (b * nq + i, h)),
        compiler_params=_params(("parallel", "parallel", "parallel"), blocks, scratch),
        name="diff_attention",
    )(qkv, qkv, qkv, bias, lam_p, subln_g.reshape(1, DA_V_DIM))


def _swa_kernel(q_ref, kc_ref, kp_ref, vc_ref, vp_ref, bias_ref, sink_ref, o_ref):
    n = pl.program_id(1)
    is_prev = lax.broadcasted_iota(jnp.int32, (1, 2 * WINDOW), 1) < WINDOW
    no_prev = jnp.where(is_prev & (n == 0), NEG, 0.0).astype(F32)
    for hk in range(SW_KV_HEADS):
        cols = slice(hk * HEAD_DIM, (hk + 1) * HEAD_DIM)
        heads = [slice((hk * SW_GROUP + g) * HEAD_DIM, (hk * SW_GROUP + g + 1) * HEAD_DIM) for g in range(SW_GROUP)]
        keys = jnp.concatenate([kp_ref[:, cols], kc_ref[:, cols]], axis=0)
        values = jnp.concatenate([vp_ref[:, cols], vc_ref[:, cols]], axis=0)
        for i in range(q_ref.shape[0] // WINDOW):
            rows = slice(i * WINDOW, (i + 1) * WINDOW)
            band = slice(i * WINDOW, (i + 2) * WINDOW)
            q = jnp.concatenate([q_ref[rows, hd] for hd in heads], axis=0)
            s = _dot_nt(q, keys[band]) + bias_ref[hk]
            if i == 0:
                s = s + no_prev
            sink = sink_ref[hk]
            m = jnp.maximum(jnp.max(s, axis=-1, keepdims=True), sink)
            e = jnp.exp(s - m)
            den = jnp.sum(e, axis=-1, keepdims=True) + jnp.exp(sink - m)
            o = jnp.dot(e.astype(BF16), values[band], preferred_element_type=F32) / den
            for g, hd in enumerate(heads):
                o_ref[rows, hd] = o[g * WINDOW:(g + 1) * WINDOW, :].astype(o_ref.dtype)


def _sliding_attention(qkv, bias_tiles, sinks, batch, seq, group=SW_BLOCKS_PER_STEP):
    nb = seq // WINDOW
    ng = nb // group
    span = group * WINDOW
    q_col = 3 * DA_QK // SW_Q
    k_col = (3 * DA_QK + SW_Q) // SW_KV
    v_col = k_col + 1
    cur = lambda b, n: b * ng + n
    prev = lambda b, n: b * nb + jnp.maximum(n * group - 1, 0)
    rows = SW_GROUP * WINDOW
    bias = bias_tiles.reshape(SW_KV_HEADS, SW_GROUP, 2, WINDOW, WINDOW)[:, :, ::-1]
    bias = bias.transpose(0, 1, 3, 2, 4).reshape(SW_KV_HEADS, rows, 2 * WINDOW)
    sink_cols = jnp.repeat(sinks.astype(F32).reshape(SW_KV_HEADS, SW_GROUP), WINDOW, axis=1).reshape(
        SW_KV_HEADS, rows, 1)
    blocks = (2 * _nbytes((span, SW_Q), BF16) + 2 * _nbytes((span + WINDOW, SW_KV), BF16)
              + _nbytes((SW_KV_HEADS, rows, 2 * WINDOW), F32) + _nbytes((SW_KV_HEADS, rows, 128), F32))
    return pl.pallas_call(
        _swa_kernel,
        out_shape=jax.ShapeDtypeStruct((batch * seq, SW_Q), BF16),
        grid=(batch, ng),
        in_specs=[pl.BlockSpec((span, SW_Q), lambda b, n: (cur(b, n), q_col)),
                  pl.BlockSpec((span, SW_KV), lambda b, n: (cur(b, n), k_col)),
                  pl.BlockSpec((WINDOW, SW_KV), lambda b, n: (prev(b, n), k_col)),
                  pl.BlockSpec((span, SW_KV), lambda b, n: (cur(b, n), v_col)),
                  pl.BlockSpec((WINDOW, SW_KV), lambda b, n: (prev(b, n), v_col)),
                  pl.BlockSpec((SW_KV_HEADS, rows, 2 * WINDOW), lambda b, n: (0, 0, 0)),
                  pl.BlockSpec((SW_KV_HEADS, rows, 1), lambda b, n: (0, 0, 0))],
        out_specs=pl.BlockSpec((span, SW_Q), lambda b, n: (cur(b, n), 0)),
        compiler_params=_params(("parallel", "parallel"), blocks, 16 * group * _nbytes((rows, 2 * WINDOW), F32)),
        name="sliding_attention",
    )(qkv, qkv, qkv, qkv, qkv, bias, sink_cols)


def _merge_kernel(oa_ref, ob_ref, wa_ref, wb_ref, ga_ref, gb_ref, o_ref):
    a = jnp.dot(oa_ref[...], wa_ref[...], preferred_element_type=F32)
    b = jnp.dot(ob_ref[...], wb_ref[...], preferred_element_type=F32)
    o_ref[...] = (ga_ref[...].astype(F32) * a + gb_ref[...].astype(F32) * b).astype(o_ref.dtype)


def _merge(o_a, o_b, w_a, w_b, gates, bm=1024, bn=1024):
    t, ka = o_a.shape
    kb = o_b.shape[1]
    d = w_a.shape[1]
    nj = d // bn
    blocks = (_nbytes((bm, ka), BF16) + _nbytes((bm, kb), BF16) + _nbytes((ka, bn), BF16)
              + _nbytes((kb, bn), BF16) + 2 * _nbytes((bm, bn), gates.dtype) + _nbytes((bm, bn), BF16))
    return pl.pallas_call(
        _merge_kernel,
        out_shape=jax.ShapeDtypeStruct((t, d), BF16),
        grid=(t // bm, nj),
        in_specs=[pl.BlockSpec((bm, ka), lambda i, j: (i, 0)),
                  pl.BlockSpec((bm, kb), lambda i, j: (i, 0)),
                  pl.BlockSpec((ka, bn), lambda i, j: (0, j)),
                  pl.BlockSpec((kb, bn), lambda i, j: (0, j)),
                  pl.BlockSpec((bm, bn), lambda i, j: (i, j)),
                  pl.BlockSpec((bm, bn), lambda i, j: (i, nj + j))],
        out_specs=pl.BlockSpec((bm, bn), lambda i, j: (i, j)),
        compiler_params=_params(("parallel", "parallel"), blocks, 2 * _nbytes((bm, bn), F32)),
        name="merge",
    )(o_a, o_b, w_a, w_b, gates, gates)


def _sort_pairs(n):
    pairs = []

    def merge(lo, hi, r):
        step = r * 2
        if step < hi - lo:
            merge(lo, hi, step)
            merge(lo + r, hi, step)
            pairs.extend((i, i + r) for i in range(lo + r, hi - r, step))
        else:
            pairs.append((lo, lo + r))

    def sort(lo, hi):
        if hi - lo >= 1:
            mid = lo + (hi - lo) // 2
            sort(lo, mid)
            sort(mid + 1, hi)
            merge(lo, hi, 1)

    sort(0, n - 1)
    return pairs


_SORT16 = _sort_pairs(PEER_TOPK)


def _sort_desc(xs):
    xs = list(xs)
    for i, j in _SORT16:
        xs[i], xs[j] = jnp.maximum(xs[i], xs[j]), jnp.minimum(xs[i], xs[j])
    return xs


def _merge_top(a, b):
    k = PEER_TOPK
    xs = [jnp.maximum(a[i], b[k - 1 - i]) for i in range(k)]
    d = k // 2
    while d >= 1:
        for i in range(k):
            if not i & d:
                xs[i], xs[i + d] = jnp.maximum(xs[i], xs[i + d]), jnp.minimum(xs[i], xs[i + d])
        d //= 2
    return xs


def _top16_over_rows(s):
    groups = [s[a * 8:(a + 1) * 8, :] for a in range(s.shape[0] // 8)]
    xs = _sort_desc(groups)
    for shift in (4, 2, 1):
        xs = _merge_top(xs, [pltpu.roll(x, shift, 0) for x in xs])
    return xs


def _count_leading(pred, values):
    n = len(values)

    def pick(lo, hi, taken):
        if not taken:
            return values[(lo + hi) // 2 - 1]
        mid = (lo + hi) // 2
        return jnp.where(taken[0], pick(mid, hi, taken[1:]), pick(lo, mid, taken[1:]))

    taken = []
    step = n // 2
    while step >= 1:
        taken.append(pred(pick(0, n, taken)))
        step //= 2
    total = None
    for i, t in enumerate(taken):
        part = jnp.where(t, float(n >> (i + 1)), 0.0)
        total = part if total is None else total + part
    return total + jnp.where(pred(values[n - 1]), 1.0, 0.0)


def _route_kernel(h_ref, wq_ref, k1_ref, k2_ref, cnt_ref, e1_ref, rank_ref, e2_ref, top_ref):
    q = jnp.dot(h_ref[...], wq_ref[...], preferred_element_type=F32).astype(BF16)
    s1 = _dot_nt(k1_ref[...], q[:, :KEY_DIM])
    s2 = _dot_nt(k2_ref[...], q[:, KEY_DIM:])
    v1 = _top16_over_rows(s1)
    v2 = _top16_over_rows(s2)
    k = PEER_TOPK
    top = [v1[0] + v2[b] for b in range(k)]
    rest = [v1[a] + v2[b] for a in range(1, k) for b in range(k) if (a + 1) * (b + 1) <= k]
    pad = jnp.full(top[0].shape, -jnp.inf, F32)
    rest = rest + [pad] * (-len(rest) % k)
    for g in range(len(rest) // k):
        top = _merge_top(top, _sort_desc(rest[g * k:(g + 1) * k]))
    z = jnp.ones_like(top[0])
    for c in top[1:]:
        z = z + jnp.exp(c - top[0])
    tau = top[k - 1][0:1]
    best = [v[0:1] for v in v2]
    cnt = _count_leading(lambda b: s1 + b >= tau, best)
    rank = _count_leading(lambda b: b > s2, best)
    cnt_ref[0] = cnt
    rank_ref[0] = _pack_rows(rank.astype(BF16))
    e1_ref[0] = jnp.exp(s1 - v1[0][0:1]) / z[0:1]
    e2_ref[0] = _pack_rows(jnp.exp(s2 - v2[0][0:1]).astype(BF16))
    top_ref[0] = 1.0 / z[0:1]


def _route(hn, wq, k1, k2, tm=1024):
    t, d = hn.shape
    wide = jax.ShapeDtypeStruct((PEER_HEADS, N_KEYS, t), F32)
    narrow = jax.ShapeDtypeStruct((PEER_HEADS, N_KEYS // 2, t), jnp.uint32)
    spec = pl.BlockSpec((1, N_KEYS, tm), lambda i, h: (h, 0, i))
    narrow_spec = pl.BlockSpec((1, N_KEYS // 2, tm), lambda i, h: (h, 0, i))
    blocks = _nbytes((tm, d), BF16) + _nbytes((d, 2 * KEY_DIM), BF16) + 3 * _nbytes((N_KEYS, tm), F32)
    return pl.pallas_call(
        _route_kernel,
        out_shape=(wide, wide, narrow, narrow, jax.ShapeDtypeStruct((PEER_HEADS, 1, t), F32)),
        grid=(t // tm, PEER_HEADS),
        in_specs=[pl.BlockSpec((tm, d), lambda i, h: (i, 0)),
                  pl.BlockSpec((d, 2 * KEY_DIM), lambda i, h: (0, h)),
                  pl.BlockSpec((N_KEYS, KEY_DIM), lambda i, h: (0, 0)),
                  pl.BlockSpec((N_KEYS, KEY_DIM), lambda i, h: (0, 0))],
        out_specs=(spec, spec, narrow_spec, narrow_spec, pl.BlockSpec((1, 1, tm), lambda i, h: (h, 0, i))),
        compiler_params=_params(("parallel", "parallel"), blocks, 24 * _nbytes((N_KEYS, tm), F32)),
        name="peer_route",
    )(hn, wq, k1, k2)


def _peer_up_kernel(inv_u_ref, inv_v_ref, u_ref, h_ref, inv_ref, wscale_ref, cnt_ref, e1_ref, rank_ref, e2_ref,
                    o_ref, act_a, act_b, *, expert_tiles):
    s = pl.program_id(0)
    first_scale_block = (jnp.maximum(s - 1, 0) % expert_tiles) * (act_a.shape[0] // FP8_ROW_BLOCK)

    @pl.when(s == 0)
    def _():
        act_a[...] = jnp.zeros(act_a.shape, F32)
        act_b[...] = jnp.zeros(act_b.shape, F32)

    sub = PEER_UP_SUB_ROWS
    reps = sub // BF16_ROWS
    chunks = u_ref.shape[0]
    blocks_per_chunk = act_a.shape[0] // N_KEYS // chunks

    def epilogue_block(act_old, r):
        scale_block = first_scale_block + r // (FP8_ROW_BLOCK // N_KEYS)
        inv_u, inv_v = inv_u_ref[scale_block], inv_v_ref[scale_block]
        cnt_rows = [cnt_ref[h, pl.ds(r, 1), :] for h in range(PEER_HEADS)]
        e1_rows = [e1_ref[h, pl.ds(r, 1), :] * (wscale_ref[...] * inv_v) for h in range(PEER_HEADS)]
        for c in range(act_old.shape[1] // 128):
            cols = slice(c * 128, (c + 1) * 128)
            for part in range(N_KEYS // sub):
                act_rows = pl.ds(pl.multiple_of(r * N_KEYS + part * sub, sub), sub)
                in_rows = slice(part * sub // 2, (part + 1) * sub // 2)
                out_rows = pl.ds(pl.multiple_of((r * N_KEYS + part * sub) // 4, sub // 4), sub // 4)
                act = act_old[act_rows, cols]
                act_old[act_rows, cols] = jnp.zeros_like(act)
                act = act * (inv_ref[:, cols] * inv_u)
                gate = None
                for h in range(PEER_HEADS):
                    cnt = jnp.tile(jnp.broadcast_to(cnt_rows[h][:, cols], (BF16_ROWS, 128)).astype(BF16), (reps, 1))
                    e1 = jnp.tile(jnp.broadcast_to(e1_rows[h][:, cols], (BF16_ROWS, 128)).astype(BF16), (reps, 1))
                    routed = _unpack_rows(rank_ref[h, in_rows, cols]) < cnt
                    term = jnp.where(routed, _unpack_rows(e2_ref[h, in_rows, cols]) * e1, jnp.zeros((), BF16))
                    gate = term if gate is None else gate + term
                w = jax.nn.gelu(act).astype(BF16) * gate
                o_ref[out_rows, cols] = _pack_rows(w.astype(FP8))

    def step(act_new, act_old):
        def chunk(k, carry):
            act_new[...] += _dot_nt(u_ref[k], h_ref[k])
            for b in range(blocks_per_chunk):
                epilogue_block(act_old, k * blocks_per_chunk + b)
            return carry

        lax.fori_loop(0, chunks, chunk, 0, unroll=PEER_UP_UNROLL)

    @pl.when(s % 2 == 0)
    def _():
        step(act_a, act_b)

    @pl.when(s % 2 == 1)
    def _():
        step(act_b, act_a)


def _peer_up(u, inv_u, inv_v, hn, inv_h, w_scale, cnt, e1, rank, e2, te=PEER_UP_TE, tm=512):
    chunks, n_exp, dc = u.shape
    t = hn.shape[1]
    rows = te // N_KEYS
    assert rows % chunks == 0
    nj = n_exp // te
    steps = (t // tm) * nj

    def tile_of(step):
        return step // nj, step % nj

    def now(s):
        return tile_of(jnp.minimum(s, steps - 1))

    def lag(s):
        return tile_of(jnp.maximum(s - 1, 0))

    row_spec = pl.BlockSpec((PEER_HEADS, rows, tm), lambda s: (0, lag(s)[1], lag(s)[0]))
    full_spec = pl.BlockSpec((PEER_HEADS, N_KEYS // 2, tm), lambda s: (0, 0, lag(s)[0]))
    blocks = (_nbytes((chunks, te, dc), FP8) + _nbytes((chunks, tm, dc), FP8)
              + 2 * _nbytes((PEER_HEADS, N_KEYS, tm), BF16)
              + 2 * _nbytes((PEER_HEADS, rows, tm), F32) + _nbytes((te, tm), BF16))
    smem = pl.BlockSpec(memory_space=pltpu.SMEM)
    return pl.pallas_call(
        functools.partial(_peer_up_kernel, expert_tiles=nj),
        out_shape=jax.ShapeDtypeStruct((n_exp // 4, t), jnp.uint32),
        grid=(steps + 1,),
        in_specs=[smem, smem,
                  pl.BlockSpec((chunks, te, dc), lambda s: (0, now(s)[1], 0)),
                  pl.BlockSpec((chunks, tm, dc), lambda s: (0, now(s)[0], 0)),
                  pl.BlockSpec((1, tm), lambda s: (0, lag(s)[0])),
                  pl.BlockSpec((1, tm), lambda s: (0, lag(s)[0])),
                  row_spec, row_spec, full_spec, full_spec],
        out_specs=pl.BlockSpec((te // 4, tm), lambda s: (lag(s)[1], lag(s)[0])),
        scratch_shapes=[pltpu.VMEM((te, tm), F32), pltpu.VMEM((te, tm), F32)],
        compiler_params=_params(("arbitrary",), blocks, 3 * _nbytes((te, tm), F32)),
        name="peer_up",
    )(inv_u, inv_v, u, hn, inv_h, w_scale, cnt, e1, rank, e2)


def _peer_down_kernel(vt_ref, w_ref, inv_ref, x_ref, o_ref, acc_ref):
    kk = pl.program_id(2)

    @pl.when(kk == 0)
    def _():
        acc_ref[...] = jnp.zeros(acc_ref.shape, F32)

    acc_ref[...] += jnp.dot(vt_ref[...], _unpack_rows(w_ref[...], FP8), preferred_element_type=F32)

    @pl.when(kk == pl.num_programs(2) - 1)
    def _():
        o_ref[...] = x_ref[...] + (acc_ref[...] * inv_ref[...]).T


def _peer_down(vt, wt, inv_w_scale, x, bd=1024, bt=1024, tk=4096):
    d, n_exp = vt.shape
    t = wt.shape[1]
    blocks = (_nbytes((bd, tk), FP8) + _nbytes((tk, bt), FP8) + 2 * _nbytes((bt, bd), F32))
    return pl.pallas_call(
        _peer_down_kernel,
        out_shape=jax.ShapeDtypeStruct((t, d), F32),
        grid=(d // bd, t // bt, n_exp // tk),
        in_specs=[pl.BlockSpec((bd, tk), lambda i, j, k: (i, k)),
                  pl.BlockSpec((tk // 4, bt), lambda i, j, k: (k, j)),
                  pl.BlockSpec((1, bt), lambda i, j, k: (0, j)),
                  pl.BlockSpec((bt, bd), lambda i, j, k: (j, i))],
        out_specs=pl.BlockSpec((bt, bd), lambda i, j, k: (j, i)),
        scratch_shapes=[pltpu.VMEM((bd, bt), F32)],
        compiler_params=_params(("parallel", "parallel", "arbitrary"), blocks, 3 * _nbytes((bd, bt), F32)),
        name="peer_down",
    )(vt, wt, inv_w_scale, x)


def _w_scale(h_norm, u_norm, top_weight, inv_v):
    bound = h_norm.reshape(1, -1) * u_norm * jnp.sum(top_weight, axis=0) * jnp.max(inv_v)
    return _pow2_scale(bound)


def _ple_kernel(h_ref, wg_ref, p_ref, wp_ref, x_ref, o_ref):
    gate = jax.nn.sigmoid(jnp.dot(h_ref[...], wg_ref[...], preferred_element_type=F32))
    emb = jnp.dot(p_ref[...], wp_ref[...], preferred_element_type=F32)
    o_ref[...] = x_ref[...] + gate * emb


def _ple(hp, w_gate, p, w_proj, x, bm=1024, bn=1024):
    t, d = hp.shape
    pd = p.shape[1]
    n = w_gate.shape[1]
    blocks = (_nbytes((bm, d), BF16) + _nbytes((d, bn), BF16) + _nbytes((bm, pd), BF16)
              + _nbytes((pd, bn), BF16) + 2 * _nbytes((bm, bn), F32))
    return pl.pallas_call(
        _ple_kernel,
        out_shape=jax.ShapeDtypeStruct((t, n), F32),
        grid=(t // bm, n // bn),
        in_specs=[pl.BlockSpec((bm, d), lambda i, j: (i, 0)),
                  pl.BlockSpec((d, bn), lambda i, j: (0, j)),
                  pl.BlockSpec((bm, pd), lambda i, j: (i, 0)),
                  pl.BlockSpec((pd, bn), lambda i, j: (0, j)),
                  pl.BlockSpec((bm, bn), lambda i, j: (i, j))],
        out_specs=pl.BlockSpec((bm, bn), lambda i, j: (i, j)),
        compiler_params=_params(("parallel", "parallel"), blocks, 2 * _nbytes((bm, bn), F32)),
        name="ple",
    )(hp, w_gate, p, w_proj, x)


def _qkv_column_scale():
    s = HEAD_DIM ** -0.5
    parts = [(DA_QK, s), (DA_QK, 1.0), (DA_V, 1.0), (SW_Q, s), (SW_KV, 1.0), (SW_KV, 1.0)]
    return jnp.concatenate([jnp.full((1, w), v, F32) for w, v in parts], axis=1)


@jax.jit
def kernel(x, p, positions, rel_bias, norm_mix, w_in, da_lambda, da_subln, sw_sinks, w_br_a, w_br_b, w_out,
           norm_ffn, peer_wq, peer_k1, peer_k2, peer_u, peer_v, norm_ple, ple_gate, ple_proj, norm_final):
    del positions
    batch, seq, d = x.shape
    t = batch * seq
    depth = w_in.shape[0]
    xf = x.reshape(t, d)
    da_bias = _bias_tiles(rel_bias[:, :DA_HEADS], DA_BLK, None, True)
    sw_bias = _bias_tiles(rel_bias[:, DA_HEADS:], WINDOW, WINDOW, False)
    col_scale = _qkv_column_scale()
    tile = pl.BlockSpec((1, 1024), lambda i, j: (0, j))
    for i in range(depth):
        lam_init = 0.8 - 0.6 * math.exp(-0.3 * i)
        h = _rmsnorm(xf, norm_mix[i], BF16)
        qkv = _matmul(_mm_scale_kernel, h, _narrow(w_in, i, 0, QKV_WIDTH, 1024), [col_scale], [tile],
                      BF16, 1024, 1024, "proj_qkv")
        gates = _matmul(_mm_sigmoid_kernel, h, _narrow(w_in, i, QKV_WIDTH, w_in.shape[2] - QKV_WIDTH, 1024),
                        [], [], BF16, 1024, 1024, "proj_gates")
        o_a = _diff_attention(qkv, da_bias, da_lambda[i], da_subln[i], lam_init, batch, seq)
        o_b = _sliding_attention(qkv, sw_bias, sw_sinks[i], batch, seq)
        merged = _merge(o_a, o_b, _narrow(w_br_a, i), _narrow(w_br_b, i), gates)
        xf = _matmul(_mm_residual_kernel, merged, _narrow(w_out, i), [xf],
                     [pl.BlockSpec((1024, 1024), lambda i, j: (i, j))], F32, 1024, 1024, "proj_out")
        hn, hn8, hn_inv, hn_norm = _rmsnorm_fp8(xf, norm_ffn[i], PEER_UP_CHUNKS)
        cnt, e1, rank, e2, top = _route(hn, _narrow(peer_wq, i), peer_k1[i].astype(BF16), peer_k2[i].astype(BF16))
        u8, inv_u, u_norm = _fp8_rows(peer_u, i, "chunked", PEER_UP_CHUNKS)
        vt8, inv_v, _ = _fp8_rows(peer_v, i, "transposed")
        w_scale, inv_w_scale = _w_scale(hn_norm, u_norm, top, inv_v)
        wt = _peer_up(u8, inv_u, inv_v, hn8, hn_inv.reshape(1, t), w_scale, cnt, e1, rank, e2)
        xf = _peer_down(vt8, wt, inv_w_scale, xf)
        hp = _rmsnorm(xf, norm_ple[i], BF16)
        xf = _ple(hp, _narrow(ple_gate, i), p[i].reshape(t, -1).astype(BF16), ple_proj[i].astype(BF16), xf)
    return _rmsnorm(xf, norm_final, F32).reshape(batch, seq, d)
```
